```python
import jax, jax.numpy as jnp
from jax import lax
import numpy as np

D_MODEL = 1024
BATCH = 4
SEQ = 4096
DEPTH = 1
DEC_BATCH = 16
DEC_SEQ = 2048
PAST_LEN = 128

MIX_WIDTH = D_MODEL
RET_WIDTH = MIX_WIDTH // 2
MLSTM_WIDTH = MIX_WIDTH - RET_WIDTH
N_RET_HEADS = 4
RET_HEAD_DIM = RET_WIDTH // N_RET_HEADS
N_MLSTM_HEADS = 4
MLSTM_HEAD_DIM = MLSTM_WIDTH // N_MLSTM_HEADS
CHUNK = 128
ROPE_BASE = 10000.0
N_EXPERTS = 16
EXPERT_CAP_FACTOR = 2
D_FF_EXPERT = 2 * D_MODEL
LN_EPS = 1e-5
NEG_BIG = -1e30
DEEPNORM_ALPHA = (2.0 * DEPTH) ** 0.25
DEEPNORM_BETA = (8.0 * DEPTH) ** -0.25
SPLIT_SIZES = (RET_WIDTH,) * 4 + (MLSTM_WIDTH,) * 4 + (N_MLSTM_HEADS,) * 4
PROJ_WIDTH = sum(SPLIT_SIZES)
SPLIT_POINTS = tuple(sum(SPLIT_SIZES[: i + 1]) for i in range(len(SPLIT_SIZES) - 1))
GATE_OFF = 4 * RET_WIDTH + 4 * MLSTM_WIDTH
F_FWD_OFF = GATE_OFF + N_MLSTM_HEADS
F_BWD_OFF = GATE_OFF + 3 * N_MLSTM_HEADS

kernel_name = "hybrid_retention_mlstm_ec_encoder"


def layer_norm(x, g, b):
    xf = x.astype(jnp.float32)
    mu = xf.mean(-1, keepdims=True)
    var = jnp.square(xf - mu).mean(-1, keepdims=True)
    y = (xf - mu) * lax.rsqrt(var + LN_EPS) * g.astype(jnp.float32) + b.astype(jnp.float32)
    return y.astype(x.dtype)


def head_norm(h, g):
    mu = h.mean(-1, keepdims=True)
    var = jnp.square(h - mu).mean(-1, keepdims=True)
    hn = (h - mu) * lax.rsqrt(var + LN_EPS)
    return hn.reshape(h.shape[0], h.shape[1], -1) * g.astype(jnp.float32)


def rotary(x):
    S, Dh = x.shape[1], x.shape[-1]
    half = Dh // 2
    inv = 1.0 / (ROPE_BASE ** (jnp.arange(half, dtype=jnp.float32) / half))
    ang = jnp.arange(S, dtype=jnp.float32)[:, None] * inv[None, :]
    cos = jnp.cos(ang)[None, :, None, :]
    sin = jnp.sin(ang)[None, :, None, :]
    x1, x2 = x[..., :half], x[..., half:]
    return jnp.concatenate([x1 * cos - x2 * sin, x1 * sin + x2 * cos], axis=-1)


def to_chunks(t):
    B, S, H, D = t.shape
    return t.transpose(0, 2, 1, 3).reshape(B, H, S // CHUNK, CHUNK, D)


def gate_chunks(t):
    B, S, H = t.shape
    return t.transpose(0, 2, 1).reshape(B, H, S // CHUNK, CHUNK)


def from_chunks(y):
    B, H, N, L, D = y.shape
    return y.reshape(B, H, N * L, D).transpose(0, 2, 1, 3)


def flip_seq(t):
    return jnp.flip(t, axis=1)


def retention_direction(q, k, v, include_diag):
    B, S, H, Dh = q.shape
    L = CHUNK
    qc, kc, vc = to_chunks(q), to_chunks(k), to_chunks(v)
    log_g = jnp.log1p(-jnp.exp2(-5.0 - jnp.arange(H, dtype=jnp.float32)))
    pos = jnp.arange(L, dtype=jnp.float32)
    diff = pos[:, None] - pos[None, :]
    mask = (diff >= 0) if include_diag else (diff > 0)
    decay = jnp.where(mask[None], jnp.exp(log_g[:, None, None] * jnp.where(mask, diff, 0.0)[None]), 0.0)
    scores = jnp.einsum("bhnid,bhnjd->bhnij", qc, kc) * decay[None, :, None]
    y = jnp.einsum("bhnij,bhnje->bhnie", scores, vc)
    k_w = jnp.exp(log_g[:, None] * (L - 1 - pos)[None, :])
    u = jnp.einsum("bhnjd,bhnje->nbhde", kc * k_w[None, :, None, :, None], vc)
    g_chunk = jnp.exp(log_g * L)[None, :, None, None]

    def step(s, u_n):
        return g_chunk * s + u_n, s

    _, s_prev = lax.scan(step, jnp.zeros((B, H, Dh, vc.shape[-1]), jnp.float32), u)
    q_w = jnp.exp(log_g[:, None] * (pos + 1.0)[None, :])
    y = y + jnp.einsum("bhnid,nbhde->bhnie", qc * q_w[None, :, None, :, None], s_prev)
    return from_chunks(y)


def mlstm_direction(q, k, v, i_pre, log_f):
    B, S, H, Dh = q.shape
    L = CHUNK
    qc, kc, vc = to_chunks(q), to_chunks(k), to_chunks(v)
    li = gate_chunks(i_pre)
    b = jnp.cumsum(gate_chunks(log_f), axis=-1)
    b_last = b[..., -1]
    causal = jnp.tril(jnp.ones((L, L), dtype=bool))
    log_d = jnp.where(causal, b[..., :, None] - b[..., None, :] + li[..., None, :], NEG_BIG)
    a = b_last[..., None] - b + li
    a_max = a.max(-1)
    w = jnp.exp(a - a_max[..., None])
    u_c = jnp.einsum("bhnl,bhnld,bhnle->nbhde", w, kc, vc)
    u_n = jnp.einsum("bhnl,bhnld->nbhd", w, kc)

    def step(carry, inp):
        c, n, m = carry
        uc, un, am, bl = inp
        m_new = jnp.maximum(bl + m, am)
        s_old = jnp.exp(bl + m - m_new)
        s_new = jnp.exp(am - m_new)
        c_new = s_old[..., None, None] * c + s_new[..., None, None] * uc
        n_new = s_old[..., None] * n + s_new[..., None] * un
        return (c_new, n_new, m_new), (c, n, m)

    init = (jnp.zeros((B, H, Dh, vc.shape[-1]), jnp.float32),
            jnp.zeros((B, H, Dh), jnp.float32),
            jnp.full((B, H), NEG_BIG, jnp.float32))
    xs = (u_c, u_n, jnp.moveaxis(a_max, 2, 0), jnp.moveaxis(b_last, 2, 0))
    _, (c_prev, n_prev, m_prev) = lax.scan(step, init, xs)
    inter_log = b + jnp.moveaxis(m_prev, 0, 2)[..., None]
    m_row = jnp.maximum(log_d.max(-1), inter_log)
    d_w = jnp.exp(log_d - m_row[..., None])
    s_inter = jnp.exp(inter_log - m_row)
    qk = jnp.einsum("bhnid,bhnjd->bhnij", qc, kc) * d_w
    num = jnp.einsum("bhnij,bhnje->bhnie", qk, vc) + s_inter[..., None] * jnp.einsum("bhnid,nbhde->bhnie", qc, c_prev)
    den = qk.sum(-1) + s_inter * jnp.einsum("bhnid,nbhd->bhni", qc, n_prev)
    h = num / jnp.maximum(jnp.abs(den), jnp.exp(-m_row))[..., None]
    return from_chunks(h)


def hybrid_mixer(xn, w_in, b_in, ret_norm_g, mlstm_norm_g, w_o):
    B, S, _ = xn.shape
    proj = (jnp.einsum("bsd,dp->bsp", xn, w_in) + b_in).astype(jnp.float32)
    rq, rk, rv, rg, mq, mk, mv, mo, i_f, f_f, i_b, f_b = jnp.split(proj, SPLIT_POINTS, axis=-1)
    rq = rotary(rq.reshape(B, S, N_RET_HEADS, RET_HEAD_DIM))
    rk = rotary(rk.reshape(B, S, N_RET_HEADS, RET_HEAD_DIM)) * (RET_HEAD_DIM ** -0.5)
    rv = rv.reshape(B, S, N_RET_HEADS, RET_HEAD_DIM)
    y_ret = retention_direction(rq, rk, rv, True) + flip_seq(
        retention_direction(flip_seq(rq), flip_seq(rk), flip_seq(rv), False))
    y_ret = head_norm(y_ret, ret_norm_g) * jax.nn.silu(rg)
    mq = mq.reshape(B, S, N_MLSTM_HEADS, MLSTM_HEAD_DIM)
    mk = mk.reshape(B, S, N_MLSTM_HEADS, MLSTM_HEAD_DIM) * (MLSTM_HEAD_DIM ** -0.5)
    mv = mv.reshape(B, S, N_MLSTM_HEADS, MLSTM_HEAD_DIM)
    h_fwd = mlstm_direction(mq, mk, mv, i_f, jax.nn.log_sigmoid(f_f))
    h_bwd = flip_seq(mlstm_direction(flip_seq(mq), flip_seq(mk), flip_seq(mv),
                                     flip_seq(i_b), jax.nn.log_sigmoid(flip_seq(f_b))))
    y_m = head_norm(h_fwd + h_bwd, mlstm_norm_g) * jax.nn.sigmoid(mo)
    mixed = jnp.concatenate([y_ret, y_m], axis=-1).astype(xn.dtype)
    return jnp.einsum("bsm,md->bsd", mixed, w_o)


def expert_choice_ffn(x, w_router, w_gate, w_up, w_down):
    B, S, D = x.shape
    x2d = x.reshape(B * S, D)
    cap = EXPERT_CAP_FACTOR * (B * S) // N_EXPERTS
    aff = jax.nn.softmax(jnp.einsum("td,de->te", x2d, w_router).astype(jnp.float32), axis=-1)
    gates, idx = lax.top_k(aff.T, cap)
    xe = x2d[idx]
    hid = jax.nn.silu(jnp.einsum("ecd,edf->ecf", xe, w_gate)) * jnp.einsum("ecd,edf->ecf", xe, w_up)
    ye = jnp.einsum("ecf,efd->ecd", hid, w_down) * gates[..., None].astype(x.dtype)
    out = jnp.zeros_like(x2d).at[idx.reshape(-1)].add(ye.reshape(-1, D))
    return out.reshape(B, S, D)


def encoder_trunk(x, ln_in_g, ln_in_b, w_in, b_in, ret_norm_g, mlstm_norm_g, w_o,
                  ln1_g, ln1_b, w_router, w_gate, w_up, w_down, ln2_g, ln2_b):
    h = layer_norm(x, ln_in_g, ln_in_b)
    for l in range(DEPTH):
        mix = hybrid_mixer(h, w_in[l], b_in[l], ret_norm_g[l], mlstm_norm_g[l], w_o[l])
        h = layer_norm(DEEPNORM_ALPHA * h + mix, ln1_g[l], ln1_b[l])
        ffn = expert_choice_ffn(h, w_router[l], w_gate[l], w_up[l], w_down[l])
        h = layer_norm(DEEPNORM_ALPHA * h + ffn, ln2_g[l], ln2_b[l])
    return h


def setup_inputs(seed: int = 0) -> dict:
    key = jax.random.key(seed)
    ks = jax.random.split(key, 20)
    f32 = jnp.float32
    D, P, E, F = D_MODEL, PROJ_WIDTH, N_EXPERTS, D_FF_EXPERT
    nrm = lambda k, shape, s: jax.random.normal(k, shape, f32) * s
    forget_init = jnp.linspace(3.0, 6.0, N_MLSTM_HEADS, dtype=f32)
    b_in = nrm(ks[4], (DEPTH, P), 0.02)
    b_in = b_in.at[:, F_FWD_OFF:F_FWD_OFF + N_MLSTM_HEADS].add(forget_init)
    b_in = b_in.at[:, F_BWD_OFF:F_BWD_OFF + N_MLSTM_HEADS].add(forget_init)
    return {
        "x_prompt": jax.random.normal(ks[0], (BATCH, SEQ, D), f32),
        "x_sample": jax.random.normal(ks[1], (DEC_BATCH, DEC_SEQ, D), f32),
        "ln_in_g": 1.0 + nrm(ks[2], (D,), 0.02),
        "ln_in_b": nrm(ks[3], (D,), 0.02),
        "w_in": nrm(ks[5], (DEPTH, D, P), D ** -0.5),
        "b_in": b_in,
        "ret_norm_g": 1.0 + nrm(ks[6], (DEPTH, RET_WIDTH), 0.02),
        "mlstm_norm_g": 1.0 + nrm(ks[7], (DEPTH, MLSTM_WIDTH), 0.02),
        "w_o": nrm(ks[8], (DEPTH, MIX_WIDTH, D), DEEPNORM_BETA * MIX_WIDTH ** -0.5),
        "ln1_g": 1.0 + nrm(ks[9], (DEPTH, D), 0.02),
        "ln1_b": nrm(ks[10], (DEPTH, D), 0.02),
        "w_router": nrm(ks[11], (DEPTH, D, E), D ** -0.5),
        "w_gate": nrm(ks[12], (DEPTH, E, D, F), D ** -0.5),
        "w_up": nrm(ks[13], (DEPTH, E, D, F), D ** -0.5),
        "w_down": nrm(ks[14], (DEPTH, E, F, D), DEEPNORM_BETA * F ** -0.5),
        "ln2_g": 1.0 + nrm(ks[15], (DEPTH, D), 0.02),
        "ln2_b": nrm(ks[16], (DEPTH, D), 0.02),
    }


def reference(x_prompt, x_sample, ln_in_g, ln_in_b, w_in, b_in, ret_norm_g, mlstm_norm_g, w_o,
              ln1_g, ln1_b, w_router, w_gate, w_up, w_down, ln2_g, ln2_b):
    y_prompt = encoder_trunk(x_prompt, ln_in_g, ln_in_b, w_in, b_in, ret_norm_g, mlstm_norm_g, w_o,
                             ln1_g, ln1_b, w_router, w_gate, w_up, w_down, ln2_g, ln2_b)
    y_sample = encoder_trunk(x_sample, ln_in_g, ln_in_b, w_in, b_in, ret_norm_g, mlstm_norm_g, w_o,
                             ln1_g, ln1_b, w_router, w_gate, w_up, w_down, ln2_g, ln2_b)
    return (y_prompt, y_sample)
```

```python
import functools

import jax
import jax.numpy as jnp
from jax import lax
from jax.experimental import pallas as pl
from jax.experimental.pallas import tpu as pltpu

D_MODEL = 1024
N_HEADS = 4
HEAD_DIM = 128
SEC = N_HEADS * HEAD_DIM
CHUNK = 128
N_EXPERTS = 16
D_FF = 2 * D_MODEL
CAP_FACTOR = 2
ROPE_BASE = 10000.0
LN_EPS = 1e-5
NEG_BIG = -1e30
DEPTH = 1
ALPHA = (2.0 * DEPTH) ** 0.25
K_SCALE = HEAD_DIM ** -0.5
LANES = 128
SUBLANES = 8
TOK_BLOCK = 128
VMEM_LIMIT = 56 * 1024 * 1024

_MXU = jnp.bfloat16
_F32 = jnp.float32


def _dot(a, b):
    return jnp.dot(a, b, preferred_element_type=_F32)


def _dot_nt(a, b):
    return lax.dot_general(a, b, (((1,), (1,)), ((), ())), preferred_element_type=_F32)


def _split3(x):
    x1 = x.astype(_MXU)
    r1 = x - x1.astype(_F32)
    x2 = r1.astype(_MXU)
    r2 = r1 - x2.astype(_F32)
    return x1, x2, r2.astype(_MXU)


def _dot01_left(a01, x):
    x1, x2, x3 = _split3(x)
    return _dot(a01, x1) + _dot(a01, x2) + _dot(a01, x3)


def _dot01_right(x, a01):
    x1, x2, x3 = _split3(x)
    return _dot(x1, a01) + _dot(x2, a01) + _dot(x3, a01)


def _layer_norm(x, g, b):
    mu = jnp.mean(x, axis=-1, keepdims=True)
    xc = x - mu
    var = jnp.mean(xc * xc, axis=-1, keepdims=True)
    return xc * lax.rsqrt(var + LN_EPS) * g + b


def _log_sigmoid(x):
    return jnp.minimum(x, 0.0) - jnp.log1p(jnp.exp(-jnp.abs(x)))


def _sigmoid(x):
    return 1.0 / (1.0 + jnp.exp(-x))


def _params(sem):
    return pltpu.CompilerParams(dimension_semantics=sem, vmem_limit_bytes=VMEM_LIMIT)


_P_COL = {0: 0, 1: 1, 2: 2, 4: 3, 5: 4, 6: 5}
_G2_COL = {3: 0, 7: 1}


def _inproj_kernel(x_ref, lg_ref, lb_ref, w_ref, b_ref, wg_ref, bg_ref, wgt_ref, bgt_ref,
                   cos_ref, sin_ref, h0_ref, p_ref, g2_ref, gc_ref, gr_ref):
    tm = x_ref.shape[0]
    h = _layer_norm(x_ref[...], lg_ref[...], lb_ref[...])
    h0_ref[...] = h
    hb = h.astype(_MXU)
    cos = cos_ref[...]
    sin = sin_ref[...]
    for sec in range(8):
        acc = _dot(hb, w_ref[:, sec * SEC:(sec + 1) * SEC]) + b_ref[:, sec * SEC:(sec + 1) * SEC]
        if sec in (0, 1):
            c0 = _P_COL[sec] * SEC
            for hh in range(N_HEADS):
                s = acc[:, hh * HEAD_DIM:(hh + 1) * HEAD_DIM]
                r = s * cos + pltpu.roll(s, HEAD_DIM // 2, 1) * sin
                if sec == 1:
                    r = r * K_SCALE
                p_ref[:, c0 + hh * HEAD_DIM:c0 + (hh + 1) * HEAD_DIM] = r.astype(p_ref.dtype)
        elif sec in _P_COL:
            if sec == 5:
                acc = acc * K_SCALE
            c0 = _P_COL[sec] * SEC
            p_ref[:, c0:c0 + SEC] = acc.astype(p_ref.dtype)
        else:
            c0 = _G2_COL[sec] * SEC
            g2_ref[:, c0:c0 + SEC] = acc

    pre = _dot(hb, wg_ref[...]) + bg_ref[...]
    pre_t = _dot_nt(wgt_ref[...], hb) + bgt_ref[...]
    row = lax.broadcasted_iota(jnp.int32, (CHUNK, CHUNK), 0)
    col = lax.broadcasted_iota(jnp.int32, (CHUNK, CHUNK), 1)
    tri_le = (col <= row).astype(_MXU)
    tri_ge = (col >= row).astype(_MXU)
    for c in range(tm // CHUNK):
        sl = slice(c * CHUNK, (c + 1) * CHUNK)
        blk = pre[sl, :]
        ls = _log_sigmoid(blk)
        pref = _dot01_left(tri_le, ls)
        suf = _dot01_left(tri_ge, ls)
        gc_ref[sl, :] = jnp.where((col >= 4) & (col < 8), pref,
                                  jnp.where((col >= 12) & (col < 16), suf, blk))
        blk_t = pre_t[:, sl]
        ls_t = _log_sigmoid(blk_t)
        pref_t = _dot01_right(ls_t, tri_ge)
        suf_t = _dot01_right(ls_t, tri_le)
        gr_ref[:, sl] = jnp.where((row >= 4) & (row < 8), pref_t,
                                  jnp.where((row >= 12) & (row < 16), suf_t, blk_t))


def _inproj(x2, seq, ln_g, ln_b, w_main, b_main, wg, bg, wgt, bgt, cos, sin, tm=512):
    T = x2.shape[0]
    nseq = seq // tm
    const = lambda i: (0, 0)
    return pl.pallas_call(
        _inproj_kernel,
        grid=(T // tm,),
        in_specs=[
            pl.BlockSpec((tm, D_MODEL), lambda i: (i, 0)),
            pl.BlockSpec((1, D_MODEL), const),
            pl.BlockSpec((1, D_MODEL), const),
            pl.BlockSpec((D_MODEL, 8 * SEC), const),
            pl.BlockSpec((1, 8 * SEC), const),
            pl.BlockSpec((D_MODEL, LANES), const),
            pl.BlockSpec((1, LANES), const),
            pl.BlockSpec((LANES, D_MODEL), const),
            pl.BlockSpec((LANES, 1), const),
            pl.BlockSpec((tm, HEAD_DIM), lambda i: (i % nseq, 0)),
            pl.BlockSpec((tm, HEAD_DIM), lambda i: (i % nseq, 0)),
        ],
        out_specs=[
            pl.BlockSpec((tm, D_MODEL), lambda i: (i, 0)),
            pl.BlockSpec((tm, 6 * SEC), lambda i: (i, 0)),
            pl.BlockSpec((tm, 2 * SEC), lambda i: (i, 0)),
            pl.BlockSpec((tm, LANES), lambda i: (i, 0)),
            pl.BlockSpec((LANES, tm), lambda i: (0, i)),
        ],
        out_shape=[
            jax.ShapeDtypeStruct((T, D_MODEL), _F32),
            jax.ShapeDtypeStruct((T, 6 * SEC), _MXU),
            jax.ShapeDtypeStruct((T, 2 * SEC), _F32),
            jax.ShapeDtypeStruct((T, LANES), _F32),
            jax.ShapeDtypeStruct((LANES, T), _F32),
        ],
        compiler_params=_params(("parallel",)),
        name="inproj",
    )(x2, ln_g, ln_b, w_main, b_main, wg, bg, wgt, bgt, cos, sin)


def _init_state(s_ref, c_ref, n_ref, m_ref):
    s_ref[...] = jnp.zeros(s_ref.shape, _F32)
    c_ref[...] = jnp.zeros(c_ref.shape, _F32)
    n_ref[...] = jnp.zeros(n_ref.shape, _F32)
    m_ref[...] = jnp.full(m_ref.shape, NEG_BIG, _F32)


def _mlstm_direction(q, k, v, gc, gr, ch_i, ch_f, last_lane, mask, c_ref, n_ref, m_ref, h):
    cum_col = gc[:, ch_f:ch_f + 1]
    li_col = gc[:, ch_i:ch_i + 1]
    cum_row = gr[ch_f:ch_f + 1, :]
    li_row = gr[ch_i:ch_i + 1, :]
    cum_last = cum_row[:, last_lane:last_lane + 1]
    m_prev = m_ref[h:h + 1, 0:1]
    c_prev = c_ref[h]
    n_prev = n_ref[h:h + 1, :]

    log_d = jnp.where(mask, cum_col - cum_row + li_row, NEG_BIG)
    inter = cum_col + m_prev
    m_row = jnp.maximum(jnp.max(log_d, axis=1, keepdims=True), inter)
    d_w = jnp.exp(log_d - m_row)
    s_inter = jnp.exp(inter - m_row)
    qk = _dot_nt(q, k) * d_w
    num = _dot(qk.astype(_MXU), v) + s_inter * _dot(q, c_prev.astype(_MXU))
    den = (jnp.sum(qk, axis=1, keepdims=True)
           + s_inter * jnp.sum(q.astype(_F32) * n_prev, axis=1, keepdims=True))
    h_out = num / jnp.maximum(jnp.abs(den), jnp.exp(-m_row))

    a_row = cum_last - cum_row + li_row
    a_max = jnp.max(a_row, axis=1, keepdims=True)
    w_col = jnp.exp(cum_last - cum_col + li_col - a_max)
    kw = k.astype(_F32) * w_col
    u_c = _dot(kw.T.astype(_MXU), v)
    u_n = jnp.sum(kw, axis=0, keepdims=True)
    m_new = jnp.maximum(cum_last + m_prev, a_max)
    s_old = jnp.exp(cum_last + m_prev - m_new)
    s_new = jnp.exp(a_max - m_new)
    c_ref[h] = s_old * c_prev + s_new * u_c
    n_ref[h:h + 1, :] = s_old * n_prev + s_new * u_n
    m_ref[h:h + 1, :] = jnp.broadcast_to(m_new, (1, LANES))
    return h_out


def _retention_state_update(k, v, kw_tab, gl_tab, s_ref, h):
    kw = k.astype(_F32) * kw_tab[h]
    s_ref[h] = gl_tab[h] * s_ref[h] + _dot(kw.T.astype(_MXU), v)


def _sweep_bwd_kernel(rq_ref, rk_ref, rv_ref, mq_ref, mk_ref, mv_ref, gc_ref, gr_ref,
                      qwb_ref, kwb_ref, gl_ref, yb_ref, s_ref, c_ref, n_ref, m_ref):
    @pl.when(pl.program_id(1) == 0)
    def _():
        _init_state(s_ref, c_ref, n_ref, m_ref)

    gc = gc_ref[...]
    gr = gr_ref[...]
    row = lax.broadcasted_iota(jnp.int32, (CHUNK, CHUNK), 0)
    col = lax.broadcasted_iota(jnp.int32, (CHUNK, CHUNK), 1)
    mask = col >= row
    for h in range(N_HEADS):
        sl = slice(h * HEAD_DIM, (h + 1) * HEAD_DIM)
        q, k, v = rq_ref[:, sl], rk_ref[:, sl], rv_ref[:, sl]
        qs = (q.astype(_F32) * qwb_ref[h]).astype(_MXU)
        yb_ref[:, sl] = _dot(qs, s_ref[h].astype(_MXU))
        _retention_state_update(k, v, kwb_ref, gl_ref, s_ref, h)
    for h in range(N_HEADS):
        sl = slice(h * HEAD_DIM, (h + 1) * HEAD_DIM)
        h_b = _mlstm_direction(mq_ref[:, sl], mk_ref[:, sl], mv_ref[:, sl], gc, gr,
                               8 + h, 12 + h, 0, mask, c_ref, n_ref, m_ref, h)
        yb_ref[:, SEC + h * HEAD_DIM:SEC + (h + 1) * HEAD_DIM] = h_b


def _sweep_fwd_kernel(rq_ref, rk_ref, rv_ref, mq_ref, mk_ref, mv_ref, gc_ref, gr_ref,
                      yb_ref, g2_ref, rng_ref, mng_ref, dsym_ref, qwf_ref, kwf_ref, gl_ref,
                      mixed_ref, s_ref, c_ref, n_ref, m_ref):
    @pl.when(pl.program_id(1) == 0)
    def _():
        _init_state(s_ref, c_ref, n_ref, m_ref)

    gc = gc_ref[...]
    gr = gr_ref[...]
    row = lax.broadcasted_iota(jnp.int32, (CHUNK, CHUNK), 0)
    col = lax.broadcasted_iota(jnp.int32, (CHUNK, CHUNK), 1)
    mask = col <= row

    def head_norm(y):
        mu = jnp.mean(y, axis=1, keepdims=True)
        yc = y - mu
        var = jnp.mean(yc * yc, axis=1, keepdims=True)
        return yc * lax.rsqrt(var + LN_EPS)

    for h in range(N_HEADS):
        sl = slice(h * HEAD_DIM, (h + 1) * HEAD_DIM)
        q, k, v = rq_ref[:, sl], rk_ref[:, sl], rv_ref[:, sl]
        p = (_dot_nt(q, k) * dsym_ref[h]).astype(_MXU)
        qs = (q.astype(_F32) * qwf_ref[h]).astype(_MXU)
        y = _dot(p, v) + _dot(qs, s_ref[h].astype(_MXU)) + yb_ref[:, sl]
        _retention_state_update(k, v, kwf_ref, gl_ref, s_ref, h)
        g = g2_ref[:, sl]
        out = head_norm(y) * rng_ref[:, sl] * (g * _sigmoid(g))
        mixed_ref[:, sl] = out.astype(mixed_ref.dtype)
    for h in range(N_HEADS):
        sl = slice(h * HEAD_DIM, (h + 1) * HEAD_DIM)
        sl2 = slice(SEC + h * HEAD_DIM, SEC + (h + 1) * HEAD_DIM)
        h_f = _mlstm_direction(mq_ref[:, sl], mk_ref[:, sl], mv_ref[:, sl], gc, gr,
                               h, 4 + h, CHUNK - 1, mask, c_ref, n_ref, m_ref, h)
        out = head_norm(h_f + yb_ref[:, sl2]) * mng_ref[:, sl] * _sigmoid(g2_ref[:, sl2])
        mixed_ref[:, sl2] = out.astype(mixed_ref.dtype)


def _state_scratch():
    return [
        pltpu.VMEM((N_HEADS, HEAD_DIM, HEAD_DIM), _F32),
        pltpu.VMEM((N_HEADS, HEAD_DIM, HEAD_DIM), _F32),
        pltpu.VMEM((SUBLANES, HEAD_DIM), _F32),
        pltpu.VMEM((SUBLANES, LANES), _F32),
    ]


def _sweep_specs(nchunk, reverse):
    def rb(b, n):
        return b * nchunk + ((nchunk - 1 - n) if reverse else n)
    p_specs = [pl.BlockSpec((CHUNK, SEC), functools.partial(lambda b, n, s: (rb(b, n), s), s=s))
               for s in range(6)]
    gc_spec = pl.BlockSpec((CHUNK, LANES), lambda b, n: (rb(b, n), 0))
    gr_spec = pl.BlockSpec((LANES, CHUNK), lambda b, n: (0, rb(b, n)))
    wide = pl.BlockSpec((CHUNK, 2 * SEC), lambda b, n: (rb(b, n), 0))
    tab = pl.BlockSpec((N_HEADS, CHUNK, HEAD_DIM), lambda b, n: (0, 0, 0))
    return p_specs, gc_spec, gr_spec, wide, tab


def _sweep_bwd(P, GC, GR, qwb, kwb, gl, batch, seq):
    T = P.shape[0]
    nchunk = seq // CHUNK
    p_specs, gc_spec, gr_spec, wide, tab = _sweep_specs(nchunk, True)
    return pl.pallas_call(
        _sweep_bwd_kernel,
        grid=(batch, nchunk),
        in_specs=p_specs + [gc_spec, gr_spec, tab, tab, tab],
        out_specs=wide,
        out_shape=jax.ShapeDtypeStruct((T, 2 * SEC), _F32),
        scratch_shapes=_state_scratch(),
        compiler_params=_params(("parallel", "arbitrary")),
        name="sweep_bwd",
    )(P, P, P, P, P, P, GC, GR, qwb, kwb, gl)


def _sweep_fwd(P, G2, GC, GR, YB, rng, mng, dsym, qwf, kwf, gl, batch, seq):
    T = P.shape[0]
    nchunk = seq // CHUNK
    p_specs, gc_spec, gr_spec, wide, tab = _sweep_specs(nchunk, False)
    gain = pl.BlockSpec((1, SEC), lambda b, n: (0, 0))
    return pl.pallas_call(
        _sweep_fwd_kernel,
        grid=(batch, nchunk),
        in_specs=p_specs + [gc_spec, gr_spec, wide, wide, gain, gain, tab, tab, tab, tab],
        out_specs=wide,
        out_shape=jax.ShapeDtypeStruct((T, 2 * SEC), _MXU),
        scratch_shapes=_state_scratch(),
        compiler_params=_params(("parallel", "arbitrary")),
        name="sweep_fwd",
    )(P, P, P, P, P, P, GC, GR, YB, G2, rng, mng, dsym, qwf, kwf, gl)


def _outproj_kernel(mixed_ref, h0_ref, wo_ref, lg_ref, lb_ref, wr_ref, h1_ref, aff_ref):
    z = ALPHA * h0_ref[...] + _dot(mixed_ref[...], wo_ref[...])
    h1 = _layer_norm(z, lg_ref[...], lb_ref[...])
    h1_ref[...] = h1
    logits = _dot(h1.astype(_MXU), wr_ref[...])
    lane = lax.broadcasted_iota(jnp.int32, logits.shape, 1)
    valid = lane < N_EXPERTS
    logits = jnp.where(valid, logits, NEG_BIG)
    e = jnp.exp(logits - jnp.max(logits, axis=1, keepdims=True))
    aff = e / jnp.sum(e, axis=1, keepdims=True)
    aff_ref[...] = jnp.where(valid, aff, 0.0)


def _outproj(mixed, h0, wo, ln_g, ln_b, wr, tm=512):
    T = mixed.shape[0]
    const = lambda i: (0, 0)
    return pl.pallas_call(
        _outproj_kernel,
        grid=(T // tm,),
        in_specs=[
            pl.BlockSpec((tm, D_MODEL), lambda i: (i, 0)),
            pl.BlockSpec((tm, D_MODEL), lambda i: (i, 0)),
            pl.BlockSpec((D_MODEL, D_MODEL), const),
            pl.BlockSpec((1, D_MODEL), const),
            pl.BlockSpec((1, D_MODEL), const),
            pl.BlockSpec((D_MODEL, LANES), const),
        ],
        out_specs=[
            pl.BlockSpec((tm, D_MODEL), lambda i: (i, 0)),
            pl.BlockSpec((tm, LANES), lambda i: (i, 0)),
        ],
        out_shape=[
            jax.ShapeDtypeStruct((T, D_MODEL), _F32),
            jax.ShapeDtypeStruct((T, LANES), _F32),
        ],
        compiler_params=_params(("parallel",)),
        name="outproj",
    )(mixed, h0, wo, ln_g, ln_b, wr)


def _thresh_kernel(aff_ref, thr_ref, rem_ref, *, cap):
    rows = aff_ref.shape[0]
    bits = lax.bitcast_convert_type(aff_ref[...], jnp.int32)

    def count(pred):
        c = jnp.sum(pred.astype(jnp.int32).reshape(rows // SUBLANES, SUBLANES, LANES), axis=0)
        c = jnp.broadcast_to(jnp.sum(c, axis=0, keepdims=True), (SUBLANES, LANES))
        for shift in (64, 32, 16):
            c = c + pltpu.roll(c, shift, 1)
        return c

    def body(i, ans):
        cand = ans | jnp.left_shift(jnp.int32(1), 30 - i)
        c = count(bits >= cand[0:1, :])
        return jnp.where(c >= cap, cand, ans)

    ans = lax.fori_loop(0, 31, body, jnp.zeros((SUBLANES, LANES), jnp.int32))
    thr_ref[...] = ans
    rem_ref[...] = cap - count(bits > ans[0:1, :])


def _thresh(affc, cap):
    shp = jax.ShapeDtypeStruct((SUBLANES, LANES), jnp.int32)
    return pl.pallas_call(
        functools.partial(_thresh_kernel, cap=cap),
        out_shape=[shp, shp],
        compiler_params=pltpu.CompilerParams(vmem_limit_bytes=VMEM_LIMIT),
        name="thresh",
    )(affc)


def _select_kernel(aff_ref, thr_ref, rem_ref, spread_ref, gsel_ref, lidx_ref, cnt_ref, off_ref,
                   nsel_ref, neq_ref):
    @pl.when(pl.program_id(0) == 0)
    def _():
        nsel_ref[...] = jnp.zeros(nsel_ref.shape, _F32)
        neq_ref[...] = jnp.zeros(neq_ref.shape, _F32)

    aff = aff_ref[...]
    bits = lax.bitcast_convert_type(aff, jnp.int32)
    thr = thr_ref[0:1, :]
    rem = rem_ref[0:1, :].astype(_F32)
    row = lax.broadcasted_iota(jnp.int32, (TOK_BLOCK, LANES), 0)
    lane = lax.broadcasted_iota(jnp.int32, (TOK_BLOCK, LANES), 1)
    valid = lane < N_EXPERTS
    before = (lane < row).astype(_MXU)
    gt = (bits > thr) & valid
    eq = (bits == thr) & valid
    eq_before = _dot(before, eq.astype(_MXU)) + neq_ref[0:1, :]
    sel = gt | (eq & (eq_before < rem))
    pos = _dot(before, sel.astype(_MXU))
    cnt = jnp.sum(sel.astype(_F32), axis=0, keepdims=True)
    off_ref[0] = nsel_ref[0:1, :].astype(jnp.int32)
    cnt_ref[0] = cnt.astype(jnp.int32)
    nsel_ref[0:1, :] = nsel_ref[0:1, :] + cnt
    neq_ref[0:1, :] = neq_ref[0:1, :] + jnp.sum(eq.astype(_F32), axis=0, keepdims=True)
    gsel_ref[...] = jnp.where(sel, aff, 0.0)

    ranked = jnp.where(sel, pos, -1.0).astype(_MXU)
    spread = _dot(ranked, spread_ref[...])
    slot = (lax.broadcasted_iota(jnp.int32, spread.shape, 1) % TOK_BLOCK).astype(_F32)
    tok = lax.broadcasted_iota(jnp.int32, spread.shape, 0)
    lidx_ref[0] = jnp.sum(jnp.where(spread == slot, tok, 0), axis=0, keepdims=True)


def _select(aff, thr, rem, spread):
    T = aff.shape[0]
    nb = T // TOK_BLOCK
    const = lambda b: (0, 0)
    return pl.pallas_call(
        _select_kernel,
        grid=(nb,),
        in_specs=[
            pl.BlockSpec((TOK_BLOCK, LANES), lambda b: (b, 0)),
            pl.BlockSpec((SUBLANES, LANES), const),
            pl.BlockSpec((SUBLANES, LANES), const),
            pl.BlockSpec((LANES, N_EXPERTS * TOK_BLOCK), const),
        ],
        out_specs=[
            pl.BlockSpec((TOK_BLOCK, LANES), lambda b: (b, 0)),
            pl.BlockSpec((1, 1, N_EXPERTS * TOK_BLOCK), lambda b: (b, 0, 0)),
            pl.BlockSpec((1, 1, LANES), lambda b: (b, 0, 0)),
            pl.BlockSpec((1, 1, LANES), lambda b: (b, 0, 0)),
        ],
        out_shape=[
            jax.ShapeDtypeStruct((T, LANES), _F32),
            jax.ShapeDtypeStruct((nb, 1, N_EXPERTS * TOK_BLOCK), jnp.int32),
            jax.ShapeDtypeStruct((nb, 1, LANES), jnp.int32),
            jax.ShapeDtypeStruct((nb, 1, LANES), jnp.int32),
        ],
        scratch_shapes=[pltpu.VMEM((SUBLANES, LANES), _F32), pltpu.VMEM((SUBLANES, LANES), _F32)],
        compiler_params=_params(("arbitrary",)),
        name="select",
    )(aff, thr, rem, spread)


WAIT_ROWS = 128


def _dispatch_kernel(cnt_ref, off_ref, lidx_hbm, h1_hbm, xe_hbm, lidx_smem, sem_idx, sem_rows, *, cap, nb):
    e = pl.program_id(0)
    cp = pltpu.make_async_copy(lidx_hbm.at[e], lidx_smem, sem_idx)
    cp.start()
    cp.wait()

    def block(b, carry):
        c = cnt_ref[e * nb + b]
        base = e * cap + off_ref[e * nb + b]

        def one(r, carry2):
            t = b * TOK_BLOCK + lidx_smem[b, r]
            pltpu.make_async_copy(h1_hbm.at[pl.ds(t, 1)], xe_hbm.at[pl.ds(base + r, 1)], sem_rows).start()
            return carry2

        return lax.fori_loop(0, c, one, carry)

    lax.fori_loop(0, nb, block, 0)

    def drain(i, carry):
        pltpu.make_async_copy(h1_hbm.at[pl.ds(0, WAIT_ROWS)], xe_hbm.at[pl.ds(e * cap, WAIT_ROWS)],
                              sem_rows).wait()
        return carry

    lax.fori_loop(0, cap // WAIT_ROWS, drain, 0)


def _dispatch(cnt_e, off_e, lidx_e, h1, cap):
    nb = lidx_e.shape[1]
    return pl.pallas_call(
        functools.partial(_dispatch_kernel, cap=cap, nb=nb),
        grid_spec=pltpu.PrefetchScalarGridSpec(
            num_scalar_prefetch=2,
            grid=(N_EXPERTS,),
            in_specs=[pl.BlockSpec(memory_space=pl.ANY), pl.BlockSpec(memory_space=pl.ANY)],
            out_specs=pl.BlockSpec(memory_space=pl.ANY),
            scratch_shapes=[
                pltpu.SMEM((nb, TOK_BLOCK), jnp.int32),
                pltpu.SemaphoreType.DMA,
                pltpu.SemaphoreType.DMA,
            ],
        ),
        out_shape=jax.ShapeDtypeStruct((N_EXPERTS * cap, D_MODEL), _F32),
        compiler_params=_params(("arbitrary",)),
        name="dispatch",
    )(cnt_e, off_e, lidx_e, h1)


def _ffn_kernel(x_ref, wg_ref, wu_ref, wd_ref, y_ref):
    x = x_ref[...].astype(_MXU)
    g = _dot(x, wg_ref[0])
    u = _dot(x, wu_ref[0])
    hid = (g * _sigmoid(g) * u).astype(_MXU)
    y_ref[...] = _dot(hid, wd_ref[0])


def _ffn(xe, wg, wu, wd, cap, tm):
    per = cap // tm
    return pl.pallas_call(
        _ffn_kernel,
        grid=(N_EXPERTS, per),
        in_specs=[
            pl.BlockSpec((tm, D_MODEL), lambda e, j: (e * per + j, 0)),
            pl.BlockSpec((1, D_MODEL, D_FF), lambda e, j: (e, 0, 0)),
            pl.BlockSpec((1, D_MODEL, D_FF), lambda e, j: (e, 0, 0)),
            pl.BlockSpec((1, D_FF, D_MODEL), lambda e, j: (e, 0, 0)),
        ],
        out_specs=pl.BlockSpec((tm, D_MODEL), lambda e, j: (e * per + j, 0)),
        out_shape=jax.ShapeDtypeStruct((N_EXPERTS * cap, D_MODEL), _F32),
        compiler_params=_params(("parallel", "arbitrary")),
        name="ffn",
    )(xe, wg, wu, wd)


def _combine_kernel(cnt_ref, off_ref, lidx_hbm, ye_hbm, gsel_ref, h1_ref, lg_ref, lb_ref, y_ref,
                    slots_ref, lidx_smem, sem_idx, sem_rows, *, cap):
    b = pl.program_id(0)

    @pl.when(b == 0)
    def _():
        slots_ref[...] = jnp.zeros(slots_ref.shape, _F32)

    cp = pltpu.make_async_copy(lidx_hbm.at[b], lidx_smem, sem_idx)
    cp.start()
    cp.wait()

    def row_copy(e, src_row, dst_row):
        return pltpu.make_async_copy(ye_hbm.at[pl.ds(src_row, 1)], slots_ref.at[e, pl.ds(dst_row, 1)], sem_rows)

    for e in range(N_EXPERTS):
        c = cnt_ref[b * N_EXPERTS + e]
        base = e * cap + off_ref[b * N_EXPERTS + e]

        def one(r, carry):
            row_copy(e, base + r, lidx_smem[0, e * TOK_BLOCK + r]).start()
            return carry

        lax.fori_loop(0, c, one, 0)
    for e in range(N_EXPERTS):
        c = cnt_ref[b * N_EXPERTS + e]

        def done(r, carry):
            row_copy(e, 0, 0).wait()
            return carry

        lax.fori_loop(0, c, done, 0)

    acc = ALPHA * h1_ref[...]
    gsel = gsel_ref[...]
    for e in range(N_EXPERTS):
        acc = acc + gsel[:, e:e + 1] * slots_ref[e]
    y_ref[...] = _layer_norm(acc, lg_ref[...], lb_ref[...])


def _combine(cnt_t, off_t, lidx, ye, gsel, h1, ln_g, ln_b, cap):
    T = h1.shape[0]
    nb = T // TOK_BLOCK
    return pl.pallas_call(
        functools.partial(_combine_kernel, cap=cap),
        grid_spec=pltpu.PrefetchScalarGridSpec(
            num_scalar_prefetch=2,
            grid=(nb,),
            in_specs=[
                pl.BlockSpec(memory_space=pl.ANY),
                pl.BlockSpec(memory_space=pl.ANY),
                pl.BlockSpec((TOK_BLOCK, LANES), lambda b, *_: (b, 0)),
                pl.BlockSpec((TOK_BLOCK, D_MODEL), lambda b, *_: (b, 0)),
                pl.BlockSpec((1, D_MODEL), lambda b, *_: (0, 0)),
                pl.BlockSpec((1, D_MODEL), lambda b, *_: (0, 0)),
            ],
            out_specs=pl.BlockSpec((TOK_BLOCK, D_MODEL), lambda b, *_: (b, 0)),
            scratch_shapes=[
                pltpu.VMEM((N_EXPERTS, TOK_BLOCK, D_MODEL), _F32),
                pltpu.SMEM((1, N_EXPERTS * TOK_BLOCK), jnp.int32),
                pltpu.SemaphoreType.DMA,
                pltpu.SemaphoreType.DMA,
            ],
        ),
        out_shape=jax.ShapeDtypeStruct((T, D_MODEL), _F32),
        compiler_params=_params(("arbitrary",)),
        name="combine",
    )(cnt_t, off_t, lidx, ye, gsel, h1, ln_g, ln_b)


def _tables(seq):
    half = HEAD_DIM // 2
    inv = 1.0 / (ROPE_BASE ** (jnp.arange(half, dtype=_F32) / half))
    ang = jnp.arange(seq, dtype=_F32)[:, None] * inv[None, :]
    cos = jnp.concatenate([jnp.cos(ang), jnp.cos(ang)], axis=1)
    sin = jnp.concatenate([-jnp.sin(ang), jnp.sin(ang)], axis=1)
    log_g = jnp.log1p(-jnp.exp2(-5.0 - jnp.arange(N_HEADS, dtype=_F32)))[:, None, None]
    pos = jnp.arange(CHUNK, dtype=_F32)
    rows = lambda f: jnp.broadcast_to(jnp.exp(log_g * f[None, :, None]), (N_HEADS, CHUNK, HEAD_DIM))
    dsym = jnp.exp(log_g * jnp.abs(pos[:, None] - pos[None, :])[None])
    tabs = dict(
        cos=cos, sin=sin, dsym=dsym,
        qwf=rows(pos + 1.0), kwf=rows(CHUNK - 1.0 - pos),
        qwb=rows(CHUNK - pos), kwb=rows(pos),
        gl=rows(jnp.full((CHUNK,), float(CHUNK), _F32)),
    )
    e_of_col = jnp.arange(N_EXPERTS * TOK_BLOCK) // TOK_BLOCK
    tabs["spread"] = (jnp.arange(LANES)[:, None] == e_of_col[None, :]).astype(_MXU)
    return tabs


def _trunk(x, w):
    batch, seq, _ = x.shape
    T = batch * seq
    nb = T // TOK_BLOCK
    cap = CAP_FACTOR * T // N_EXPERTS
    t = _tables(seq)
    h0, P, G2, GC, GR = _inproj(x.reshape(T, D_MODEL), seq, w["ln_in_g"], w["ln_in_b"], w["w_main"],
                                w["b_main"], w["wg"], w["bg"], w["wgt"], w["bgt"], t["cos"], t["sin"])
    YB = _sweep_bwd(P, GC, GR, t["qwb"], t["kwb"], t["gl"], batch, seq)
    mixed = _sweep_fwd(P, G2, GC, GR, YB, w["ret_g"], w["mlstm_g"], t["dsym"], t["qwf"], t["kwf"], t["gl"],
                       batch, seq)
    h1, aff = _outproj(mixed, h0, w["w_o"], w["ln1_g"], w["ln1_b"], w["w_r"])
    affc = aff[:, :N_EXPERTS].reshape(T // SUBLANES, LANES)
    thr, rem = _thresh(affc, cap)
    gsel, lidx, cnt, off = _select(aff, thr, rem, t["spread"])
    cnt2 = cnt.reshape(nb, LANES)[:, :N_EXPERTS]
    off2 = off.reshape(nb, LANES)[:, :N_EXPERTS]
    lidx_e = lidx.reshape(nb, N_EXPERTS, TOK_BLOCK).transpose(1, 0, 2)
    xe = _dispatch(cnt2.T.reshape(-1), off2.T.reshape(-1), lidx_e, h1, cap)
    ye = _ffn(xe, w["w_gate"], w["w_up"], w["w_down"], cap, min(256, cap))
    y = _combine(cnt2.reshape(-1), off2.reshape(-1), lidx, ye, gsel, h1, w["ln2_g"], w["ln2_b"], cap)
    return y.reshape(batch, seq, D_MODEL)


def _prep_weights(ln_in_g, ln_in_b, w_in, b_in, ret_norm_g, mlstm_norm_g, w_o, ln1_g, ln1_b, w_router,
                  w_gate, w_up, w_down, ln2_g, ln2_b):
    main = 8 * SEC
    ngate = 4 * N_HEADS
    row = lambda v: v.reshape(1, -1).astype(_F32)
    wg = jnp.pad(w_in[0][:, main:main + ngate], ((0, 0), (0, LANES - ngate)))
    bg = jnp.pad(b_in[0][main:main + ngate], (0, LANES - ngate))
    return dict(
        ln_in_g=row(ln_in_g), ln_in_b=row(ln_in_b),
        w_main=w_in[0][:, :main].astype(_MXU), b_main=row(b_in[0][:main]),
        wg=wg.astype(_MXU), bg=row(bg), wgt=wg.T.astype(_MXU), bgt=bg.reshape(-1, 1).astype(_F32),
        ret_g=row(ret_norm_g[0]), mlstm_g=row(mlstm_norm_g[0]),
        w_o=w_o[0].astype(_MXU), ln1_g=row(ln1_g[0]), ln1_b=row(ln1_b[0]),
        w_r=jnp.pad(w_router[0], ((0, 0), (0, LANES - N_EXPERTS))).astype(_MXU),
        w_gate=w_gate[0].astype(_MXU), w_up=w_up[0].astype(_MXU), w_down=w_down[0].astype(_MXU),
        ln2_g=row(ln2_g[0]), ln2_b=row(ln2_b[0]),
    )


def kernel(x_prompt, x_sample, ln_in_g, ln_in_b, w_in, b_in, ret_norm_g, mlstm_norm_g, w_o, ln1_g, ln1_b,
           w_router, w_gate, w_up, w_down, ln2_g, ln2_b):
    w = _prep_weights(ln_in_g, ln_in_b, w_in, b_in, ret_norm_g, mlstm_norm_g, w_o, ln1_g, ln1_b, w_router,
                      w_gate, w_up, w_down, ln2_g, ln2_b)
    return (_trunk(x_prompt, w), _trunk(x_sample, w))
```

```python
import functools

import jax
import jax.numpy as jnp
from jax import lax
from jax.experimental import pallas as pl
from jax.experimental.pallas import tpu as pltpu

D_MODEL = 1024
N_HEADS = 4
HEAD_DIM = 128
SEC = N_HEADS * HEAD_DIM
CHUNK = 128
N_EXPERTS = 16
D_FF = 2 * D_MODEL
CAP_FACTOR = 2
ROPE_BASE = 10000.0
LN_EPS = 1e-5
NEG_BIG = -1e30
DEPTH = 1
ALPHA = (2.0 * DEPTH) ** 0.25
K_SCALE = HEAD_DIM ** -0.5
LANES = 128
SUBLANES = 8
TOK_BLOCK = 128
VMEM_LIMIT = 56 * 1024 * 1024

_MXU = jnp.bfloat16
_F32 = jnp.float32


def _dot(a, b):
    return jnp.dot(a, b, preferred_element_type=_F32)


def _dot_nt(a, b):
    return lax.dot_general(a, b, (((1,), (1,)), ((), ())), preferred_element_type=_F32)


def _split3(x):
    x1 = x.astype(_MXU)
    r1 = x - x1.astype(_F32)
    x2 = r1.astype(_MXU)
    r2 = r1 - x2.astype(_F32)
    return x1, x2, r2.astype(_MXU)


def _dot01_left(a01, x):
    x1, x2, x3 = _split3(x)
    return _dot(a01, x1) + _dot(a01, x2) + _dot(a01, x3)


def _dot01_right(x, a01):
    x1, x2, x3 = _split3(x)
    return _dot(x1, a01) + _dot(x2, a01) + _dot(x3, a01)


def _layer_norm(x, g, b):
    mu = jnp.mean(x, axis=-1, keepdims=True)
    xc = x - mu
    var = jnp.mean(xc * xc, axis=-1, keepdims=True)
    return xc * lax.rsqrt(var + LN_EPS) * g + b


def _log_sigmoid(x):
    return jnp.minimum(x, 0.0) - jnp.log1p(jnp.exp(-jnp.abs(x)))


def _sigmoid(x):
    return 1.0 / (1.0 + jnp.exp(-x))


def _params(sem):
    return pltpu.CompilerParams(dimension_semantics=sem, vmem_limit_bytes=VMEM_LIMIT)


ROW_TILES = D_MODEL // LANES


def _store_row_tiles(ref, x):
    for j in range(ROW_TILES):
        ref[:, j, :] = x[:, j * LANES:(j + 1) * LANES]


def _load_row_tiles(ref):
    return jnp.concatenate([ref[:, j, :] for j in range(ROW_TILES)], axis=1)


_P_COL = {0: 0, 1: 1, 2: 2, 4: 3, 5: 4, 6: 5}
_G2_COL = {3: 0, 7: 1}


def _inproj_kernel(x_ref, lg_ref, lb_ref, w_ref, b_ref, wg_ref, bg_ref, wgt_ref, bgt_ref,
                   cos_ref, sin_ref, h0_ref, p_ref, g2_ref, gc_ref, gr_ref):
    tm = x_ref.shape[0]
    h = _layer_norm(x_ref[...], lg_ref[...], lb_ref[...])
    h0_ref[...] = h
    hb = h.astype(_MXU)
    cos = cos_ref[...]
    sin = sin_ref[...]
    for sec in range(8):
        acc = _dot(hb, w_ref[:, sec * SEC:(sec + 1) * SEC]) + b_ref[:, sec * SEC:(sec + 1) * SEC]
        if sec in (0, 1):
            c0 = _P_COL[sec] * SEC
            for hh in range(N_HEADS):
                s = acc[:, hh * HEAD_DIM:(hh + 1) * HEAD_DIM]
                r = s * cos + pltpu.roll(s, HEAD_DIM // 2, 1) * sin
                if sec == 1:
                    r = r * K_SCALE
                p_ref[:, c0 + hh * HEAD_DIM:c0 + (hh + 1) * HEAD_DIM] = r.astype(p_ref.dtype)
        elif sec in _P_COL:
            if sec == 5:
                acc = acc * K_SCALE
            c0 = _P_COL[sec] * SEC
            p_ref[:, c0:c0 + SEC] = acc.astype(p_ref.dtype)
        else:
            c0 = _G2_COL[sec] * SEC
            g2_ref[:, c0:c0 + SEC] = acc

    pre = _dot(hb, wg_ref[...]) + bg_ref[...]
    pre_t = _dot_nt(wgt_ref[...], hb) + bgt_ref[...]
    row = lax.broadcasted_iota(jnp.int32, (CHUNK, CHUNK), 0)
    col = lax.broadcasted_iota(jnp.int32, (CHUNK, CHUNK), 1)
    tri_le = (col <= row).astype(_MXU)
    tri_ge = (col >= row).astype(_MXU)
    for c in range(tm // CHUNK):
        sl = slice(c * CHUNK, (c + 1) * CHUNK)
        blk = pre[sl, :]
        ls = _log_sigmoid(blk)
        pref = _dot01_left(tri_le, ls)
        suf = _dot01_left(tri_ge, ls)
        gc_ref[sl, :] = jnp.where((col >= 4) & (col < 8), pref,
                                  jnp.where((col >= 12) & (col < 16), suf, blk))
        blk_t = pre_t[:, sl]
        ls_t = _log_sigmoid(blk_t)
        pref_t = _dot01_right(ls_t, tri_ge)
        suf_t = _dot01_right(ls_t, tri_le)
        gr_ref[:, sl] = jnp.where((row >= 4) & (row < 8), pref_t,
                                  jnp.where((row >= 12) & (row < 16), suf_t, blk_t))


def _inproj(x2, seq, ln_g, ln_b, w_main, b_main, wg, bg, wgt, bgt, cos, sin, tm=512):
    T = x2.shape[0]
    nseq = seq // tm
    const = lambda i: (0, 0)
    return pl.pallas_call(
        _inproj_kernel,
        grid=(T // tm,),
        in_specs=[
            pl.BlockSpec((tm, D_MODEL), lambda i: (i, 0)),
            pl.BlockSpec((1, D_MODEL), const),
            pl.BlockSpec((1, D_MODEL), const),
            pl.BlockSpec((D_MODEL, 8 * SEC), const),
            pl.BlockSpec((1, 8 * SEC), const),
            pl.BlockSpec((D_MODEL, LANES), const),
            pl.BlockSpec((1, LANES), const),
            pl.BlockSpec((LANES, D_MODEL), const),
            pl.BlockSpec((LANES, 1), const),
            pl.BlockSpec((tm, HEAD_DIM), lambda i: (i % nseq, 0)),
            pl.BlockSpec((tm, HEAD_DIM), lambda i: (i % nseq, 0)),
        ],
        out_specs=[
            pl.BlockSpec((tm, D_MODEL), lambda i: (i, 0)),
            pl.BlockSpec((tm, 6 * SEC), lambda i: (i, 0)),
            pl.BlockSpec((tm, 2 * SEC), lambda i: (i, 0)),
            pl.BlockSpec((tm, LANES), lambda i: (i, 0)),
            pl.BlockSpec((LANES, tm), lambda i: (0, i)),
        ],
        out_shape=[
            jax.ShapeDtypeStruct((T, D_MODEL), _F32),
            jax.ShapeDtypeStruct((T, 6 * SEC), _MXU),
            jax.ShapeDtypeStruct((T, 2 * SEC), _F32),
            jax.ShapeDtypeStruct((T, LANES), _F32),
            jax.ShapeDtypeStruct((LANES, T), _F32),
        ],
        compiler_params=_params(("parallel",)),
        name="inproj",
    )(x2, ln_g, ln_b, w_main, b_main, wg, bg, wgt, bgt, cos, sin)


def _init_state(s_ref, c_ref, n_ref, m_ref):
    s_ref[...] = jnp.zeros(s_ref.shape, _F32)
    c_ref[...] = jnp.zeros(c_ref.shape, _F32)
    n_ref[...] = jnp.zeros(n_ref.shape, _F32)
    m_ref[...] = jnp.full(m_ref.shape, NEG_BIG, _F32)


def _mlstm_direction(q, k, v, gc, gr, ch_i, ch_f, last_lane, mask, c_ref, n_ref, m_ref, h):
    cum_col = gc[:, ch_f:ch_f + 1]
    li_col = gc[:, ch_i:ch_i + 1]
    cum_row = gr[ch_f:ch_f + 1, :]
    li_row = gr[ch_i:ch_i + 1, :]
    cum_last = cum_row[:, last_lane:last_lane + 1]
    m_prev = m_ref[h:h + 1, 0:1]
    c_prev = c_ref[h]
    n_prev = n_ref[h:h + 1, :]

    log_d = jnp.where(mask, cum_col - cum_row + li_row, NEG_BIG)
    inter = cum_col + m_prev
    m_row = jnp.maximum(jnp.max(log_d, axis=1, keepdims=True), inter)
    d_w = jnp.exp(log_d - m_row)
    s_inter = jnp.exp(inter - m_row)
    qk = _dot_nt(q, k) * d_w
    num = _dot(qk.astype(_MXU), v) + s_inter * _dot(q, c_prev.astype(_MXU))
    den = (jnp.sum(qk, axis=1, keepdims=True)
           + s_inter * jnp.sum(q.astype(_F32) * n_prev, axis=1, keepdims=True))
    h_out = num / jnp.maximum(jnp.abs(den), jnp.exp(-m_row))

    a_row = cum_last - cum_row + li_row
    a_max = jnp.max(a_row, axis=1, keepdims=True)
    w_col = jnp.exp(cum_last - cum_col + li_col - a_max)
    kw = k.astype(_F32) * w_col
    u_c = _dot(kw.T.astype(_MXU), v)
    u_n = jnp.sum(kw, axis=0, keepdims=True)
    m_new = jnp.maximum(cum_last + m_prev, a_max)
    s_old = jnp.exp(cum_last + m_prev - m_new)
    s_new = jnp.exp(a_max - m_new)
    c_ref[h] = s_old * c_prev + s_new * u_c
    n_ref[h:h + 1, :] = s_old * n_prev + s_new * u_n
    m_ref[h:h + 1, :] = jnp.broadcast_to(m_new, (1, LANES))
    return h_out


def _retention_state_update(k, v, kw_tab, gl_tab, s_ref, h):
    kw = k.astype(_F32) * kw_tab[h]
    s_ref[h] = gl_tab[h] * s_ref[h] + _dot(kw.T.astype(_MXU), v)


def _sweep_bwd_kernel(rq_ref, rk_ref, rv_ref, mq_ref, mk_ref, mv_ref, gc_ref, gr_ref,
                      qwb_ref, kwb_ref, gl_ref, yb_ref, s_ref, c_ref, n_ref, m_ref):
    @pl.when(pl.program_id(1) == 0)
    def _():
        _init_state(s_ref, c_ref, n_ref, m_ref)

    gc = gc_ref[...]
    gr = gr_ref[...]
    row = lax.broadcasted_iota(jnp.int32, (CHUNK, CHUNK), 0)
    col = lax.broadcasted_iota(jnp.int32, (CHUNK, CHUNK), 1)
    mask = col >= row
    for h in range(N_HEADS):
        sl = slice(h * HEAD_DIM, (h + 1) * HEAD_DIM)
        q, k, v = rq_ref[:, sl], rk_ref[:, sl], rv_ref[:, sl]
        qs = (q.astype(_F32) * qwb_ref[h]).astype(_MXU)
        yb_ref[:, sl] = _dot(qs, s_ref[h].astype(_MXU))
        _retention_state_update(k, v, kwb_ref, gl_ref, s_ref, h)
    for h in range(N_HEADS):
        sl = slice(h * HEAD_DIM, (h + 1) * HEAD_DIM)
        h_b = _mlstm_direction(mq_ref[:, sl], mk_ref[:, sl], mv_ref[:, sl], gc, gr,
                               8 + h, 12 + h, 0, mask, c_ref, n_ref, m_ref, h)
        yb_ref[:, SEC + h * HEAD_DIM:SEC + (h + 1) * HEAD_DIM] = h_b


def _sweep_fwd_kernel(rq_ref, rk_ref, rv_ref, mq_ref, mk_ref, mv_ref, gc_ref, gr_ref,
                      yb_ref, g2_ref, rng_ref, mng_ref, dsym_ref, qwf_ref, kwf_ref, gl_ref,
                      mixed_ref, s_ref, c_ref, n_ref, m_ref):
    @pl.when(pl.program_id(1) == 0)
    def _():
        _init_state(s_ref, c_ref, n_ref, m_ref)

    gc = gc_ref[...]
    gr = gr_ref[...]
    row = lax.broadcasted_iota(jnp.int32, (CHUNK, CHUNK), 0)
    col = lax.broadcasted_iota(jnp.int32, (CHUNK, CHUNK), 1)
    mask = col <= row

    def head_norm(y):
        mu = jnp.mean(y, axis=1, keepdims=True)
        yc = y - mu
        var = jnp.mean(yc * yc, axis=1, keepdims=True)
        return yc * lax.rsqrt(var + LN_EPS)

    for h in range(N_HEADS):
        sl = slice(h * HEAD_DIM, (h + 1) * HEAD_DIM)
        q, k, v = rq_ref[:, sl], rk_ref[:, sl], rv_ref[:, sl]
        p = (_dot_nt(q, k) * dsym_ref[h]).astype(_MXU)
        qs = (q.astype(_F32) * qwf_ref[h]).astype(_MXU)
        y = _dot(p, v) + _dot(qs, s_ref[h].astype(_MXU)) + yb_ref[:, sl]
        _retention_state_update(k, v, kwf_ref, gl_ref, s_ref, h)
        g = g2_ref[:, sl]
        out = head_norm(y) * rng_ref[:, sl] * (g * _sigmoid(g))
        mixed_ref[:, sl] = out.astype(mixed_ref.dtype)
    for h in range(N_HEADS):
        sl = slice(h * HEAD_DIM, (h + 1) * HEAD_DIM)
        sl2 = slice(SEC + h * HEAD_DIM, SEC + (h + 1) * HEAD_DIM)
        h_f = _mlstm_direction(mq_ref[:, sl], mk_ref[:, sl], mv_ref[:, sl], gc, gr,
                               h, 4 + h, CHUNK - 1, mask, c_ref, n_ref, m_ref, h)
        out = head_norm(h_f + yb_ref[:, sl2]) * mng_ref[:, sl] * _sigmoid(g2_ref[:, sl2])
        mixed_ref[:, sl2] = out.astype(mixed_ref.dtype)


def _state_scratch():
    return [
        pltpu.VMEM((N_HEADS, HEAD_DIM, HEAD_DIM), _F32),
        pltpu.VMEM((N_HEADS, HEAD_DIM, HEAD_DIM), _F32),
        pltpu.VMEM((SUBLANES, HEAD_DIM), _F32),
        pltpu.VMEM((SUBLANES, LANES), _F32),
    ]


def _sweep_specs(nchunk, reverse):
    def rb(b, n):
        return b * nchunk + ((nchunk - 1 - n) if reverse else n)
    p_specs = [pl.BlockSpec((CHUNK, SEC), functools.partial(lambda b, n, s: (rb(b, n), s), s=s))
               for s in range(6)]
    gc_spec = pl.BlockSpec((CHUNK, LANES), lambda b, n: (rb(b, n), 0))
    gr_spec = pl.BlockSpec((LANES, CHUNK), lambda b, n: (0, rb(b, n)))
    wide = pl.BlockSpec((CHUNK, 2 * SEC), lambda b, n: (rb(b, n), 0))
    tab = pl.BlockSpec((N_HEADS, CHUNK, HEAD_DIM), lambda b, n: (0, 0, 0))
    return p_specs, gc_spec, gr_spec, wide, tab


def _sweep_bwd(P, GC, GR, qwb, kwb, gl, batch, seq):
    T = P.shape[0]
    nchunk = seq // CHUNK
    p_specs, gc_spec, gr_spec, wide, tab = _sweep_specs(nchunk, True)
    return pl.pallas_call(
        _sweep_bwd_kernel,
        grid=(batch, nchunk),
        in_specs=p_specs + [gc_spec, gr_spec, tab, tab, tab],
        out_specs=wide,
        out_shape=jax.ShapeDtypeStruct((T, 2 * SEC), _F32),
        scratch_shapes=_state_scratch(),
        compiler_params=_params(("parallel", "arbitrary")),
        name="sweep_bwd",
    )(P, P, P, P, P, P, GC, GR, qwb, kwb, gl)


def _sweep_fwd(P, G2, GC, GR, YB, rng, mng, dsym, qwf, kwf, gl, batch, seq):
    T = P.shape[0]
    nchunk = seq // CHUNK
    p_specs, gc_spec, gr_spec, wide, tab = _sweep_specs(nchunk, False)
    gain = pl.BlockSpec((1, SEC), lambda b, n: (0, 0))
    return pl.pallas_call(
        _sweep_fwd_kernel,
        grid=(batch, nchunk),
        in_specs=p_specs + [gc_spec, gr_spec, wide, wide, gain, gain, tab, tab, tab, tab],
        out_specs=wide,
        out_shape=jax.ShapeDtypeStruct((T, 2 * SEC), _MXU),
        scratch_shapes=_state_scratch(),
        compiler_params=_params(("parallel", "arbitrary")),
        name="sweep_fwd",
    )(P, P, P, P, P, P, GC, GR, YB, G2, rng, mng, dsym, qwf, kwf, gl)


def _outproj_kernel(mixed_ref, h0_ref, wo_ref, lg_ref, lb_ref, wr_ref, h1_ref, aff_ref):
    z = ALPHA * h0_ref[...] + _dot(mixed_ref[...], wo_ref[...])
    h1 = _layer_norm(z, lg_ref[...], lb_ref[...])
    _store_row_tiles(h1_ref, h1)
    logits = _dot(h1.astype(_MXU), wr_ref[...])
    lane = lax.broadcasted_iota(jnp.int32, logits.shape, 1)
    valid = lane < N_EXPERTS
    logits = jnp.where(valid, logits, NEG_BIG)
    e = jnp.exp(logits - jnp.max(logits, axis=1, keepdims=True))
    aff = e / jnp.sum(e, axis=1, keepdims=True)
    aff_ref[...] = jnp.where(valid, aff, 0.0)


def _outproj(mixed, h0, wo, ln_g, ln_b, wr, tm=512):
    T = mixed.shape[0]
    const = lambda i: (0, 0)
    return pl.pallas_call(
        _outproj_kernel,
        grid=(T // tm,),
        in_specs=[
            pl.BlockSpec((tm, D_MODEL), lambda i: (i, 0)),
            pl.BlockSpec((tm, D_MODEL), lambda i: (i, 0)),
            pl.BlockSpec((D_MODEL, D_MODEL), const),
            pl.BlockSpec((1, D_MODEL), const),
            pl.BlockSpec((1, D_MODEL), const),
            pl.BlockSpec((D_MODEL, LANES), const),
        ],
        out_specs=[
            pl.BlockSpec((tm, ROW_TILES, LANES), lambda i: (i, 0, 0)),
            pl.BlockSpec((tm, LANES), lambda i: (i, 0)),
        ],
        out_shape=[
            jax.ShapeDtypeStruct((T, ROW_TILES, LANES), _F32),
            jax.ShapeDtypeStruct((T, LANES), _F32),
        ],
        compiler_params=_params(("parallel",)),
        name="outproj",
    )(mixed, h0, wo, ln_g, ln_b, wr)


def _thresh_kernel(aff_ref, thr_ref, rem_ref, *, cap):
    rows = aff_ref.shape[0]
    bits = lax.bitcast_convert_type(aff_ref[...], jnp.int32)

    def count(pred):
        c = jnp.sum(pred.astype(jnp.int32).reshape(rows // SUBLANES, SUBLANES, LANES), axis=0)
        c = jnp.broadcast_to(jnp.sum(c, axis=0, keepdims=True), (SUBLANES, LANES))
        for shift in (64, 32, 16):
            c = c + pltpu.roll(c, shift, 1)
        return c

    def body(i, ans):
        cand = ans | jnp.left_shift(jnp.int32(1), 30 - i)
        c = count(bits >= cand[0:1, :])
        return jnp.where(c >= cap, cand, ans)

    ans = lax.fori_loop(0, 31, body, jnp.zeros((SUBLANES, LANES), jnp.int32))
    thr_ref[...] = ans
    rem_ref[...] = cap - count(bits > ans[0:1, :])


def _thresh(affc, cap):
    shp = jax.ShapeDtypeStruct((SUBLANES, LANES), jnp.int32)
    return pl.pallas_call(
        functools.partial(_thresh_kernel, cap=cap),
        out_shape=[shp, shp],
        compiler_params=pltpu.CompilerParams(vmem_limit_bytes=VMEM_LIMIT),
        name="thresh",
    )(affc)


def _select_kernel(aff_ref, thr_ref, rem_ref, spread_ref, gsel_ref, lidx_ref, cnt_ref, off_ref,
                   nsel_ref, neq_ref):
    @pl.when(pl.program_id(0) == 0)
    def _():
        nsel_ref[...] = jnp.zeros(nsel_ref.shape, _F32)
        neq_ref[...] = jnp.zeros(neq_ref.shape, _F32)

    aff = aff_ref[...]
    bits = lax.bitcast_convert_type(aff, jnp.int32)
    thr = thr_ref[0:1, :]
    rem = rem_ref[0:1, :].astype(_F32)
    row = lax.broadcasted_iota(jnp.int32, (TOK_BLOCK, LANES), 0)
    lane = lax.broadcasted_iota(jnp.int32, (TOK_BLOCK, LANES), 1)
    valid = lane < N_EXPERTS
    before = (lane < row).astype(_MXU)
    gt = (bits > thr) & valid
    eq = (bits == thr) & valid
    eq_before = _dot(before, eq.astype(_MXU)) + neq_ref[0:1, :]
    sel = gt | (eq & (eq_before < rem))
    pos = _dot(before, sel.astype(_MXU))
    cnt = jnp.sum(sel.astype(_F32), axis=0, keepdims=True)
    off_ref[0] = nsel_ref[0:1, :].astype(jnp.int32)
    cnt_ref[0] = cnt.astype(jnp.int32)
    nsel_ref[0:1, :] = nsel_ref[0:1, :] + cnt
    neq_ref[0:1, :] = neq_ref[0:1, :] + jnp.sum(eq.astype(_F32), axis=0, keepdims=True)
    gsel_ref[...] = jnp.where(sel, aff, 0.0)

    ranked = jnp.where(sel, pos, -1.0).astype(_MXU)
    spread = _dot(ranked, spread_ref[...])
    slot = (lax.broadcasted_iota(jnp.int32, spread.shape, 1) % TOK_BLOCK).astype(_F32)
    tok = lax.broadcasted_iota(jnp.int32, spread.shape, 0) + pl.program_id(0) * TOK_BLOCK
    lidx_ref[0] = jnp.sum(jnp.where(spread == slot, tok, 0), axis=0, keepdims=True)


def _select(aff, thr, rem, spread):
    T = aff.shape[0]
    nb = T // TOK_BLOCK
    const = lambda b: (0, 0)
    return pl.pallas_call(
        _select_kernel,
        grid=(nb,),
        in_specs=[
            pl.BlockSpec((TOK_BLOCK, LANES), lambda b: (b, 0)),
            pl.BlockSpec((SUBLANES, LANES), const),
            pl.BlockSpec((SUBLANES, LANES), const),
            pl.BlockSpec((LANES, N_EXPERTS * TOK_BLOCK), const),
        ],
        out_specs=[
            pl.BlockSpec((TOK_BLOCK, LANES), lambda b: (b, 0)),
            pl.BlockSpec((1, 1, N_EXPERTS * TOK_BLOCK), lambda b: (b, 0, 0)),
            pl.BlockSpec((1, 1, LANES), lambda b: (b, 0, 0)),
            pl.BlockSpec((1, 1, LANES), lambda b: (b, 0, 0)),
        ],
        out_shape=[
            jax.ShapeDtypeStruct((T, LANES), _F32),
            jax.ShapeDtypeStruct((nb, 1, N_EXPERTS * TOK_BLOCK), jnp.int32),
            jax.ShapeDtypeStruct((nb, 1, LANES), jnp.int32),
            jax.ShapeDtypeStruct((nb, 1, LANES), jnp.int32),
        ],
        scratch_shapes=[pltpu.VMEM((SUBLANES, LANES), _F32), pltpu.VMEM((SUBLANES, LANES), _F32)],
        compiler_params=_params(("arbitrary",)),
        name="select",
    )(aff, thr, rem, spread)


def _ffn_kernel(cnt_ref, lidx_hbm, h1_hbm, wg_ref, wu_ref, wd_ref, y_ref,
                xbuf, lidx_smem, walk_ref, sem_idx, sem_rows, *, tm, per, nb):
    e = pl.program_id(0)
    j = pl.program_id(1)
    step = e * per + j
    slot = step % 2

    def idx_copy(en):
        return pltpu.make_async_copy(lidx_hbm.at[en], lidx_smem.at[en % 2], sem_idx.at[en % 2])

    def issue_tile(en, dst_slot):
        par = en % 2

        def cond(st):
            return st[0] < tm

        def body(st):
            n, b, r = st
            c = cnt_ref[en * nb + b]
            take = jnp.minimum(c - r, tm - n)

            def one(i, carry):
                t = lidx_smem[par, b, r + i]
                pltpu.make_async_copy(h1_hbm.at[t], xbuf.at[dst_slot, n + i], sem_rows.at[dst_slot]).start()
                return carry

            lax.fori_loop(0, take, one, 0)
            done = r + take >= c
            return n + take, jnp.where(done, b + 1, b), jnp.where(done, 0, r + take)

        _, b, r = lax.while_loop(cond, body, (jnp.int32(0), walk_ref[0], walk_ref[1]))
        walk_ref[0] = b
        walk_ref[1] = r

    def restart_walk():
        walk_ref[0] = jnp.int32(0)
        walk_ref[1] = jnp.int32(0)

    @pl.when(step == 0)
    def _():
        idx_copy(0).start()
        idx_copy(0).wait()
        restart_walk()
        issue_tile(0, 0)

    @pl.when((j == 0) & (e + 1 < N_EXPERTS))
    def _():
        idx_copy(e + 1).start()

    @pl.when(j + 1 < per)
    def _():
        issue_tile(e, 1 - slot)

    @pl.when((j + 1 == per) & (e + 1 < N_EXPERTS))
    def _():
        idx_copy(e + 1).wait()
        restart_walk()
        issue_tile(e + 1, 1 - slot)

    pltpu.make_async_copy(h1_hbm.at[pl.ds(0, tm)], xbuf.at[slot], sem_rows.at[slot]).wait()
    x = _load_row_tiles(xbuf.at[slot]).astype(_MXU)
    g = _dot(x, wg_ref[0])
    u = _dot(x, wu_ref[0])
    hid = (g * _sigmoid(g) * u).astype(_MXU)
    _store_row_tiles(y_ref, _dot(hid, wd_ref[0]))


def _ffn(cnt_e, lidx_e, h1, wg, wu, wd, cap, tm):
    per = cap // tm
    nb = lidx_e.shape[1]
    return pl.pallas_call(
        functools.partial(_ffn_kernel, tm=tm, per=per, nb=nb),
        grid_spec=pltpu.PrefetchScalarGridSpec(
            num_scalar_prefetch=1,
            grid=(N_EXPERTS, per),
            in_specs=[
                pl.BlockSpec(memory_space=pl.ANY),
                pl.BlockSpec(memory_space=pl.ANY),
                pl.BlockSpec((1, D_MODEL, D_FF), lambda e, j, *_: (e, 0, 0)),
                pl.BlockSpec((1, D_MODEL, D_FF), lambda e, j, *_: (e, 0, 0)),
                pl.BlockSpec((1, D_FF, D_MODEL), lambda e, j, *_: (e, 0, 0)),
            ],
            out_specs=pl.BlockSpec((tm, ROW_TILES, LANES), lambda e, j, *_: (e * per + j, 0, 0)),
            scratch_shapes=[
                pltpu.VMEM((2, tm, ROW_TILES, LANES), _F32),
                pltpu.SMEM((2, nb, TOK_BLOCK), jnp.int32),
                pltpu.SMEM((2,), jnp.int32),
                pltpu.SemaphoreType.DMA((2,)),
                pltpu.SemaphoreType.DMA((2,)),
            ],
        ),
        out_shape=jax.ShapeDtypeStruct((N_EXPERTS * cap, ROW_TILES, LANES), _F32),
        compiler_params=_params(("arbitrary", "arbitrary")),
        name="ffn",
    )(cnt_e, lidx_e, h1, wg, wu, wd)


def _combine_kernel(cnt_ref, off_ref, lidx_hbm, ye_hbm, gsel_ref, h1_ref, lg_ref, lb_ref, y_ref,
                    slots_ref, lidx_smem, sem_idx, sem_rows, *, cap, nb):
    b = pl.program_id(0)
    slot = b % 2

    def idx_copy(bn):
        return pltpu.make_async_copy(lidx_hbm.at[bn], lidx_smem.at[bn % 2], sem_idx.at[bn % 2])

    def issue_block(bn):
        par = bn % 2
        for e in range(N_EXPERTS):
            c = cnt_ref[bn * N_EXPERTS + e]
            base = e * cap + off_ref[bn * N_EXPERTS + e]

            def one(r, carry):
                t = lidx_smem[par, 0, e * TOK_BLOCK + r] - bn * TOK_BLOCK
                pltpu.make_async_copy(ye_hbm.at[base + r], slots_ref.at[par, e, t], sem_rows.at[par]).start()
                return carry

            lax.fori_loop(0, c, one, 0)

    @pl.when(b == 0)
    def _():
        slots_ref[...] = jnp.zeros(slots_ref.shape, _F32)
        idx_copy(0).start()
        idx_copy(0).wait()
        issue_block(0)
        if nb > 1:
            idx_copy(1).start()

    @pl.when(b + 1 < nb)
    def _():
        idx_copy(b + 1).wait()
        issue_block(b + 1)

    @pl.when(b + 2 < nb)
    def _():
        idx_copy(b + 2).start()

    for e in range(N_EXPERTS):
        c = cnt_ref[b * N_EXPERTS + e]

        @pl.when(c > 0)
        def _():
            pltpu.make_async_copy(ye_hbm.at[pl.ds(0, c)], slots_ref.at[slot, e, pl.ds(0, c)],
                                  sem_rows.at[slot]).wait()

    gsel = gsel_ref[...]
    parts = []
    for j in range(ROW_TILES):
        acc = ALPHA * h1_ref[:, j, :]
        for e in range(N_EXPERTS):
            acc = acc + gsel[:, e:e + 1] * slots_ref[slot, e, :, j, :]
        parts.append(acc)
    y_ref[...] = _layer_norm(jnp.concatenate(parts, axis=1), lg_ref[...], lb_ref[...])


def _combine(cnt_t, off_t, lidx, ye, gsel, h1, ln_g, ln_b, cap):
    T = h1.shape[0]
    nb = T // TOK_BLOCK
    return pl.pallas_call(
        functools.partial(_combine_kernel, cap=cap, nb=nb),
        grid_spec=pltpu.PrefetchScalarGridSpec(
            num_scalar_prefetch=2,
            grid=(nb,),
            in_specs=[
                pl.BlockSpec(memory_space=pl.ANY),
                pl.BlockSpec(memory_space=pl.ANY),
                pl.BlockSpec((TOK_BLOCK, LANES), lambda b, *_: (b, 0)),
                pl.BlockSpec((TOK_BLOCK, ROW_TILES, LANES), lambda b, *_: (b, 0, 0)),
                pl.BlockSpec((1, D_MODEL), lambda b, *_: (0, 0)),
                pl.BlockSpec((1, D_MODEL), lambda b, *_: (0, 0)),
            ],
            out_specs=pl.BlockSpec((TOK_BLOCK, D_MODEL), lambda b, *_: (b, 0)),
            scratch_shapes=[
                pltpu.VMEM((2, N_EXPERTS, TOK_BLOCK, ROW_TILES, LANES), _F32),
                pltpu.SMEM((2, 1, N_EXPERTS * TOK_BLOCK), jnp.int32),
                pltpu.SemaphoreType.DMA((2,)),
                pltpu.SemaphoreType.DMA((2,)),
            ],
        ),
        out_shape=jax.ShapeDtypeStruct((T, D_MODEL), _F32),
        compiler_params=_params(("arbitrary",)),
        name="combine",
    )(cnt_t, off_t, lidx, ye, gsel, h1, ln_g, ln_b)


def _tables(seq):
    half = HEAD_DIM // 2
    inv = 1.0 / (ROPE_BASE ** (jnp.arange(half, dtype=_F32) / half))
    ang = jnp.arange(seq, dtype=_F32)[:, None] * inv[None, :]
    cos = jnp.concatenate([jnp.cos(ang), jnp.cos(ang)], axis=1)
    sin = jnp.concatenate([-jnp.sin(ang), jnp.sin(ang)], axis=1)
    log_g = jnp.log1p(-jnp.exp2(-5.0 - jnp.arange(N_HEADS, dtype=_F32)))[:, None, None]
    pos = jnp.arange(CHUNK, dtype=_F32)
    rows = lambda f: jnp.broadcast_to(jnp.exp(log_g * f[None, :, None]), (N_HEADS, CHUNK, HEAD_DIM))
    dsym = jnp.exp(log_g * jnp.abs(pos[:, None] - pos[None, :])[None])
    tabs = dict(
        cos=cos, sin=sin, dsym=dsym,
        qwf=rows(pos + 1.0), kwf=rows(CHUNK - 1.0 - pos),
        qwb=rows(CHUNK - pos), kwb=rows(pos),
        gl=rows(jnp.full((CHUNK,), float(CHUNK), _F32)),
    )
    e_of_col = jnp.arange(N_EXPERTS * TOK_BLOCK) // TOK_BLOCK
    tabs["spread"] = (jnp.arange(LANES)[:, None] == e_of_col[None, :]).astype(_MXU)
    return tabs


def _trunk(x, w):
    batch, seq, _ = x.shape
    T = batch * seq
    nb = T // TOK_BLOCK
    cap = CAP_FACTOR * T // N_EXPERTS
    t = _tables(seq)
    h0, P, G2, GC, GR = _inproj(x.reshape(T, D_MODEL), seq, w["ln_in_g"], w["ln_in_b"], w["w_main"],
                                w["b_main"], w["wg"], w["bg"], w["wgt"], w["bgt"], t["cos"], t["sin"])
    YB = _sweep_bwd(P, GC, GR, t["qwb"], t["kwb"], t["gl"], batch, seq)
    mixed = _sweep_fwd(P, G2, GC, GR, YB, w["ret_g"], w["mlstm_g"], t["dsym"], t["qwf"], t["kwf"], t["gl"],
                       batch, seq)
    h1, aff = _outproj(mixed, h0, w["w_o"], w["ln1_g"], w["ln1_b"], w["w_r"])
    affc = aff[:, :N_EXPERTS].reshape(T // SUBLANES, LANES)
    thr, rem = _thresh(affc, cap)
    gsel, lidx, cnt, off = _select(aff, thr, rem, t["spread"])
    cnt2 = cnt.reshape(nb, LANES)[:, :N_EXPERTS]
    off2 = off.reshape(nb, LANES)[:, :N_EXPERTS]
    lidx_e = lidx.reshape(nb, N_EXPERTS, TOK_BLOCK).transpose(1, 0, 2)
    ye = _ffn(cnt2.T.reshape(-1), lidx_e, h1, w["w_gate"], w["w_up"], w["w_down"], cap, min(256, cap))
    y = _combine(cnt2.reshape(-1), off2.reshape(-1), lidx, ye, gsel, h1, w["ln2_g"], w["ln2_b"], cap)
    return y.reshape(batch, seq, D_MODEL)


def _prep_weights(ln_in_g, ln_in_b, w_in, b_in, ret_norm_g, mlstm_norm_g, w_o, ln1_g, ln1_b, w_router,
                  w_gate, w_up, w_down, ln2_g, ln2_b):
    main = 8 * SEC
    ngate = 4 * N_HEADS
    row = lambda v: v.reshape(1, -1).astype(_F32)
    wg = jnp.pad(w_in[0][:, main:main + ngate], ((0, 0), (0, LANES - ngate)))
    bg = jnp.pad(b_in[0][main:main + ngate], (0, LANES - ngate))
    return dict(
        ln_in_g=row(ln_in_g), ln_in_b=row(ln_in_b),
        w_main=w_in[0][:, :main].astype(_MXU), b_main=row(b_in[0][:main]),
        wg=wg.astype(_MXU), bg=row(bg), wgt=wg.T.astype(_MXU), bgt=bg.reshape(-1, 1).astype(_F32),
        ret_g=row(ret_norm_g[0]), mlstm_g=row(mlstm_norm_g[0]),
        w_o=w_o[0].astype(_MXU), ln1_g=row(ln1_g[0]), ln1_b=row(ln1_b[0]),
        w_r=jnp.pad(w_router[0], ((0, 0), (0, LANES - N_EXPERTS))).astype(_MXU),
        w_gate=w_gate[0].astype(_MXU), w_up=w_up[0].astype(_MXU), w_down=w_down[0].astype(_MXU),
        ln2_g=row(ln2_g[0]), ln2_b=row(ln2_b[0]),
    )


def kernel(x_prompt, x_sample, ln_in_g, ln_in_b, w_in, b_in, ret_norm_g, mlstm_norm_g, w_o, ln1_g, ln1_b,
           w_router, w_gate, w_up, w_down, ln2_g, ln2_b):
    w = _prep_weights(ln_in_g, ln_in_b, w_in, b_in, ret_norm_g, mlstm_norm_g, w_o, ln1_g, ln1_b, w_router,
                      w_gate, w_up, w_down, ln2_g, ln2_b)
    return (_trunk(x_prompt, w), _trunk(x_sample, w))
```

```python
import functools

import jax
import jax.numpy as jnp
from jax import lax
from jax.experimental import pallas as pl
from jax.experimental.pallas import tpu as pltpu

D_MODEL = 1024
N_HEADS = 4
HEAD_DIM = 128
SEC = N_HEADS * HEAD_DIM
CHUNK = 128
N_EXPERTS = 16
D_FF = 2 * D_MODEL
CAP_FACTOR = 2
ROPE_BASE = 10000.0
LN_EPS = 1e-5
NEG_BIG = -1e30
DEPTH = 1
ALPHA = (2.0 * DEPTH) ** 0.25
K_SCALE = HEAD_DIM ** -0.5
LANES = 128
SUBLANES = 8
TOK_BLOCK = 128
MIN_NORMAL_BITS = 0x00800000
VMEM_LIMIT = 56 * 1024 * 1024

_MXU = jnp.bfloat16
_F32 = jnp.float32


def _dot(a, b):
    return jnp.dot(a, b, preferred_element_type=_F32)


def _dot_nt(a, b):
    return lax.dot_general(a, b, (((1,), (1,)), ((), ())), preferred_element_type=_F32)


def _split3(x):
    x1 = x.astype(_MXU)
    r1 = x - x1.astype(_F32)
    x2 = r1.astype(_MXU)
    r2 = r1 - x2.astype(_F32)
    return x1, x2, r2.astype(_MXU)


def _dot01_left(a01, x):
    x1, x2, x3 = _split3(x)
    return _dot(a01, x1) + _dot(a01, x2) + _dot(a01, x3)


def _dot01_right(x, a01):
    x1, x2, x3 = _split3(x)
    return _dot(x1, a01) + _dot(x2, a01) + _dot(x3, a01)


def _layer_norm(x, g, b):
    mu = jnp.mean(x, axis=-1, keepdims=True)
    xc = x - mu
    var = jnp.mean(xc * xc, axis=-1, keepdims=True)
    return xc * lax.rsqrt(var + LN_EPS) * g + b


def _log_sigmoid(x):
    return jnp.minimum(x, 0.0) - jnp.log1p(jnp.exp(-jnp.abs(x)))


def _sigmoid(x):
    return 1.0 / (1.0 + jnp.exp(-x))


def _params(sem):
    return pltpu.CompilerParams(dimension_semantics=sem, vmem_limit_bytes=VMEM_LIMIT)


ROW_TILES = D_MODEL // LANES


ISSUE_UNROLL = 4


def _for_each(lo, hi, body):
    nblk = lax.shift_right_logical(hi - lo, ISSUE_UNROLL.bit_length() - 1)

    def block(k, carry):
        for u in range(ISSUE_UNROLL):
            body(lo + k * ISSUE_UNROLL + u)
        return carry

    def single(i, carry):
        body(i)
        return carry

    lax.fori_loop(0, nblk, block, 0)
    lax.fori_loop(lo + nblk * ISSUE_UNROLL, hi, single, 0)


def _row_view(buf, r, group0=0):
    return buf.at[group0 + lax.shift_right_logical(r, 3), :, jnp.bitwise_and(r, SUBLANES - 1), :]


def _matrix_value(buf):
    rows = buf.shape[0] * SUBLANES
    return jnp.concatenate([buf[:, j].reshape(rows, LANES) for j in range(ROW_TILES)], axis=1)


def _row_tile_copies(mat, hbm, row0, sem):
    rows = mat.shape[0]
    return [pltpu.make_async_copy(mat.at[:, pl.ds(j * LANES, LANES)], hbm.at[pl.ds(row0, rows), j, :], sem)
            for j in range(ROW_TILES)]


def _matrix_copies(hbm, row0, mat, sem):
    rows = mat.shape[0]
    return [pltpu.make_async_copy(hbm.at[pl.ds(row0, rows), j, :], mat.at[:, pl.ds(j * LANES, LANES)], sem)
            for j in range(ROW_TILES)]


def _pipelined_writeback(buf, sem, hbm, value, step, nsteps, rows):
    slot = step % 2

    @pl.when(step >= 2)
    def _():
        for c in _row_tile_copies(buf.at[slot], hbm, (step - 2) * rows, sem.at[slot]):
            c.wait()

    buf[slot] = value
    for c in _row_tile_copies(buf.at[slot], hbm, step * rows, sem.at[slot]):
        c.start()

    @pl.when(step == nsteps - 1)
    def _():
        if nsteps > 1:
            for c in _row_tile_copies(buf.at[1 - slot], hbm, (step - 1) * rows, sem.at[1 - slot]):
                c.wait()
        for c in _row_tile_copies(buf.at[slot], hbm, step * rows, sem.at[slot]):
            c.wait()


_P_COL = {0: 0, 1: 1, 2: 2, 4: 3, 5: 4, 6: 5}
_G2_COL = {3: 0, 7: 1}


def _inproj_kernel(x_ref, lg_ref, lb_ref, w_ref, b_ref, wg_ref, bg_ref, wgt_ref, bgt_ref,
                   cos_ref, sin_ref, h0_ref, p_ref, g2_ref, gc_ref, gr_ref):
    tm = x_ref.shape[0]
    h = _layer_norm(x_ref[...], lg_ref[...], lb_ref[...])
    h0_ref[...] = h
    hb = h.astype(_MXU)
    cos = cos_ref[...]
    sin = sin_ref[...]
    for sec in range(8):
        acc = _dot(hb, w_ref[:, sec * SEC:(sec + 1) * SEC]) + b_ref[:, sec * SEC:(sec + 1) * SEC]
        if sec in (0, 1):
            c0 = _P_COL[sec] * SEC
            for hh in range(N_HEADS):
                s = acc[:, hh * HEAD_DIM:(hh + 1) * HEAD_DIM]
                r = s * cos + pltpu.roll(s, HEAD_DIM // 2, 1) * sin
                if sec == 1:
                    r = r * K_SCALE
                p_ref[:, c0 + hh * HEAD_DIM:c0 + (hh + 1) * HEAD_DIM] = r.astype(p_ref.dtype)
        elif sec in _P_COL:
            if sec == 5:
                acc = acc * K_SCALE
            c0 = _P_COL[sec] * SEC
            p_ref[:, c0:c0 + SEC] = acc.astype(p_ref.dtype)
        else:
            c0 = _G2_COL[sec] * SEC
            g2_ref[:, c0:c0 + SEC] = acc

    pre = _dot(hb, wg_ref[...]) + bg_ref[...]
    pre_t = _dot_nt(wgt_ref[...], hb) + bgt_ref[...]
    row = lax.broadcasted_iota(jnp.int32, (CHUNK, CHUNK), 0)
    col = lax.broadcasted_iota(jnp.int32, (CHUNK, CHUNK), 1)
    tri_le = (col <= row).astype(_MXU)
    tri_ge = (col >= row).astype(_MXU)
    for c in range(tm // CHUNK):
        sl = slice(c * CHUNK, (c + 1) * CHUNK)
        blk = pre[sl, :]
        ls = _log_sigmoid(blk)
        pref = _dot01_left(tri_le, ls)
        suf = _dot01_left(tri_ge, ls)
        gc_ref[sl, :] = jnp.where((col >= 4) & (col < 8), pref,
                                  jnp.where((col >= 12) & (col < 16), suf, blk))
        blk_t = pre_t[:, sl]
        ls_t = _log_sigmoid(blk_t)
        pref_t = _dot01_right(ls_t, tri_ge)
        suf_t = _dot01_right(ls_t, tri_le)
        gr_ref[:, sl] = jnp.where((row >= 4) & (row < 8), pref_t,
                                  jnp.where((row >= 12) & (row < 16), suf_t, blk_t))


def _inproj(x2, seq, ln_g, ln_b, w_main, b_main, wg, bg, wgt, bgt, cos, sin, tm=512):
    T = x2.shape[0]
    nseq = seq // tm
    const = lambda i: (0, 0)
    return pl.pallas_call(
        _inproj_kernel,
        grid=(T // tm,),
        in_specs=[
            pl.BlockSpec((tm, D_MODEL), lambda i: (i, 0)),
            pl.BlockSpec((1, D_MODEL), const),
            pl.BlockSpec((1, D_MODEL), const),
            pl.BlockSpec((D_MODEL, 8 * SEC), const),
            pl.BlockSpec((1, 8 * SEC), const),
            pl.BlockSpec((D_MODEL, LANES), const),
            pl.BlockSpec((1, LANES), const),
            pl.BlockSpec((LANES, D_MODEL), const),
            pl.BlockSpec((LANES, 1), const),
            pl.BlockSpec((tm, HEAD_DIM), lambda i: (i % nseq, 0)),
            pl.BlockSpec((tm, HEAD_DIM), lambda i: (i % nseq, 0)),
        ],
        out_specs=[
            pl.BlockSpec((tm, D_MODEL), lambda i: (i, 0)),
            pl.BlockSpec((tm, 6 * SEC), lambda i: (i, 0)),
            pl.BlockSpec((tm, 2 * SEC), lambda i: (i, 0)),
            pl.BlockSpec((tm, LANES), lambda i: (i, 0)),
            pl.BlockSpec((LANES, tm), lambda i: (0, i)),
        ],
        out_shape=[
            jax.ShapeDtypeStruct((T, D_MODEL), _F32),
            jax.ShapeDtypeStruct((T, 6 * SEC), _MXU),
            jax.ShapeDtypeStruct((T, 2 * SEC), _F32),
            jax.ShapeDtypeStruct((T, LANES), _F32),
            jax.ShapeDtypeStruct((LANES, T), _F32),
        ],
        compiler_params=_params(("parallel",)),
        name="inproj",
    )(x2, ln_g, ln_b, w_main, b_main, wg, bg, wgt, bgt, cos, sin)


def _init_state(s_ref, c_ref, n_ref, m_ref):
    s_ref[...] = jnp.zeros(s_ref.shape, _F32)
    c_ref[...] = jnp.zeros(c_ref.shape, _F32)
    n_ref[...] = jnp.zeros(n_ref.shape, _F32)
    m_ref[...] = jnp.full(m_ref.shape, NEG_BIG, _F32)


def _mlstm_direction(q, k, v, gc, gr, ch_i, ch_f, last_lane, mask, c_ref, n_ref, m_ref, h):
    cum_col = gc[:, ch_f:ch_f + 1]
    li_col = gc[:, ch_i:ch_i + 1]
    cum_row = gr[ch_f:ch_f + 1, :]
    li_row = gr[ch_i:ch_i + 1, :]
    cum_last = cum_row[:, last_lane:last_lane + 1]
    m_prev = m_ref[h:h + 1, 0:1]
    c_prev = c_ref[h]
    n_prev = n_ref[h:h + 1, :]

    log_d = jnp.where(mask, cum_col - cum_row + li_row, NEG_BIG)
    inter = cum_col + m_prev
    m_row = jnp.maximum(jnp.max(log_d, axis=1, keepdims=True), inter)
    d_w = jnp.exp(log_d - m_row)
    s_inter = jnp.exp(inter - m_row)
    qk = _dot_nt(q, k) * d_w
    num = _dot(qk.astype(_MXU), v) + s_inter * _dot(q, c_prev.astype(_MXU))
    den = (jnp.sum(qk, axis=1, keepdims=True)
           + s_inter * jnp.sum(q.astype(_F32) * n_prev, axis=1, keepdims=True))
    h_out = num / jnp.maximum(jnp.abs(den), jnp.exp(-m_row))

    a_row = cum_last - cum_row + li_row
    a_max = jnp.max(a_row, axis=1, keepdims=True)
    w_col = jnp.exp(cum_last - cum_col + li_col - a_max)
    kw = k.astype(_F32) * w_col
    u_c = _dot(kw.T.astype(_MXU), v)
    u_n = jnp.sum(kw, axis=0, keepdims=True)
    m_new = jnp.maximum(cum_last + m_prev, a_max)
    s_old = jnp.exp(cum_last + m_prev - m_new)
    s_new = jnp.exp(a_max - m_new)
    c_ref[h] = s_old * c_prev + s_new * u_c
    n_ref[h:h + 1, :] = s_old * n_prev + s_new * u_n
    m_ref[h:h + 1, :] = jnp.broadcast_to(m_new, (1, LANES))
    return h_out


def _retention_state_update(k, v, kw_tab, gl_tab, s_ref, h):
    kw = k.astype(_F32) * kw_tab[h]
    s_ref[h] = gl_tab[h] * s_ref[h] + _dot(kw.T.astype(_MXU), v)


def _sweep_bwd_kernel(rq_ref, rk_ref, rv_ref, mq_ref, mk_ref, mv_ref, gc_ref, gr_ref,
                      qwb_ref, kwb_ref, gl_ref, yb_ref, s_ref, c_ref, n_ref, m_ref):
    @pl.when(pl.program_id(1) == 0)
    def _():
        _init_state(s_ref, c_ref, n_ref, m_ref)

    gc = gc_ref[...]
    gr = gr_ref[...]
    row = lax.broadcasted_iota(jnp.int32, (CHUNK, CHUNK), 0)
    col = lax.broadcasted_iota(jnp.int32, (CHUNK, CHUNK), 1)
    mask = col >= row
    for h in range(N_HEADS):
        sl = slice(h * HEAD_DIM, (h + 1) * HEAD_DIM)
        q, k, v = rq_ref[:, sl], rk_ref[:, sl], rv_ref[:, sl]
        qs = (q.astype(_F32) * qwb_ref[h]).astype(_MXU)
        yb_ref[:, sl] = _dot(qs, s_ref[h].astype(_MXU))
        _retention_state_update(k, v, kwb_ref, gl_ref, s_ref, h)
    for h in range(N_HEADS):
        sl = slice(h * HEAD_DIM, (h + 1) * HEAD_DIM)
        h_b = _mlstm_direction(mq_ref[:, sl], mk_ref[:, sl], mv_ref[:, sl], gc, gr,
                               8 + h, 12 + h, 0, mask, c_ref, n_ref, m_ref, h)
        yb_ref[:, SEC + h * HEAD_DIM:SEC + (h + 1) * HEAD_DIM] = h_b


def _sweep_fwd_kernel(rq_ref, rk_ref, rv_ref, mq_ref, mk_ref, mv_ref, gc_ref, gr_ref,
                      yb_ref, g2_ref, rng_ref, mng_ref, dsym_ref, qwf_ref, kwf_ref, gl_ref,
                      mixed_ref, s_ref, c_ref, n_ref, m_ref):
    @pl.when(pl.program_id(1) == 0)
    def _():
        _init_state(s_ref, c_ref, n_ref, m_ref)

    gc = gc_ref[...]
    gr = gr_ref[...]
    row = lax.broadcasted_iota(jnp.int32, (CHUNK, CHUNK), 0)
    col = lax.broadcasted_iota(jnp.int32, (CHUNK, CHUNK), 1)
    mask = col <= row

    def head_norm(y):
        mu = jnp.mean(y, axis=1, keepdims=True)
        yc = y - mu
        var = jnp.mean(yc * yc, axis=1, keepdims=True)
        return yc * lax.rsqrt(var + LN_EPS)

    for h in range(N_HEADS):
        sl = slice(h * HEAD_DIM, (h + 1) * HEAD_DIM)
        q, k, v = rq_ref[:, sl], rk_ref[:, sl], rv_ref[:, sl]
        p = (_dot_nt(q, k) * dsym_ref[h]).astype(_MXU)
        qs = (q.astype(_F32) * qwf_ref[h]).astype(_MXU)
        y = _dot(p, v) + _dot(qs, s_ref[h].astype(_MXU)) + yb_ref[:, sl]
        _retention_state_update(k, v, kwf_ref, gl_ref, s_ref, h)
        g = g2_ref[:, sl]
        out = head_norm(y) * rng_ref[:, sl] * (g * _sigmoid(g))
        mixed_ref[:, sl] = out.astype(mixed_ref.dtype)
    for h in range(N_HEADS):
        sl = slice(h * HEAD_DIM, (h + 1) * HEAD_DIM)
        sl2 = slice(SEC + h * HEAD_DIM, SEC + (h + 1) * HEAD_DIM)
        h_f = _mlstm_direction(mq_ref[:, sl], mk_ref[:, sl], mv_ref[:, sl], gc, gr,
                               h, 4 + h, CHUNK - 1, mask, c_ref, n_ref, m_ref, h)
        out = head_norm(h_f + yb_ref[:, sl2]) * mng_ref[:, sl] * _sigmoid(g2_ref[:, sl2])
        mixed_ref[:, sl2] = out.astype(mixed_ref.dtype)


def _state_scratch():
    return [
        pltpu.VMEM((N_HEADS, HEAD_DIM, HEAD_DIM), _F32),
        pltpu.VMEM((N_HEADS, HEAD_DIM, HEAD_DIM), _F32),
        pltpu.VMEM((SUBLANES, HEAD_DIM), _F32),
        pltpu.VMEM((SUBLANES, LANES), _F32),
    ]


def _sweep_specs(nchunk, reverse):
    def rb(b, n):
        return b * nchunk + ((nchunk - 1 - n) if reverse else n)
    p_specs = [pl.BlockSpec((CHUNK, SEC), functools.partial(lambda b, n, s: (rb(b, n), s), s=s))
               for s in range(6)]
    gc_spec = pl.BlockSpec((CHUNK, LANES), lambda b, n: (rb(b, n), 0))
    gr_spec = pl.BlockSpec((LANES, CHUNK), lambda b, n: (0, rb(b, n)))
    wide = pl.BlockSpec((CHUNK, 2 * SEC), lambda b, n: (rb(b, n), 0))
    tab = pl.BlockSpec((N_HEADS, CHUNK, HEAD_DIM), lambda b, n: (0, 0, 0))
    return p_specs, gc_spec, gr_spec, wide, tab


def _sweep_bwd(P, GC, GR, qwb, kwb, gl, batch, seq):
    T = P.shape[0]
    nchunk = seq // CHUNK
    p_specs, gc_spec, gr_spec, wide, tab = _sweep_specs(nchunk, True)
    return pl.pallas_call(
        _sweep_bwd_kernel,
        grid=(batch, nchunk),
        in_specs=p_specs + [gc_spec, gr_spec, tab, tab, tab],
        out_specs=wide,
        out_shape=jax.ShapeDtypeStruct((T, 2 * SEC), _F32),
        scratch_shapes=_state_scratch(),
        compiler_params=_params(("parallel", "arbitrary")),
        name="sweep_bwd",
    )(P, P, P, P, P, P, GC, GR, qwb, kwb, gl)


def _sweep_fwd(P, G2, GC, GR, YB, rng, mng, dsym, qwf, kwf, gl, batch, seq):
    T = P.shape[0]
    nchunk = seq // CHUNK
    p_specs, gc_spec, gr_spec, wide, tab = _sweep_specs(nchunk, False)
    gain = pl.BlockSpec((1, SEC), lambda b, n: (0, 0))
    return pl.pallas_call(
        _sweep_fwd_kernel,
        grid=(batch, nchunk),
        in_specs=p_specs + [gc_spec, gr_spec, wide, wide, gain, gain, tab, tab, tab, tab],
        out_specs=wide,
        out_shape=jax.ShapeDtypeStruct((T, 2 * SEC), _MXU),
        scratch_shapes=_state_scratch(),
        compiler_params=_params(("parallel", "arbitrary")),
        name="sweep_fwd",
    )(P, P, P, P, P, P, GC, GR, YB, G2, rng, mng, dsym, qwf, kwf, gl)


def _outproj_kernel(mixed_ref, h0_ref, wo_ref, lg_ref, lb_ref, wr_ref, h1_hbm, aff_ref, hbuf, sem, *, nsteps):
    z = ALPHA * h0_ref[...] + _dot(mixed_ref[...], wo_ref[...])
    h1 = _layer_norm(z, lg_ref[...], lb_ref[...])
    _pipelined_writeback(hbuf, sem, h1_hbm, h1, pl.program_id(0), nsteps, h1.shape[0])
    logits = _dot(h1.astype(_MXU), wr_ref[...])
    lane = lax.broadcasted_iota(jnp.int32, logits.shape, 1)
    valid = lane < N_EXPERTS
    logits = jnp.where(valid, logits, NEG_BIG)
    e = jnp.exp(logits - jnp.max(logits, axis=1, keepdims=True))
    aff = e / jnp.sum(e, axis=1, keepdims=True)
    aff_ref[...] = jnp.where(valid, aff, 0.0)


def _outproj(mixed, h0, wo, ln_g, ln_b, wr, tm=512):
    T = mixed.shape[0]
    const = lambda i: (0, 0)
    return pl.pallas_call(
        functools.partial(_outproj_kernel, nsteps=T // tm),
        grid=(T // tm,),
        in_specs=[
            pl.BlockSpec((tm, D_MODEL), lambda i: (i, 0)),
            pl.BlockSpec((tm, D_MODEL), lambda i: (i, 0)),
            pl.BlockSpec((D_MODEL, D_MODEL), const),
            pl.BlockSpec((1, D_MODEL), const),
            pl.BlockSpec((1, D_MODEL), const),
            pl.BlockSpec((D_MODEL, LANES), const),
        ],
        out_specs=[
            pl.BlockSpec(memory_space=pl.ANY),
            pl.BlockSpec((tm, LANES), lambda i: (i, 0)),
        ],
        out_shape=[
            jax.ShapeDtypeStruct((T, ROW_TILES, LANES), _F32),
            jax.ShapeDtypeStruct((T, LANES), _F32),
        ],
        scratch_shapes=[pltpu.VMEM((2, tm, D_MODEL), _F32), pltpu.SemaphoreType.DMA((2,))],
        compiler_params=_params(("arbitrary",)),
        name="outproj",
    )(mixed, h0, wo, ln_g, ln_b, wr)


def _thresh_kernel(aff_ref, thr_ref, rem_ref, *, cap):
    rows = aff_ref.shape[0]
    aff = aff_ref[...]

    def count(pred):
        c = jnp.sum(pred.astype(jnp.int32).reshape(rows // SUBLANES, SUBLANES, LANES), axis=0)
        c = jnp.broadcast_to(jnp.sum(c, axis=0, keepdims=True), (SUBLANES, LANES))
        for shift in (64, 32, 16):
            c = c + pltpu.roll(c, shift, 1)
        return c

    def body(i, ans):
        cand = ans | jnp.left_shift(jnp.int32(1), 30 - i)
        c = count(aff >= lax.bitcast_convert_type(cand[0:1, :], _F32))
        return jnp.where(c >= cap, cand, ans)

    ans = lax.fori_loop(0, 31, body, jnp.zeros((SUBLANES, LANES), jnp.int32))
    thr = jnp.where(ans >= MIN_NORMAL_BITS, lax.bitcast_convert_type(ans, _F32), 0.0)
    thr_ref[...] = thr
    rem_ref[...] = cap - count(aff > thr[0:1, :])


def _thresh(affc, cap):
    return pl.pallas_call(
        functools.partial(_thresh_kernel, cap=cap),
        out_shape=[jax.ShapeDtypeStruct((SUBLANES, LANES), _F32),
                   jax.ShapeDtypeStruct((SUBLANES, LANES), jnp.int32)],
        compiler_params=pltpu.CompilerParams(vmem_limit_bytes=VMEM_LIMIT),
        name="thresh",
    )(affc)


def _select_kernel(aff_ref, thr_ref, rem_ref, spread_ref, gsel_ref, lidx_ref, cnt_ref, off_ref,
                   nsel_ref, neq_ref):
    @pl.when(pl.program_id(0) == 0)
    def _():
        nsel_ref[...] = jnp.zeros(nsel_ref.shape, _F32)
        neq_ref[...] = jnp.zeros(neq_ref.shape, _F32)

    aff = aff_ref[...]
    thr = thr_ref[0:1, :]
    rem = rem_ref[0:1, :].astype(_F32)
    row = lax.broadcasted_iota(jnp.int32, (TOK_BLOCK, LANES), 0)
    lane = lax.broadcasted_iota(jnp.int32, (TOK_BLOCK, LANES), 1)
    valid = lane < N_EXPERTS
    before = (lane < row).astype(_MXU)
    gt = (aff > thr) & valid
    eq = (aff == thr) & valid
    eq_before = _dot(before, eq.astype(_MXU)) + neq_ref[0:1, :]
    sel = gt | (eq & (eq_before < rem))
    pos = _dot(before, sel.astype(_MXU))
    cnt = jnp.sum(sel.astype(_F32), axis=0, keepdims=True)
    off_ref[0] = nsel_ref[0:1, :].astype(jnp.int32)
    cnt_ref[0] = cnt.astype(jnp.int32)
    nsel_ref[0:1, :] = nsel_ref[0:1, :] + cnt
    neq_ref[0:1, :] = neq_ref[0:1, :] + jnp.sum(eq.astype(_F32), axis=0, keepdims=True)
    gsel_ref[...] = jnp.where(sel, aff, 0.0)

    ranked = jnp.where(sel, pos, -1.0).astype(_MXU)
    spread = _dot(ranked, spread_ref[...])
    slot = (lax.broadcasted_iota(jnp.int32, spread.shape, 1) % TOK_BLOCK).astype(_F32)
    tok = lax.broadcasted_iota(jnp.int32, spread.shape, 0) + pl.program_id(0) * TOK_BLOCK
    lidx_ref[0] = jnp.sum(jnp.where(spread == slot, tok, 0), axis=0, keepdims=True)


def _select(aff, thr, rem, spread):
    T = aff.shape[0]
    nb = T // TOK_BLOCK
    const = lambda b: (0, 0)
    return pl.pallas_call(
        _select_kernel,
        grid=(nb,),
        in_specs=[
            pl.BlockSpec((TOK_BLOCK, LANES), lambda b: (b, 0)),
            pl.BlockSpec((SUBLANES, LANES), const),
            pl.BlockSpec((SUBLANES, LANES), const),
            pl.BlockSpec((LANES, N_EXPERTS * TOK_BLOCK), const),
        ],
        out_specs=[
            pl.BlockSpec((TOK_BLOCK, LANES), lambda b: (b, 0)),
            pl.BlockSpec((1, 1, N_EXPERTS * TOK_BLOCK), lambda b: (b, 0, 0)),
            pl.BlockSpec((1, 1, LANES), lambda b: (b, 0, 0)),
            pl.BlockSpec((1, 1, LANES), lambda b: (b, 0, 0)),
        ],
        out_shape=[
            jax.ShapeDtypeStruct((T, LANES), _F32),
            jax.ShapeDtypeStruct((nb, 1, N_EXPERTS * TOK_BLOCK), jnp.int32),
            jax.ShapeDtypeStruct((nb, 1, LANES), jnp.int32),
            jax.ShapeDtypeStruct((nb, 1, LANES), jnp.int32),
        ],
        scratch_shapes=[pltpu.VMEM((SUBLANES, LANES), _F32), pltpu.VMEM((SUBLANES, LANES), _F32)],
        compiler_params=_params(("arbitrary",)),
        name="select",
    )(aff, thr, rem, spread)


def _ffn_kernel(cnt_ref, lidx_hbm, h1_hbm, wg_ref, wu_ref, wd_ref, y_hbm,
                xbuf, ybuf, lidx_smem, walk_ref, sem_idx, sem_rows, sem_out, *, tm, per, nb):
    e = pl.program_id(0)
    j = pl.program_id(1)
    step = e * per + j
    slot = step % 2

    per_expert = nb * TOK_BLOCK

    def idx_copy(en):
        return pltpu.make_async_copy(lidx_hbm.at[en], lidx_smem.at[pl.ds((en % 2) * per_expert, per_expert)],
                                     sem_idx.at[en % 2])

    def issue_tile(en, dst_slot):
        list_base = (en % 2) * per_expert
        group0 = dst_slot * (tm // SUBLANES)
        sem = sem_rows.at[dst_slot]

        def cond(st):
            return st[0] < tm

        def body(st):
            n, b, r = st
            c = cnt_ref[en * nb + b]
            take = jnp.minimum(c - r, tm - n)
            src_minus_dst = list_base + b * TOK_BLOCK + r - n

            def one(m):
                t = lidx_smem[src_minus_dst + m]
                pltpu.make_async_copy(h1_hbm.at[t], _row_view(xbuf, m, group0), sem).start()

            _for_each(n, n + take, one)
            done = r + take >= c
            return n + take, jnp.where(done, b + 1, b), jnp.where(done, 0, r + take)

        _, b, r = lax.while_loop(cond, body, (jnp.int32(0), walk_ref[0], walk_ref[1]))
        walk_ref[0] = b
        walk_ref[1] = r

    def restart_walk():
        walk_ref[0] = jnp.int32(0)
        walk_ref[1] = jnp.int32(0)

    @pl.when(step == 0)
    def _():
        idx_copy(0).start()
        idx_copy(0).wait()
        restart_walk()
        issue_tile(0, 0)

    @pl.when((j == 0) & (e + 1 < N_EXPERTS))
    def _():
        idx_copy(e + 1).start()

    @pl.when(j + 1 < per)
    def _():
        issue_tile(e, 1 - slot)

    @pl.when((j + 1 == per) & (e + 1 < N_EXPERTS))
    def _():
        idx_copy(e + 1).wait()
        restart_walk()
        issue_tile(e + 1, 1 - slot)

    pltpu.make_async_copy(h1_hbm.at[pl.ds(0, tm)], h1_hbm.at[pl.ds(0, tm)], sem_rows.at[slot]).wait()
    x = _matrix_value(xbuf.at[pl.ds(slot * (tm // SUBLANES), tm // SUBLANES)]).astype(_MXU)
    g = _dot(x, wg_ref[0])
    u = _dot(x, wu_ref[0])
    hid = (g * _sigmoid(g) * u).astype(_MXU)
    _pipelined_writeback(ybuf, sem_out, y_hbm, _dot(hid, wd_ref[0]), step, N_EXPERTS * per, tm)


def _ffn(cnt_e, lidx_e, h1, wg, wu, wd, cap, tm):
    per = cap // tm
    nb = lidx_e.shape[1] // TOK_BLOCK
    return pl.pallas_call(
        functools.partial(_ffn_kernel, tm=tm, per=per, nb=nb),
        grid_spec=pltpu.PrefetchScalarGridSpec(
            num_scalar_prefetch=1,
            grid=(N_EXPERTS, per),
            in_specs=[
                pl.BlockSpec(memory_space=pl.ANY),
                pl.BlockSpec(memory_space=pl.ANY),
                pl.BlockSpec((1, D_MODEL, D_FF), lambda e, j, *_: (e, 0, 0)),
                pl.BlockSpec((1, D_MODEL, D_FF), lambda e, j, *_: (e, 0, 0)),
                pl.BlockSpec((1, D_FF, D_MODEL), lambda e, j, *_: (e, 0, 0)),
            ],
            out_specs=pl.BlockSpec(memory_space=pl.ANY),
            scratch_shapes=[
                pltpu.VMEM((2 * tm // SUBLANES, ROW_TILES, SUBLANES, LANES), _F32),
                pltpu.VMEM((2, tm, D_MODEL), _F32),
                pltpu.SMEM((2 * nb * TOK_BLOCK,), jnp.int32),
                pltpu.SMEM((2,), jnp.int32),
                pltpu.SemaphoreType.DMA((2,)),
                pltpu.SemaphoreType.DMA((2,)),
                pltpu.SemaphoreType.DMA((2,)),
            ],
        ),
        out_shape=jax.ShapeDtypeStruct((N_EXPERTS * cap, ROW_TILES, LANES), _F32),
        compiler_params=_params(("arbitrary", "arbitrary")),
        name="ffn",
    )(cnt_e, lidx_e, h1, wg, wu, wd)


SLOT_GROUPS = TOK_BLOCK // SUBLANES


def _combine_kernel(cnt_ref, off_ref, lidx_hbm, ye_hbm, h1_hbm, gsel_ref, lg_ref, lb_ref, y_ref,
                    slots_ref, hres_ref, lidx_smem, sem_idx, sem_rows, sem_res, *, cap, nb):
    b = pl.program_id(0)
    slot = b % 2

    per_block = N_EXPERTS * TOK_BLOCK

    def idx_copy(bn):
        return pltpu.make_async_copy(lidx_hbm.at[bn], lidx_smem.at[pl.ds((bn % 2) * per_block, per_block)],
                                     sem_idx.at[bn % 2])

    def res_copies(bn):
        return _matrix_copies(h1_hbm, bn * TOK_BLOCK, hres_ref.at[bn % 2], sem_res.at[bn % 2])

    def issue_block(bn):
        par = bn % 2
        sem = sem_rows.at[par]
        tok0 = bn * TOK_BLOCK
        for c in res_copies(bn):
            c.start()
        for e in range(N_EXPERTS):
            c = cnt_ref[bn * N_EXPERTS + e]
            base = e * cap + off_ref[bn * N_EXPERTS + e]
            list_minus_src = par * per_block + e * TOK_BLOCK - base
            group0 = (par * N_EXPERTS + e) * SLOT_GROUPS

            def one(src_row):
                t = lidx_smem[list_minus_src + src_row] - tok0
                pltpu.make_async_copy(ye_hbm.at[src_row], _row_view(slots_ref, t, group0), sem).start()

            _for_each(base, base + c, one)

    @pl.when(b == 0)
    def _():
        slots_ref[...] = jnp.zeros(slots_ref.shape, _F32)
        idx_copy(0).start()
        idx_copy(0).wait()
        issue_block(0)
        if nb > 1:
            idx_copy(1).start()

    @pl.when(b + 1 < nb)
    def _():
        idx_copy(b + 1).wait()
        issue_block(b + 1)

    @pl.when(b + 2 < nb)
    def _():
        idx_copy(b + 2).start()

    for e in range(N_EXPERTS):
        c = cnt_ref[b * N_EXPERTS + e]

        @pl.when(c > 0)
        def _():
            pltpu.make_async_copy(ye_hbm.at[pl.ds(0, c)], ye_hbm.at[pl.ds(0, c)], sem_rows.at[slot]).wait()

    for c in res_copies(b):
        c.wait()

    gsel = gsel_ref[...]
    parts = []
    for j in range(ROW_TILES):
        acc = ALPHA * hres_ref[slot, :, j * LANES:(j + 1) * LANES]
        for e in range(N_EXPERTS):
            rows = slots_ref[pl.ds((slot * N_EXPERTS + e) * SLOT_GROUPS, SLOT_GROUPS), j]
            acc = acc + gsel[:, e:e + 1] * rows.reshape(TOK_BLOCK, LANES)
        parts.append(acc)
    y_ref[...] = _layer_norm(jnp.concatenate(parts, axis=1), lg_ref[...], lb_ref[...])


def _combine(cnt_t, off_t, lidx, ye, gsel, h1, ln_g, ln_b, cap):
    T = h1.shape[0]
    nb = T // TOK_BLOCK
    return pl.pallas_call(
        functools.partial(_combine_kernel, cap=cap, nb=nb),
        grid_spec=pltpu.PrefetchScalarGridSpec(
            num_scalar_prefetch=2,
            grid=(nb,),
            in_specs=[
                pl.BlockSpec(memory_space=pl.ANY),
                pl.BlockSpec(memory_space=pl.ANY),
                pl.BlockSpec(memory_space=pl.ANY),
                pl.BlockSpec((TOK_BLOCK, LANES), lambda b, *_: (b, 0)),
                pl.BlockSpec((1, D_MODEL), lambda b, *_: (0, 0)),
                pl.BlockSpec((1, D_MODEL), lambda b, *_: (0, 0)),
            ],
            out_specs=pl.BlockSpec((TOK_BLOCK, D_MODEL), lambda b, *_: (b, 0)),
            scratch_shapes=[
                pltpu.VMEM((2 * N_EXPERTS * SLOT_GROUPS, ROW_TILES, SUBLANES, LANES), _F32),
                pltpu.VMEM((2, TOK_BLOCK, D_MODEL), _F32),
                pltpu.SMEM((2 * N_EXPERTS * TOK_BLOCK,), jnp.int32),
                pltpu.SemaphoreType.DMA((2,)),
                pltpu.SemaphoreType.DMA((2,)),
                pltpu.SemaphoreType.DMA((2,)),
            ],
        ),
        out_shape=jax.ShapeDtypeStruct((T, D_MODEL), _F32),
        compiler_params=_params(("arbitrary",)),
        name="combine",
    )(cnt_t, off_t, lidx, ye, h1, gsel, ln_g, ln_b)


def _tables(seq):
    half = HEAD_DIM // 2
    inv = 1.0 / (ROPE_BASE ** (jnp.arange(half, dtype=_F32) / half))
    ang = jnp.arange(seq, dtype=_F32)[:, None] * inv[None, :]
    cos = jnp.concatenate([jnp.cos(ang), jnp.cos(ang)], axis=1)
    sin = jnp.concatenate([-jnp.sin(ang), jnp.sin(ang)], axis=1)
    log_g = jnp.log1p(-jnp.exp2(-5.0 - jnp.arange(N_HEADS, dtype=_F32)))[:, None, None]
    pos = jnp.arange(CHUNK, dtype=_F32)
    rows = lambda f: jnp.broadcast_to(jnp.exp(log_g * f[None, :, None]), (N_HEADS, CHUNK, HEAD_DIM))
    dsym = jnp.exp(log_g * jnp.abs(pos[:, None] - pos[None, :])[None])
    tabs = dict(
        cos=cos, sin=sin, dsym=dsym,
        qwf=rows(pos + 1.0), kwf=rows(CHUNK - 1.0 - pos),
        qwb=rows(CHUNK - pos), kwb=rows(pos),
        gl=rows(jnp.full((CHUNK,), float(CHUNK), _F32)),
    )
    e_of_col = jnp.arange(N_EXPERTS * TOK_BLOCK) // TOK_BLOCK
    tabs["spread"] = (jnp.arange(LANES)[:, None] == e_of_col[None, :]).astype(_MXU)
    return tabs


def _trunk(x, w):
    batch, seq, _ = x.shape
    T = batch * seq
    nb = T // TOK_BLOCK
    cap = CAP_FACTOR * T // N_EXPERTS
    t = _tables(seq)
    h0, P, G2, GC, GR = _inproj(x.reshape(T, D_MODEL), seq, w["ln_in_g"], w["ln_in_b"], w["w_main"],
                                w["b_main"], w["wg"], w["bg"], w["wgt"], w["bgt"], t["cos"], t["sin"])
    YB = _sweep_bwd(P, GC, GR, t["qwb"], t["kwb"], t["gl"], batch, seq)
    mixed = _sweep_fwd(P, G2, GC, GR, YB, w["ret_g"], w["mlstm_g"], t["dsym"], t["qwf"], t["kwf"], t["gl"],
                       batch, seq)
    h1, aff = _outproj(mixed, h0, w["w_o"], w["ln1_g"], w["ln1_b"], w["w_r"])
    affc = aff[:, :N_EXPERTS].reshape(T // SUBLANES, LANES)
    thr, rem = _thresh(affc, cap)
    gsel, lidx, cnt, off = _select(aff, thr, rem, t["spread"])
    cnt2 = cnt.reshape(nb, LANES)[:, :N_EXPERTS]
    off2 = off.reshape(nb, LANES)[:, :N_EXPERTS]
    lidx_e = lidx.reshape(nb, N_EXPERTS, TOK_BLOCK).transpose(1, 0, 2).reshape(N_EXPERTS, nb * TOK_BLOCK)
    ye = _ffn(cnt2.T.reshape(-1), lidx_e, h1, w["w_gate"], w["w_up"], w["w_down"], cap, min(256, cap))
    y = _combine(cnt2.reshape(-1), off2.reshape(-1), lidx.reshape(nb, N_EXPERTS * TOK_BLOCK), ye, gsel, h1,
                 w["ln2_g"], w["ln2_b"], cap)
    return y.reshape(batch, seq, D_MODEL)


def _prep_weights(ln_in_g, ln_in_b, w_in, b_in, ret_norm_g, mlstm_norm_g, w_o, ln1_g, ln1_b, w_router,
                  w_gate, w_up, w_down, ln2_g, ln2_b):
    main = 8 * SEC
    ngate = 4 * N_HEADS
    row = lambda v: v.reshape(1, -1).astype(_F32)
    wg = jnp.pad(w_in[0][:, main:main + ngate], ((0, 0), (0, LANES - ngate)))
    bg = jnp.pad(b_in[0][main:main + ngate], (0, LANES - ngate))
    return dict(
        ln_in_g=row(ln_in_g), ln_in_b=row(ln_in_b),
        w_main=w_in[0][:, :main].astype(_MXU), b_main=row(b_in[0][:main]),
        wg=wg.astype(_MXU), bg=row(bg), wgt=wg.T.astype(_MXU), bgt=bg.reshape(-1, 1).astype(_F32),
        ret_g=row(ret_norm_g[0]), mlstm_g=row(mlstm_norm_g[0]),
        w_o=w_o[0].astype(_MXU), ln1_g=row(ln1_g[0]), ln1_b=row(ln1_b[0]),
        w_r=jnp.pad(w_router[0], ((0, 0), (0, LANES - N_EXPERTS))).astype(_MXU),
        w_gate=w_gate[0].astype(_MXU), w_up=w_up[0].astype(_MXU), w_down=w_down[0].astype(_MXU),
        ln2_g=row(ln2_g[0]), ln2_b=row(ln2_b[0]),
    )


def kernel(x_prompt, x_sample, ln_in_g, ln_in_b, w_in, b_in, ret_norm_g, mlstm_norm_g, w_o, ln1_g, ln1_b,
           w_router, w_gate, w_up, w_down, ln2_g, ln2_b):
    w = _prep_weights(ln_in_g, ln_in_b, w_in, b_in, ret_norm_g, mlstm_norm_g, w_o, ln1_g, ln1_b, w_router,
                      w_gate, w_up, w_down, ln2_g, ln2_b)
    return (_trunk(x_prompt, w), _trunk(x_sample, w))
```

```python
import functools

import jax
import jax.numpy as jnp
from jax import lax
from jax.experimental import pallas as pl
from jax.experimental.pallas import tpu as pltpu

D_MODEL = 1024
N_HEADS = 4
HEAD_DIM = 128
SEC = N_HEADS * HEAD_DIM
CHUNK = 128
N_EXPERTS = 16
D_FF = 2 * D_MODEL
CAP_FACTOR = 2
ROPE_BASE = 10000.0
LN_EPS = 1e-5
NEG_BIG = -1e30
DEPTH = 1
ALPHA = (2.0 * DEPTH) ** 0.25
K_SCALE = HEAD_DIM ** -0.5
LANES = 128
SUBLANES = 8
TOK_BLOCK = 128
MIN_NORMAL_BITS = 0x00800000
VMEM_LIMIT = 56 * 1024 * 1024

_MXU = jnp.bfloat16
_F32 = jnp.float32


def _dot(a, b):
    return jnp.dot(a, b, preferred_element_type=_F32)


def _dot_nt(a, b):
    return lax.dot_general(a, b, (((1,), (1,)), ((), ())), preferred_element_type=_F32)


def _split3(x):
    x1 = x.astype(_MXU)
    r1 = x - x1.astype(_F32)
    x2 = r1.astype(_MXU)
    r2 = r1 - x2.astype(_F32)
    return x1, x2, r2.astype(_MXU)


def _dot01_left(a01, x):
    x1, x2, x3 = _split3(x)
    return _dot(a01, x1) + _dot(a01, x2) + _dot(a01, x3)


def _dot01_right(x, a01):
    x1, x2, x3 = _split3(x)
    return _dot(x1, a01) + _dot(x2, a01) + _dot(x3, a01)


def _layer_norm(x, g, b):
    mu = jnp.mean(x, axis=-1, keepdims=True)
    xc = x - mu
    var = jnp.mean(xc * xc, axis=-1, keepdims=True)
    return xc * lax.rsqrt(var + LN_EPS) * g + b


def _log_sigmoid(x):
    return jnp.minimum(x, 0.0) - jnp.log1p(jnp.exp(-jnp.abs(x)))


def _sigmoid(x):
    return 1.0 / (1.0 + jnp.exp(-x))


def _params(sem):
    return pltpu.CompilerParams(dimension_semantics=sem, vmem_limit_bytes=VMEM_LIMIT)


ROW_TILES = D_MODEL // LANES


ISSUE_UNROLL = 4


def _for_each(lo, hi, body):
    nblk = lax.shift_right_logical(hi - lo, ISSUE_UNROLL.bit_length() - 1)

    def block(k, carry):
        for u in range(ISSUE_UNROLL):
            body(lo + k * ISSUE_UNROLL + u)
        return carry

    def single(i, carry):
        body(i)
        return carry

    lax.fori_loop(0, nblk, block, 0)
    lax.fori_loop(lo + nblk * ISSUE_UNROLL, hi, single, 0)


def _row_view(buf, r, group0=0):
    return buf.at[group0 + lax.shift_right_logical(r, 3), :, jnp.bitwise_and(r, SUBLANES - 1), :]


def _matrix_value(buf):
    rows = buf.shape[0] * SUBLANES
    return jnp.concatenate([buf[:, j].reshape(rows, LANES) for j in range(ROW_TILES)], axis=1)


def _row_tile_copies(mat, hbm, row0, sem):
    rows = mat.shape[0]
    return [pltpu.make_async_copy(mat.at[:, pl.ds(j * LANES, LANES)], hbm.at[pl.ds(row0, rows), j, :], sem)
            for j in range(ROW_TILES)]


def _matrix_copies(hbm, row0, mat, sem):
    rows = mat.shape[0]
    return [pltpu.make_async_copy(hbm.at[pl.ds(row0, rows), j, :], mat.at[:, pl.ds(j * LANES, LANES)], sem)
            for j in range(ROW_TILES)]


def _pipelined_writeback(buf, sem, hbm, value, step, nsteps, rows):
    slot = step % 2

    @pl.when(step >= 2)
    def _():
        for c in _row_tile_copies(buf.at[slot], hbm, (step - 2) * rows, sem.at[slot]):
            c.wait()

    buf[slot] = value
    for c in _row_tile_copies(buf.at[slot], hbm, step * rows, sem.at[slot]):
        c.start()

    @pl.when(step == nsteps - 1)
    def _():
        if nsteps > 1:
            for c in _row_tile_copies(buf.at[1 - slot], hbm, (step - 1) * rows, sem.at[1 - slot]):
                c.wait()
        for c in _row_tile_copies(buf.at[slot], hbm, step * rows, sem.at[slot]):
            c.wait()


def _running_max(x, reverse):
    n = x.shape[0]
    row = lax.broadcasted_iota(jnp.int32, x.shape, 0)
    step = 1
    while step < n:
        if reverse:
            shifted = jnp.where(row < n - step, pltpu.roll(x, n - step, 0), NEG_BIG)
        else:
            shifted = jnp.where(row >= step, pltpu.roll(x, step, 0), NEG_BIG)
        x = jnp.maximum(x, shifted)
        step *= 2
    return x


_P_COL = {0: 0, 1: 1, 2: 2, 4: 3, 5: 4, 6: 5}
_G2_COL = {3: 0, 7: 1}


def _inproj_kernel(x_ref, lg_ref, lb_ref, w_ref, b_ref, wg_ref, bg_ref, wgt_ref, bgt_ref,
                   cos_ref, sin_ref, h0_ref, p_ref, g2_ref, gc_ref, gr_ref):
    tm = x_ref.shape[0]
    h = _layer_norm(x_ref[...], lg_ref[...], lb_ref[...])
    h0_ref[...] = h
    hb = h.astype(_MXU)
    cos = cos_ref[...]
    sin = sin_ref[...]
    for sec in range(8):
        acc = _dot(hb, w_ref[:, sec * SEC:(sec + 1) * SEC]) + b_ref[:, sec * SEC:(sec + 1) * SEC]
        if sec in (0, 1):
            c0 = _P_COL[sec] * SEC
            for hh in range(N_HEADS):
                s = acc[:, hh * HEAD_DIM:(hh + 1) * HEAD_DIM]
                r = s * cos + pltpu.roll(s, HEAD_DIM // 2, 1) * sin
                if sec == 1:
                    r = r * K_SCALE
                p_ref[:, c0 + hh * HEAD_DIM:c0 + (hh + 1) * HEAD_DIM] = r.astype(p_ref.dtype)
        elif sec in _P_COL:
            if sec == 5:
                acc = acc * K_SCALE
            c0 = _P_COL[sec] * SEC
            p_ref[:, c0:c0 + SEC] = acc.astype(p_ref.dtype)
        else:
            c0 = _G2_COL[sec] * SEC
            g2_ref[:, c0:c0 + SEC] = acc

    pre = _dot(hb, wg_ref[...]) + bg_ref[...]
    pre_t = _dot_nt(wgt_ref[...], hb) + bgt_ref[...]
    row = lax.broadcasted_iota(jnp.int32, (CHUNK, CHUNK), 0)
    col = lax.broadcasted_iota(jnp.int32, (CHUNK, CHUNK), 1)
    tri_le = (col <= row).astype(_MXU)
    tri_ge = (col >= row).astype(_MXU)
    for c in range(tm // CHUNK):
        sl = slice(c * CHUNK, (c + 1) * CHUNK)
        blk = pre[sl, :]
        ls = _log_sigmoid(blk)
        pref = _dot01_left(tri_le, ls)
        suf = _dot01_left(tri_ge, ls)
        is_cum_f = (col >= 4) & (col < 8)
        is_cum_b = (col >= 12) & (col < 16)
        cum = jnp.where(is_cum_f, pref, jnp.where(is_cum_b, suf, 0.0))
        excess = blk - pltpu.roll(cum, LANES - 4, 1)
        run_max = jnp.where(col < 4, _running_max(excess, False), _running_max(excess, True))
        is_max = ((col >= 16) & (col < 20)) | ((col >= 24) & (col < 28))
        gc_ref[sl, :] = jnp.where(is_cum_f | is_cum_b, cum,
                                  jnp.where(is_max, pltpu.roll(run_max, 16, 1), blk))
        blk_t = pre_t[:, sl]
        ls_t = _log_sigmoid(blk_t)
        pref_t = _dot01_right(ls_t, tri_ge)
        suf_t = _dot01_right(ls_t, tri_le)
        gr_ref[:, sl] = jnp.where((row >= 4) & (row < 8), pref_t,
                                  jnp.where((row >= 12) & (row < 16), suf_t, blk_t))


def _inproj(x2, seq, ln_g, ln_b, w_main, b_main, wg, bg, wgt, bgt, cos, sin, tm=512):
    T = x2.shape[0]
    nseq = seq // tm
    const = lambda i: (0, 0)
    return pl.pallas_call(
        _inproj_kernel,
        grid=(T // tm,),
        in_specs=[
            pl.BlockSpec((tm, D_MODEL), lambda i: (i, 0)),
            pl.BlockSpec((1, D_MODEL), const),
            pl.BlockSpec((1, D_MODEL), const),
            pl.BlockSpec((D_MODEL, 8 * SEC), const),
            pl.BlockSpec((1, 8 * SEC), const),
            pl.BlockSpec((D_MODEL, LANES), const),
            pl.BlockSpec((1, LANES), const),
            pl.BlockSpec((LANES, D_MODEL), const),
            pl.BlockSpec((LANES, 1), const),
            pl.BlockSpec((tm, HEAD_DIM), lambda i: (i % nseq, 0)),
            pl.BlockSpec((tm, HEAD_DIM), lambda i: (i % nseq, 0)),
        ],
        out_specs=[
            pl.BlockSpec((tm, D_MODEL), lambda i: (i, 0)),
            pl.BlockSpec((tm, 6 * SEC), lambda i: (i, 0)),
            pl.BlockSpec((tm, 2 * SEC), lambda i: (i, 0)),
            pl.BlockSpec((tm, LANES), lambda i: (i, 0)),
            pl.BlockSpec((LANES, tm), lambda i: (0, i)),
        ],
        out_shape=[
            jax.ShapeDtypeStruct((T, D_MODEL), _F32),
            jax.ShapeDtypeStruct((T, 6 * SEC), _MXU),
            jax.ShapeDtypeStruct((T, 2 * SEC), _F32),
            jax.ShapeDtypeStruct((T, LANES), _F32),
            jax.ShapeDtypeStruct((LANES, T), _F32),
        ],
        compiler_params=_params(("parallel",)),
        name="inproj",
    )(x2, ln_g, ln_b, w_main, b_main, wg, bg, wgt, bgt, cos, sin)


def _init_state(s_ref, cn_ref, m_ref):
    s_ref[...] = jnp.zeros(s_ref.shape, _F32)
    cn_ref[...] = jnp.zeros(cn_ref.shape, _F32)
    m_ref[...] = jnp.full(m_ref.shape, NEG_BIG, _F32)


def _dot_tn(a, b):
    return lax.dot_general(a, b, (((0,), (0,)), ((), ())), preferred_element_type=_F32)


def _lane_spread(gc, spread_ref):
    return _dot01_right(gc, spread_ref[...])


def _mlstm_direction(q, k, v, li_b, cum_b, max_b, li_row, cum_row, last_lane, mask, cn_ref, m_ref, si):
    cum_last = cum_row[:, last_lane:last_lane + 1]
    m_prev = m_ref[si, 0:1, :]
    cn_prev = cn_ref[si]
    ones = jnp.ones((CHUNK, HEAD_DIM), _MXU)
    v1 = jnp.concatenate([v, ones], axis=1)

    m_row = cum_b + jnp.maximum(max_b, m_prev)
    log_d = jnp.where(mask, cum_b - cum_row + li_row, NEG_BIG)
    d_w = jnp.exp(log_d - m_row)
    s_inter = jnp.exp(cum_b + m_prev - m_row)
    qk = _dot_nt(q, k) * d_w
    qk_hi = qk.astype(_MXU)
    qk_lo = (qk - qk_hi.astype(_F32)).astype(_MXU)
    intra = _dot(qk_hi, v1)
    inter = _dot(q, cn_prev.astype(_MXU))
    num = intra[:, :HEAD_DIM] + s_inter * inter[:, :HEAD_DIM]
    den = intra[:, HEAD_DIM:] + _dot(qk_lo, ones) + s_inter * inter[:, HEAD_DIM:]
    h_out = num / jnp.maximum(jnp.abs(den), jnp.exp(-m_row))

    a_max = jnp.max(cum_last - cum_row + li_row, axis=1, keepdims=True)
    kw = (k.astype(_F32) * jnp.exp(cum_last - cum_b + li_b - a_max)).astype(_MXU)
    m_new = jnp.maximum(cum_last + m_prev, a_max)
    s_old = jnp.exp(cum_last + m_prev - m_new)
    s_new = jnp.exp(a_max - m_new)
    cn_ref[si] = (jnp.concatenate([s_old, s_old], axis=1) * cn_prev
                  + jnp.concatenate([s_new, s_new], axis=1) * _dot_tn(kw, v1))
    m_ref[si] = jnp.broadcast_to(m_new, (SUBLANES, LANES))
    return h_out


def _retention_state_update(k, v, kw, gl, s_ref, si):
    kwv = (k.astype(_F32) * kw).astype(_MXU)
    s_ref[si] = gl * s_ref[si] + _dot_tn(kwv, v)


SEQ_PER_STEP = 2


def _head_cols(h, base=0):
    return slice(base + h * HEAD_DIM, base + (h + 1) * HEAD_DIM)


def _sweep_bwd_kernel(*refs):
    seq_refs = [refs[8 * k:8 * (k + 1)] for k in range(SEQ_PER_STEP)]
    qwb_ref, kwb_ref, gl_ref, spread_ref, yb_ref, s_ref, cn_ref, m_ref = refs[8 * SEQ_PER_STEP:]

    @pl.when(pl.program_id(1) == 0)
    def _():
        _init_state(s_ref, cn_ref, m_ref)

    row = lax.broadcasted_iota(jnp.int32, (CHUNK, CHUNK), 0)
    col = lax.broadcasted_iota(jnp.int32, (CHUNK, CHUNK), 1)
    mask = col >= row
    for h in range(N_HEADS):
        sl = _head_cols(h)
        for k in range(SEQ_PER_STEP):
            rq_ref, rk_ref, rv_ref = seq_refs[k][0:3]
            si = k * N_HEADS + h
            qs = (rq_ref[:, sl].astype(_F32) * qwb_ref[h]).astype(_MXU)
            yb_ref[k, :, sl] = _dot(qs, s_ref[si].astype(_MXU))
            _retention_state_update(rk_ref[:, sl], rv_ref[:, sl], kwb_ref[h], gl_ref[h], s_ref, si)
    cols = [_lane_spread(seq_refs[k][6][...], spread_ref) for k in range(SEQ_PER_STEP)]
    for h in range(N_HEADS):
        sl = _head_cols(h)
        for k in range(SEQ_PER_STEP):
            mq_ref, mk_ref, mv_ref, _, gr_ref = seq_refs[k][3:8]
            h_b = _mlstm_direction(mq_ref[:, sl], mk_ref[:, sl], mv_ref[:, sl],
                                   cols[k][:, _head_cols(h)], cols[k][:, _head_cols(h, SEC)],
                                   cols[k][:, _head_cols(h, 2 * SEC)],
                                   gr_ref[8 + h:9 + h, :], gr_ref[12 + h:13 + h, :], 0, mask,
                                   cn_ref, m_ref, k * N_HEADS + h)
            yb_ref[k, :, _head_cols(h, SEC)] = h_b


def _sweep_fwd_kernel(*refs):
    seq_refs = [refs[10 * k:10 * (k + 1)] for k in range(SEQ_PER_STEP)]
    (rng_ref, mng_ref, dsym_ref, qwf_ref, kwf_ref, gl_ref, spread_ref,
     mixed_ref, s_ref, cn_ref, m_ref) = refs[10 * SEQ_PER_STEP:]

    @pl.when(pl.program_id(1) == 0)
    def _():
        _init_state(s_ref, cn_ref, m_ref)

    row = lax.broadcasted_iota(jnp.int32, (CHUNK, CHUNK), 0)
    col = lax.broadcasted_iota(jnp.int32, (CHUNK, CHUNK), 1)
    mask = col <= row

    pairs = [(h, k) for h in range(N_HEADS) for k in range(SEQ_PER_STEP)]

    def head_norms(ys):
        centred = [y - m for y, m in zip(ys, [jnp.mean(y, axis=1, keepdims=True) for y in ys])]
        var = [jnp.mean(c * c, axis=1, keepdims=True) for c in centred]
        return [c * lax.rsqrt(v + LN_EPS) for c, v in zip(centred, var)]

    qkv = [[seq_refs[k][i][:, _head_cols(h)] for i in range(3)] for h, k in pairs]
    scores = [_dot_nt(q, kk) for q, kk, _ in qkv]
    inter = [_dot((q.astype(_F32) * qwf_ref[h]).astype(_MXU), s_ref[k * N_HEADS + h].astype(_MXU))
             for (h, k), (q, _, _) in zip(pairs, qkv)]
    ys = [_dot((s * dsym_ref[h]).astype(_MXU), v) + it + seq_refs[k][8][:, _head_cols(h)]
          for (h, k), (_, _, v), s, it in zip(pairs, qkv, scores, inter)]
    for (h, k), (_, kk, v) in zip(pairs, qkv):
        _retention_state_update(kk, v, kwf_ref[h], gl_ref[h], s_ref, k * N_HEADS + h)
    for (h, k), yn in zip(pairs, head_norms(ys)):
        sl = _head_cols(h)
        g = seq_refs[k][9][:, sl]
        mixed_ref[k, :, sl] = (yn * rng_ref[:, sl] * (g * _sigmoid(g))).astype(mixed_ref.dtype)

    cols = [_lane_spread(seq_refs[k][6][...], spread_ref) for k in range(SEQ_PER_STEP)]
    ys = []
    for h, k in pairs:
        sl = _head_cols(h)
        mq_ref, mk_ref, mv_ref, _, gr_ref, yb_ref = seq_refs[k][3:9]
        h_f = _mlstm_direction(mq_ref[:, sl], mk_ref[:, sl], mv_ref[:, sl],
                               cols[k][:, _head_cols(h)], cols[k][:, _head_cols(h, SEC)],
                               cols[k][:, _head_cols(h, 2 * SEC)],
                               gr_ref[h:h + 1, :], gr_ref[4 + h:5 + h, :], CHUNK - 1, mask,
                               cn_ref, m_ref, k * N_HEADS + h)
        ys.append(h_f + yb_ref[:, _head_cols(h, SEC)])
    for (h, k), yn in zip(pairs, head_norms(ys)):
        sl2 = _head_cols(h, SEC)
        out = yn * mng_ref[:, _head_cols(h)] * _sigmoid(seq_refs[k][9][:, sl2])
        mixed_ref[k, :, sl2] = out.astype(mixed_ref.dtype)


def _state_scratch():
    return [
        pltpu.VMEM((SEQ_PER_STEP * N_HEADS, HEAD_DIM, HEAD_DIM), _F32),
        pltpu.VMEM((SEQ_PER_STEP * N_HEADS, HEAD_DIM, 2 * HEAD_DIM), _F32),
        pltpu.VMEM((SEQ_PER_STEP * N_HEADS, SUBLANES, LANES), _F32),
    ]


_SPREAD_SPEC = pl.BlockSpec((LANES, 3 * SEC), lambda b, n: (0, 0))


def _sweep_specs(nchunk, reverse, n_wide):
    def chunk(n):
        return (nchunk - 1 - n) if reverse else n

    per_seq = []
    for k in range(SEQ_PER_STEP):
        def rb(b, n, k=k):
            return (b * SEQ_PER_STEP + k) * nchunk + chunk(n)
        per_seq += [pl.BlockSpec((CHUNK, SEC), functools.partial(lambda b, n, s, rb: (rb(b, n), s), s=s, rb=rb))
                    for s in range(6)]
        per_seq.append(pl.BlockSpec((CHUNK, LANES), functools.partial(lambda b, n, rb: (rb(b, n), 0), rb=rb)))
        per_seq.append(pl.BlockSpec((LANES, CHUNK), functools.partial(lambda b, n, rb: (0, rb(b, n)), rb=rb)))
        per_seq += [pl.BlockSpec((CHUNK, 2 * SEC), functools.partial(lambda b, n, rb: (rb(b, n), 0), rb=rb))
                    for _ in range(n_wide)]
    out = pl.BlockSpec((SEQ_PER_STEP, CHUNK, 2 * SEC), lambda b, n: (b, chunk(n), 0))
    tab = pl.BlockSpec((N_HEADS, CHUNK, HEAD_DIM), lambda b, n: (0, 0, 0))
    return per_seq, out, tab


def _sweep_bwd(P, GC, GR, qwb, kwb, gl, spread, batch, seq):
    nchunk = seq // CHUNK
    per_seq, out, tab = _sweep_specs(nchunk, True, 0)
    yb = pl.pallas_call(
        _sweep_bwd_kernel,
        grid=(batch // SEQ_PER_STEP, nchunk),
        in_specs=per_seq + [tab, tab, tab, _SPREAD_SPEC],
        out_specs=out,
        out_shape=jax.ShapeDtypeStruct((batch, seq, 2 * SEC), _F32),
        scratch_shapes=_state_scratch(),
        compiler_params=_params(("parallel", "arbitrary")),
        name="sweep_bwd",
    )(*([P] * 6 + [GC, GR]) * SEQ_PER_STEP, qwb, kwb, gl, spread)
    return yb.reshape(batch * seq, 2 * SEC)


def _sweep_fwd(P, G2, GC, GR, YB, rng, mng, dsym, qwf, kwf, gl, spread, batch, seq):
    nchunk = seq // CHUNK
    per_seq, out, tab = _sweep_specs(nchunk, False, 2)
    gain = pl.BlockSpec((1, SEC), lambda b, n: (0, 0))
    mixed = pl.pallas_call(
        _sweep_fwd_kernel,
        grid=(batch // SEQ_PER_STEP, nchunk),
        in_specs=per_seq + [gain, gain, tab, tab, tab, tab, _SPREAD_SPEC],
        out_specs=out,
        out_shape=jax.ShapeDtypeStruct((batch, seq, 2 * SEC), _MXU),
        scratch_shapes=_state_scratch(),
        compiler_params=_params(("parallel", "arbitrary")),
        name="sweep_fwd",
    )(*([P] * 6 + [GC, GR, YB, G2]) * SEQ_PER_STEP, rng, mng, dsym, qwf, kwf, gl, spread)
    return mixed.reshape(batch * seq, 2 * SEC)


def _outproj_kernel(mixed_ref, h0_ref, wo_ref, lg_ref, lb_ref, wr_ref, h1_hbm, aff_ref, hbuf, sem, *, nsteps):
    z = ALPHA * h0_ref[...] + _dot(mixed_ref[...], wo_ref[...])
    h1 = _layer_norm(z, lg_ref[...], lb_ref[...])
    _pipelined_writeback(hbuf, sem, h1_hbm, h1, pl.program_id(0), nsteps, h1.shape[0])
    logits = _dot(h1.astype(_MXU), wr_ref[...])
    lane = lax.broadcasted_iota(jnp.int32, logits.shape, 1)
    valid = lane < N_EXPERTS
    logits = jnp.where(valid, logits, NEG_BIG)
    e = jnp.exp(logits - jnp.max(logits, axis=1, keepdims=True))
    aff = e / jnp.sum(e, axis=1, keepdims=True)
    aff_ref[...] = jnp.where(valid, aff, 0.0)


def _outproj(mixed, h0, wo, ln_g, ln_b, wr, tm=512):
    T = mixed.shape[0]
    const = lambda i: (0, 0)
    return pl.pallas_call(
        functools.partial(_outproj_kernel, nsteps=T // tm),
        grid=(T // tm,),
        in_specs=[
            pl.BlockSpec((tm, D_MODEL), lambda i: (i, 0)),
            pl.BlockSpec((tm, D_MODEL), lambda i: (i, 0)),
            pl.BlockSpec((D_MODEL, D_MODEL), const),
            pl.BlockSpec((1, D_MODEL), const),
            pl.BlockSpec((1, D_MODEL), const),
            pl.BlockSpec((D_MODEL, LANES), const),
        ],
        out_specs=[
            pl.BlockSpec(memory_space=pl.ANY),
            pl.BlockSpec((tm, LANES), lambda i: (i, 0)),
        ],
        out_shape=[
            jax.ShapeDtypeStruct((T, ROW_TILES, LANES), _F32),
            jax.ShapeDtypeStruct((T, LANES), _F32),
        ],
        scratch_shapes=[pltpu.VMEM((2, tm, D_MODEL), _F32), pltpu.SemaphoreType.DMA((2,))],
        compiler_params=_params(("arbitrary",)),
        name="outproj",
    )(mixed, h0, wo, ln_g, ln_b, wr)


def _thresh_kernel(aff_ref, thr_ref, rem_ref, *, cap):
    rows = aff_ref.shape[0]
    aff = aff_ref[...]

    def count(pred):
        c = jnp.sum(pred.astype(jnp.int32).reshape(rows // SUBLANES, SUBLANES, LANES), axis=0)
        c = jnp.broadcast_to(jnp.sum(c, axis=0, keepdims=True), (SUBLANES, LANES))
        for shift in (64, 32, 16):
            c = c + pltpu.roll(c, shift, 1)
        return c

    def body(i, ans):
        cand = ans | jnp.left_shift(jnp.int32(1), 30 - i)
        c = count(aff >= lax.bitcast_convert_type(cand[0:1, :], _F32))
        return jnp.where(c >= cap, cand, ans)

    ans = lax.fori_loop(0, 31, body, jnp.zeros((SUBLANES, LANES), jnp.int32))
    thr = jnp.where(ans >= MIN_NORMAL_BITS, lax.bitcast_convert_type(ans, _F32), 0.0)
    thr_ref[...] = thr
    rem_ref[...] = cap - count(aff > thr[0:1, :])


def _thresh(affc, cap):
    return pl.pallas_call(
        functools.partial(_thresh_kernel, cap=cap),
        out_shape=[jax.ShapeDtypeStruct((SUBLANES, LANES), _F32),
                   jax.ShapeDtypeStruct((SUBLANES, LANES), jnp.int32)],
        compiler_params=pltpu.CompilerParams(vmem_limit_bytes=VMEM_LIMIT),
        name="thresh",
    )(affc)


def _select_kernel(aff_ref, thr_ref, rem_ref, spread_ref, gsel_ref, lidx_ref, cnt_ref, off_ref,
                   nsel_ref, neq_ref):
    @pl.when(pl.program_id(0) == 0)
    def _():
        nsel_ref[...] = jnp.zeros(nsel_ref.shape, _F32)
        neq_ref[...] = jnp.zeros(neq_ref.shape, _F32)

    aff = aff_ref[...]
    thr = thr_ref[0:1, :]
    rem = rem_ref[0:1, :].astype(_F32)
    row = lax.broadcasted_iota(jnp.int32, (TOK_BLOCK, LANES), 0)
    lane = lax.broadcasted_iota(jnp.int32, (TOK_BLOCK, LANES), 1)
    valid = lane < N_EXPERTS
    before = (lane < row).astype(_MXU)
    gt = (aff > thr) & valid
    eq = (aff == thr) & valid
    eq_before = _dot(before, eq.astype(_MXU)) + neq_ref[0:1, :]
    sel = gt | (eq & (eq_before < rem))
    pos = _dot(before, sel.astype(_MXU))
    cnt = jnp.sum(sel.astype(_F32), axis=0, keepdims=True)
    off_ref[0] = nsel_ref[0:1, :].astype(jnp.int32)
    cnt_ref[0] = cnt.astype(jnp.int32)
    nsel_ref[0:1, :] = nsel_ref[0:1, :] + cnt
    neq_ref[0:1, :] = neq_ref[0:1, :] + jnp.sum(eq.astype(_F32), axis=0, keepdims=True)
    gsel_ref[...] = jnp.where(sel, aff, 0.0)

    ranked = jnp.where(sel, pos, -1.0).astype(_MXU)
    spread = _dot(ranked, spread_ref[...])
    slot = (lax.broadcasted_iota(jnp.int32, spread.shape, 1) % TOK_BLOCK).astype(_F32)
    tok = lax.broadcasted_iota(jnp.int32, spread.shape, 0) + pl.program_id(0) * TOK_BLOCK
    lidx_ref[0] = jnp.sum(jnp.where(spread == slot, tok, 0), axis=0, keepdims=True)


def _select(aff, thr, rem, spread):
    T = aff.shape[0]
    nb = T // TOK_BLOCK
    const = lambda b: (0, 0)
    return pl.pallas_call(
        _select_kernel,
        grid=(nb,),
        in_specs=[
            pl.BlockSpec((TOK_BLOCK, LANES), lambda b: (b, 0)),
            pl.BlockSpec((SUBLANES, LANES), const),
            pl.BlockSpec((SUBLANES, LANES), const),
            pl.BlockSpec((LANES, N_EXPERTS * TOK_BLOCK), const),
        ],
        out_specs=[
            pl.BlockSpec((TOK_BLOCK, LANES), lambda b: (b, 0)),
            pl.BlockSpec((1, 1, N_EXPERTS * TOK_BLOCK), lambda b: (b, 0, 0)),
            pl.BlockSpec((1, 1, LANES), lambda b: (b, 0, 0)),
            pl.BlockSpec((1, 1, LANES), lambda b: (b, 0, 0)),
        ],
        out_shape=[
            jax.ShapeDtypeStruct((T, LANES), _F32),
            jax.ShapeDtypeStruct((nb, 1, N_EXPERTS * TOK_BLOCK), jnp.int32),
            jax.ShapeDtypeStruct((nb, 1, LANES), jnp.int32),
            jax.ShapeDtypeStruct((nb, 1, LANES), jnp.int32),
        ],
        scratch_shapes=[pltpu.VMEM((SUBLANES, LANES), _F32), pltpu.VMEM((SUBLANES, LANES), _F32)],
        compiler_params=_params(("arbitrary",)),
        name="select",
    )(aff, thr, rem, spread)


def _ffn_kernel(cnt_ref, lidx_hbm, h1_hbm, wg_ref, wu_ref, wd_ref, y_hbm,
                xbuf, ybuf, lidx_smem, walk_ref, sem_idx, sem_rows, sem_out, *, tm, per, nb):
    e = pl.program_id(0)
    j = pl.program_id(1)
    step = e * per + j
    slot = step % 2

    per_expert = nb * TOK_BLOCK

    def idx_copy(en):
        return pltpu.make_async_copy(lidx_hbm.at[en], lidx_smem.at[pl.ds((en % 2) * per_expert, per_expert)],
                                     sem_idx.at[en % 2])

    def issue_tile(en, dst_slot):
        list_base = (en % 2) * per_expert
        group0 = dst_slot * (tm // SUBLANES)
        sem = sem_rows.at[dst_slot]

        def cond(st):
            return st[0] < tm

        def body(st):
            n, b, r = st
            c = cnt_ref[en * nb + b]
            take = jnp.minimum(c - r, tm - n)
            src_minus_dst = list_base + b * TOK_BLOCK + r - n

            def one(m):
                t = lidx_smem[src_minus_dst + m]
                pltpu.make_async_copy(h1_hbm.at[t], _row_view(xbuf, m, group0), sem).start()

            _for_each(n, n + take, one)
            done = r + take >= c
            return n + take, jnp.where(done, b + 1, b), jnp.where(done, 0, r + take)

        _, b, r = lax.while_loop(cond, body, (jnp.int32(0), walk_ref[0], walk_ref[1]))
        walk_ref[0] = b
        walk_ref[1] = r

    def restart_walk():
        walk_ref[0] = jnp.int32(0)
        walk_ref[1] = jnp.int32(0)

    @pl.when(step == 0)
    def _():
        idx_copy(0).start()
        idx_copy(0).wait()
        restart_walk()
        issue_tile(0, 0)

    @pl.when((j == 0) & (e + 1 < N_EXPERTS))
    def _():
        idx_copy(e + 1).start()

    @pl.when(j + 1 < per)
    def _():
        issue_tile(e, 1 - slot)

    @pl.when((j + 1 == per) & (e + 1 < N_EXPERTS))
    def _():
        idx_copy(e + 1).wait()
        restart_walk()
        issue_tile(e + 1, 1 - slot)

    pltpu.make_async_copy(h1_hbm.at[pl.ds(0, tm)], h1_hbm.at[pl.ds(0, tm)], sem_rows.at[slot]).wait()
    x = _matrix_value(xbuf.at[pl.ds(slot * (tm // SUBLANES), tm // SUBLANES)]).astype(_MXU)
    g = _dot(x, wg_ref[0])
    u = _dot(x, wu_ref[0])
    hid = (g * _sigmoid(g) * u).astype(_MXU)
    _pipelined_writeback(ybuf, sem_out, y_hbm, _dot(hid, wd_ref[0]), step, N_EXPERTS * per, tm)


def _ffn(cnt_e, lidx_e, h1, wg, wu, wd, cap, tm):
    per = cap // tm
    nb = lidx_e.shape[1] // TOK_BLOCK
    return pl.pallas_call(
        functools.partial(_ffn_kernel, tm=tm, per=per, nb=nb),
        grid_spec=pltpu.PrefetchScalarGridSpec(
            num_scalar_prefetch=1,
            grid=(N_EXPERTS, per),
            in_specs=[
                pl.BlockSpec(memory_space=pl.ANY),
                pl.BlockSpec(memory_space=pl.ANY),
                pl.BlockSpec((1, D_MODEL, D_FF), lambda e, j, *_: (e, 0, 0)),
                pl.BlockSpec((1, D_MODEL, D_FF), lambda e, j, *_: (e, 0, 0)),
                pl.BlockSpec((1, D_FF, D_MODEL), lambda e, j, *_: (e, 0, 0)),
            ],
            out_specs=pl.BlockSpec(memory_space=pl.ANY),
            scratch_shapes=[
                pltpu.VMEM((2 * tm // SUBLANES, ROW_TILES, SUBLANES, LANES), _F32),
                pltpu.VMEM((2, tm, D_MODEL), _F32),
                pltpu.SMEM((2 * nb * TOK_BLOCK,), jnp.int32),
                pltpu.SMEM((2,), jnp.int32),
                pltpu.SemaphoreType.DMA((2,)),
                pltpu.SemaphoreType.DMA((2,)),
                pltpu.SemaphoreType.DMA((2,)),
            ],
        ),
        out_shape=jax.ShapeDtypeStruct((N_EXPERTS * cap, ROW_TILES, LANES), _F32),
        compiler_params=_params(("arbitrary", "arbitrary")),
        name="ffn",
    )(cnt_e, lidx_e, h1, wg, wu, wd)


SLOT_GROUPS = TOK_BLOCK // SUBLANES
COMBINE_ROWS = 32


def _combine_kernel(cnt_ref, off_ref, lidx_hbm, ye_hbm, h1_hbm, gsel_ref, lg_ref, lb_ref, y_ref,
                    slots_ref, hres_ref, lidx_smem, sem_idx, sem_rows, sem_res, *, cap, nb):
    b = pl.program_id(0)
    slot = b % 2

    per_block = N_EXPERTS * TOK_BLOCK

    def idx_copy(bn):
        return pltpu.make_async_copy(lidx_hbm.at[bn], lidx_smem.at[pl.ds((bn % 2) * per_block, per_block)],
                                     sem_idx.at[bn % 2])

    def res_copies(bn):
        return _matrix_copies(h1_hbm, bn * TOK_BLOCK, hres_ref.at[bn % 2], sem_res.at[bn % 2])

    def issue_block(bn):
        par = bn % 2
        sem = sem_rows.at[par]
        tok0 = bn * TOK_BLOCK
        for c in res_copies(bn):
            c.start()
        for e in range(N_EXPERTS):
            c = cnt_ref[bn * N_EXPERTS + e]
            base = e * cap + off_ref[bn * N_EXPERTS + e]
            list_minus_src = par * per_block + e * TOK_BLOCK - base
            group0 = (par * N_EXPERTS + e) * SLOT_GROUPS

            def one(src_row):
                t = lidx_smem[list_minus_src + src_row] - tok0
                pltpu.make_async_copy(ye_hbm.at[src_row], _row_view(slots_ref, t, group0), sem).start()

            _for_each(base, base + c, one)

    @pl.when(b == 0)
    def _():
        slots_ref[...] = jnp.zeros(slots_ref.shape, _F32)
        idx_copy(0).start()
        idx_copy(0).wait()
        issue_block(0)
        if nb > 1:
            idx_copy(1).start()

    @pl.when(b + 1 < nb)
    def _():
        idx_copy(b + 1).wait()
        issue_block(b + 1)

    @pl.when(b + 2 < nb)
    def _():
        idx_copy(b + 2).start()

    for e in range(N_EXPERTS):
        c = cnt_ref[b * N_EXPERTS + e]

        @pl.when(c > 0)
        def _():
            pltpu.make_async_copy(ye_hbm.at[pl.ds(0, c)], ye_hbm.at[pl.ds(0, c)], sem_rows.at[slot]).wait()

    for c in res_copies(b):
        c.wait()

    groups = COMBINE_ROWS // SUBLANES
    for tg in range(TOK_BLOCK // COMBINE_ROWS):
        rows = slice(tg * COMBINE_ROWS, (tg + 1) * COMBINE_ROWS)
        acc = [ALPHA * hres_ref[slot, rows, j * LANES:(j + 1) * LANES] for j in range(ROW_TILES)]
        for e in range(N_EXPERTS):
            gate = jnp.broadcast_to(gsel_ref[rows, e:e + 1], (COMBINE_ROWS, LANES))
            blk = slots_ref[pl.ds((slot * N_EXPERTS + e) * SLOT_GROUPS + tg * groups, groups)]
            for j in range(ROW_TILES):
                acc[j] = acc[j] + gate * blk[:, j].reshape(COMBINE_ROWS, LANES)
        y_ref[rows, :] = _layer_norm(jnp.concatenate(acc, axis=1), lg_ref[...], lb_ref[...])


def _combine(cnt_t, off_t, lidx, ye, gsel, h1, ln_g, ln_b, cap):
    T = h1.shape[0]
    nb = T // TOK_BLOCK
    return pl.pallas_call(
        functools.partial(_combine_kernel, cap=cap, nb=nb),
        grid_spec=pltpu.PrefetchScalarGridSpec(
            num_scalar_prefetch=2,
            grid=(nb,),
            in_specs=[
                pl.BlockSpec(memory_space=pl.ANY),
                pl.BlockSpec(memory_space=pl.ANY),
                pl.BlockSpec(memory_space=pl.ANY),
                pl.BlockSpec((TOK_BLOCK, LANES), lambda b, *_: (b, 0)),
                pl.BlockSpec((1, D_MODEL), lambda b, *_: (0, 0)),
                pl.BlockSpec((1, D_MODEL), lambda b, *_: (0, 0)),
            ],
            out_specs=pl.BlockSpec((TOK_BLOCK, D_MODEL), lambda b, *_: (b, 0)),
            scratch_shapes=[
                pltpu.VMEM((2 * N_EXPERTS * SLOT_GROUPS, ROW_TILES, SUBLANES, LANES), _F32),
                pltpu.VMEM((2, TOK_BLOCK, D_MODEL), _F32),
                pltpu.SMEM((2 * N_EXPERTS * TOK_BLOCK,), jnp.int32),
                pltpu.SemaphoreType.DMA((2,)),
                pltpu.SemaphoreType.DMA((2,)),
                pltpu.SemaphoreType.DMA((2,)),
            ],
        ),
        out_shape=jax.ShapeDtypeStruct((T, D_MODEL), _F32),
        compiler_params=_params(("arbitrary",)),
        name="combine",
    )(cnt_t, off_t, lidx, ye, h1, gsel, ln_g, ln_b)


def _tables(seq):
    half = HEAD_DIM // 2
    inv = 1.0 / (ROPE_BASE ** (jnp.arange(half, dtype=_F32) / half))
    ang = jnp.arange(seq, dtype=_F32)[:, None] * inv[None, :]
    cos = jnp.concatenate([jnp.cos(ang), jnp.cos(ang)], axis=1)
    sin = jnp.concatenate([-jnp.sin(ang), jnp.sin(ang)], axis=1)
    log_g = jnp.log1p(-jnp.exp2(-5.0 - jnp.arange(N_HEADS, dtype=_F32)))[:, None, None]
    pos = jnp.arange(CHUNK, dtype=_F32)
    rows = lambda f: jnp.broadcast_to(jnp.exp(log_g * f[None, :, None]), (N_HEADS, CHUNK, HEAD_DIM))
    dsym = jnp.exp(log_g * jnp.abs(pos[:, None] - pos[None, :])[None])
    tabs = dict(
        cos=cos, sin=sin, dsym=dsym,
        qwf=rows(pos + 1.0), kwf=rows(CHUNK - 1.0 - pos),
        qwb=rows(CHUNK - pos), kwb=rows(pos),
        gl=rows(jnp.full((CHUNK,), float(CHUNK), _F32)),
    )
    spread = lambda chans: (jnp.arange(LANES)[:, None] == jnp.repeat(jnp.asarray(chans), LANES)[None, :]).astype(_MXU)
    tabs["spread"] = spread(range(N_EXPERTS))
    tabs["spread_f"] = spread([0, 1, 2, 3, 4, 5, 6, 7, 16, 17, 18, 19])
    tabs["spread_b"] = spread([8, 9, 10, 11, 12, 13, 14, 15, 24, 25, 26, 27])
    return tabs


def _trunk(x, w):
    batch, seq, _ = x.shape
    T = batch * seq
    nb = T // TOK_BLOCK
    cap = CAP_FACTOR * T // N_EXPERTS
    t = _tables(seq)
    h0, P, G2, GC, GR = _inproj(x.reshape(T, D_MODEL), seq, w["ln_in_g"], w["ln_in_b"], w["w_main"],
                                w["b_main"], w["wg"], w["bg"], w["wgt"], w["bgt"], t["cos"], t["sin"])
    YB = _sweep_bwd(P, GC, GR, t["qwb"], t["kwb"], t["gl"], t["spread_b"], batch, seq)
    mixed = _sweep_fwd(P, G2, GC, GR, YB, w["ret_g"], w["mlstm_g"], t["dsym"], t["qwf"], t["kwf"], t["gl"],
                       t["spread_f"], batch, seq)
    h1, aff = _outproj(mixed, h0, w["w_o"], w["ln1_g"], w["ln1_b"], w["w_r"])
    affc = aff[:, :N_EXPERTS].reshape(T // SUBLANES, LANES)
    thr, rem = _thresh(affc, cap)
    gsel, lidx, cnt, off = _select(aff, thr, rem, t["spread"])
    cnt2 = cnt.reshape(nb, LANES)[:, :N_EXPERTS]
    off2 = off.reshape(nb, LANES)[:, :N_EXPERTS]
    lidx_e = lidx.reshape(nb, N_EXPERTS, TOK_BLOCK).transpose(1, 0, 2).reshape(N_EXPERTS, nb * TOK_BLOCK)
    ye = _ffn(cnt2.T.reshape(-1), lidx_e, h1, w["w_gate"], w["w_up"], w["w_down"], cap, min(256, cap))
    y = _combine(cnt2.reshape(-1), off2.reshape(-1), lidx.reshape(nb, N_EXPERTS * TOK_BLOCK), ye, gsel, h1,
                 w["ln2_g"], w["ln2_b"], cap)
    return y.reshape(batch, seq, D_MODEL)


def _prep_weights(ln_in_g, ln_in_b, w_in, b_in, ret_norm_g, mlstm_norm_g, w_o, ln1_g, ln1_b, w_router,
                  w_gate, w_up, w_down, ln2_g, ln2_b):
    main = 8 * SEC
    ngate = 4 * N_HEADS
    row = lambda v: v.reshape(1, -1).astype(_F32)
    wg = jnp.pad(w_in[0][:, main:main + ngate], ((0, 0), (0, LANES - ngate)))
    bg = jnp.pad(b_in[0][main:main + ngate], (0, LANES - ngate))
    return dict(
        ln_in_g=row(ln_in_g), ln_in_b=row(ln_in_b),
        w_main=w_in[0][:, :main].astype(_MXU), b_main=row(b_in[0][:main]),
        wg=wg.astype(_MXU), bg=row(bg), wgt=wg.T.astype(_MXU), bgt=bg.reshape(-1, 1).astype(_F32),
        ret_g=row(ret_norm_g[0]), mlstm_g=row(mlstm_norm_g[0]),
        w_o=w_o[0].astype(_MXU), ln1_g=row(ln1_g[0]), ln1_b=row(ln1_b[0]),
        w_r=jnp.pad(w_router[0], ((0, 0), (0, LANES - N_EXPERTS))).astype(_MXU),
        w_gate=w_gate[0].astype(_MXU), w_up=w_up[0].astype(_MXU), w_down=w_down[0].astype(_MXU),
        ln2_g=row(ln2_g[0]), ln2_b=row(ln2_b[0]),
    )


def kernel(x_prompt, x_sample, ln_in_g, ln_in_b, w_in, b_in, ret_norm_g, mlstm_norm_g, w_o, ln1_g, ln1_b,
           w_router, w_gate, w_up, w_down, ln2_g, ln2_b):
    w = _prep_weights(ln_in_g, ln_in_b, w_in, b_in, ret_norm_g, mlstm_norm_g, w_o, ln1_g, ln1_b, w_router,
                      w_gate, w_up, w_down, ln2_g, ln2_b)
    return (_trunk(x_prompt, w), _trunk(x_sample, w))
```

```python
import functools

import jax
import jax.numpy as jnp
from jax import lax
from jax.experimental import pallas as pl
from jax.experimental.pallas import tpu as pltpu

D_MODEL = 1024
N_HEADS = 4
HEAD_DIM = 128
SEC = N_HEADS * HEAD_DIM
CHUNK = 128
N_EXPERTS = 16
D_FF = 2 * D_MODEL
CAP_FACTOR = 2
ROPE_BASE = 10000.0
LN_EPS = 1e-5
NEG_BIG = -1e30
DEPTH = 1
ALPHA = (2.0 * DEPTH) ** 0.25
K_SCALE = HEAD_DIM ** -0.5
LANES = 128
SUBLANES = 8
TOK_BLOCK = 128
MIN_NORMAL_BITS = 0x00800000
VMEM_LIMIT = 56 * 1024 * 1024

_MXU = jnp.bfloat16
_F32 = jnp.float32


def _dot(a, b):
    return jnp.dot(a, b, preferred_element_type=_F32)


def _dot_nt(a, b):
    return lax.dot_general(a, b, (((1,), (1,)), ((), ())), preferred_element_type=_F32)


def _split3(x):
    x1 = x.astype(_MXU)
    r1 = x - x1.astype(_F32)
    x2 = r1.astype(_MXU)
    r2 = r1 - x2.astype(_F32)
    return x1, x2, r2.astype(_MXU)


def _dot01_left(a01, x):
    x1, x2, x3 = _split3(x)
    return _dot(a01, x1) + _dot(a01, x2) + _dot(a01, x3)


def _dot01_right(x, a01):
    x1, x2, x3 = _split3(x)
    return _dot(x1, a01) + _dot(x2, a01) + _dot(x3, a01)


def _layer_norm(x, g, b):
    mu = jnp.mean(x, axis=-1, keepdims=True)
    xc = x - mu
    var = jnp.mean(xc * xc, axis=-1, keepdims=True)
    return xc * lax.rsqrt(var + LN_EPS) * g + b


def _log_sigmoid(x):
    return jnp.minimum(x, 0.0) - jnp.log1p(jnp.exp(-jnp.abs(x)))


def _sigmoid(x):
    return 1.0 / (1.0 + jnp.exp(-x))


def _params(sem):
    return pltpu.CompilerParams(dimension_semantics=sem, vmem_limit_bytes=VMEM_LIMIT)


ROW_TILES = D_MODEL // LANES


ISSUE_UNROLL = 4


def _for_each(lo, hi, body):
    nblk = lax.shift_right_logical(hi - lo, ISSUE_UNROLL.bit_length() - 1)

    def block(k, carry):
        for u in range(ISSUE_UNROLL):
            body(lo + k * ISSUE_UNROLL + u)
        return carry

    def single(i, carry):
        body(i)
        return carry

    lax.fori_loop(0, nblk, block, 0)
    lax.fori_loop(lo + nblk * ISSUE_UNROLL, hi, single, 0)


def _row_view(buf, r, group0=0):
    return buf.at[group0 + lax.shift_right_logical(r, 3), :, jnp.bitwise_and(r, SUBLANES - 1), :]


def _matrix_value(buf):
    rows = buf.shape[0] * SUBLANES
    return jnp.concatenate([buf[:, j].reshape(rows, LANES) for j in range(ROW_TILES)], axis=1)


def _row_tile_copies(mat, hbm, row0, sem):
    rows = mat.shape[0]
    return [pltpu.make_async_copy(mat.at[:, pl.ds(j * LANES, LANES)], hbm.at[pl.ds(row0, rows), j, :], sem)
            for j in range(ROW_TILES)]


def _matrix_copies(hbm, row0, mat, sem):
    rows = mat.shape[0]
    return [pltpu.make_async_copy(hbm.at[pl.ds(row0, rows), j, :], mat.at[:, pl.ds(j * LANES, LANES)], sem)
            for j in range(ROW_TILES)]


def _pipelined_writeback(buf, sem, hbm, value, step, nsteps, rows):
    slot = step % 2

    @pl.when(step >= 2)
    def _():
        for c in _row_tile_copies(buf.at[slot], hbm, (step - 2) * rows, sem.at[slot]):
            c.wait()

    buf[slot] = value
    for c in _row_tile_copies(buf.at[slot], hbm, step * rows, sem.at[slot]):
        c.start()

    @pl.when(step == nsteps - 1)
    def _():
        if nsteps > 1:
            for c in _row_tile_copies(buf.at[1 - slot], hbm, (step - 1) * rows, sem.at[1 - slot]):
                c.wait()
        for c in _row_tile_copies(buf.at[slot], hbm, step * rows, sem.at[slot]):
            c.wait()


def _running_max(x, reverse):
    n = x.shape[0]
    row = lax.broadcasted_iota(jnp.int32, x.shape, 0)
    step = 1
    while step < n:
        if reverse:
            shifted = jnp.where(row < n - step, pltpu.roll(x, n - step, 0), NEG_BIG)
        else:
            shifted = jnp.where(row >= step, pltpu.roll(x, step, 0), NEG_BIG)
        x = jnp.maximum(x, shifted)
        step *= 2
    return x


_P_COL = {0: 0, 1: 1, 2: 2, 4: 3, 5: 4, 6: 5}
_G2_COL = {3: 0, 7: 1}


def _inproj_kernel(x_ref, lg_ref, lb_ref, w_ref, b_ref, wg_ref, bg_ref,
                   cos_ref, sin_ref, h0_ref, p_ref, g2_ref, gc_ref, gr_ref):
    tm = x_ref.shape[0]
    h = _layer_norm(x_ref[...], lg_ref[...], lb_ref[...])
    h0_ref[...] = h
    hb = h.astype(_MXU)
    cos = cos_ref[...]
    sin = sin_ref[...]
    for sec in range(8):
        acc = _dot(hb, w_ref[:, sec * SEC:(sec + 1) * SEC]) + b_ref[:, sec * SEC:(sec + 1) * SEC]
        if sec in (0, 1):
            c0 = _P_COL[sec] * SEC
            for hh in range(N_HEADS):
                s = acc[:, hh * HEAD_DIM:(hh + 1) * HEAD_DIM]
                r = s * cos + pltpu.roll(s, HEAD_DIM // 2, 1) * sin
                if sec == 1:
                    r = r * K_SCALE
                p_ref[:, c0 + hh * HEAD_DIM:c0 + (hh + 1) * HEAD_DIM] = r.astype(p_ref.dtype)
        elif sec in _P_COL:
            if sec == 5:
                acc = acc * K_SCALE
            c0 = _P_COL[sec] * SEC
            p_ref[:, c0:c0 + SEC] = acc.astype(p_ref.dtype)
        else:
            c0 = _G2_COL[sec] * SEC
            g2_ref[:, c0:c0 + SEC] = acc

    pre = _dot(hb, wg_ref[...]) + bg_ref[...]
    row = lax.broadcasted_iota(jnp.int32, (CHUNK, CHUNK), 0)
    col = lax.broadcasted_iota(jnp.int32, (CHUNK, CHUNK), 1)
    tri_le = (col <= row).astype(_MXU)
    tri_ge = (col >= row).astype(_MXU)
    for c in range(tm // CHUNK):
        sl = slice(c * CHUNK, (c + 1) * CHUNK)
        blk = pre[sl, :]
        ls = _log_sigmoid(blk)
        pref = _dot01_left(tri_le, ls)
        suf = _dot01_left(tri_ge, ls)
        is_cum_f = (col >= 4) & (col < 8)
        is_cum_b = (col >= 12) & (col < 16)
        cum = jnp.where(is_cum_f, pref, jnp.where(is_cum_b, suf, 0.0))
        excess = blk - pltpu.roll(cum, LANES - 4, 1)
        run_max = jnp.where(col < 4, _running_max(excess, False), _running_max(excess, True))
        is_max = ((col >= 16) & (col < 20)) | ((col >= 24) & (col < 28))
        gates = jnp.where(is_cum_f | is_cum_b, cum, jnp.where(is_max, pltpu.roll(run_max, 16, 1), blk))
        gc_ref[sl, :] = gates
        gr_ref[:, sl] = gates.T


def _inproj(x2, seq, ln_g, ln_b, w_main, b_main, wg, bg, cos, sin, tm=512):
    T = x2.shape[0]
    nseq = seq // tm
    const = lambda i: (0, 0)
    return pl.pallas_call(
        _inproj_kernel,
        grid=(T // tm,),
        in_specs=[
            pl.BlockSpec((tm, D_MODEL), lambda i: (i, 0)),
            pl.BlockSpec((1, D_MODEL), const),
            pl.BlockSpec((1, D_MODEL), const),
            pl.BlockSpec((D_MODEL, 8 * SEC), const),
            pl.BlockSpec((1, 8 * SEC), const),
            pl.BlockSpec((D_MODEL, LANES), const),
            pl.BlockSpec((1, LANES), const),
            pl.BlockSpec((tm, HEAD_DIM), lambda i: (i % nseq, 0)),
            pl.BlockSpec((tm, HEAD_DIM), lambda i: (i % nseq, 0)),
        ],
        out_specs=[
            pl.BlockSpec((tm, D_MODEL), lambda i: (i, 0)),
            pl.BlockSpec((tm, 6 * SEC), lambda i: (i, 0)),
            pl.BlockSpec((tm, 2 * SEC), lambda i: (i, 0)),
            pl.BlockSpec((tm, LANES), lambda i: (i, 0)),
            pl.BlockSpec((LANES, tm), lambda i: (0, i)),
        ],
        out_shape=[
            jax.ShapeDtypeStruct((T, D_MODEL), _F32),
            jax.ShapeDtypeStruct((T, 6 * SEC), _MXU),
            jax.ShapeDtypeStruct((T, 2 * SEC), _F32),
            jax.ShapeDtypeStruct((T, LANES), _F32),
            jax.ShapeDtypeStruct((LANES, T), _F32),
        ],
        compiler_params=_params(("parallel",)),
        name="inproj",
    )(x2, ln_g, ln_b, w_main, b_main, wg, bg, cos, sin)


def _init_state(s_ref, cn_ref, m_ref):
    s_ref[...] = jnp.zeros(s_ref.shape, _F32)
    cn_ref[...] = jnp.zeros(cn_ref.shape, _F32)
    m_ref[...] = jnp.full(m_ref.shape, NEG_BIG, _F32)


def _dot_tn(a, b):
    return lax.dot_general(a, b, (((0,), (0,)), ((), ())), preferred_element_type=_F32)


def _lane_spread(gc, spread_ref):
    return _dot01_right(gc, spread_ref[...])


def _mlstm_direction(q, k, v, li_b, cum_b, max_b, li_row, cum_row, last_lane, mask, cn_ref, m_ref, si):
    cum_last = cum_row[:, last_lane:last_lane + 1]
    m_prev = m_ref[si, 0:1, :]
    cn_prev = cn_ref[si]
    ones = jnp.ones((CHUNK, HEAD_DIM), _MXU)
    v1 = jnp.concatenate([v, ones], axis=1)

    m_row = cum_b + jnp.maximum(max_b, m_prev)
    log_d = jnp.where(mask, cum_b - cum_row + li_row, NEG_BIG)
    d_w = jnp.exp(log_d - m_row)
    s_inter = jnp.exp(cum_b + m_prev - m_row)
    qk = _dot_nt(q, k) * d_w
    qk_hi = qk.astype(_MXU)
    qk_lo = (qk - qk_hi.astype(_F32)).astype(_MXU)
    intra = _dot(qk_hi, v1)
    inter = _dot(q, cn_prev.astype(_MXU))
    num = intra[:, :HEAD_DIM] + s_inter * inter[:, :HEAD_DIM]
    den = intra[:, HEAD_DIM:] + _dot(qk_lo, ones) + s_inter * inter[:, HEAD_DIM:]
    h_out = num / jnp.maximum(jnp.abs(den), jnp.exp(-m_row))

    a_max = jnp.max(cum_last - cum_row + li_row, axis=1, keepdims=True)
    kw = (k.astype(_F32) * jnp.exp(cum_last - cum_b + li_b - a_max)).astype(_MXU)
    m_new = jnp.maximum(cum_last + m_prev, a_max)
    s_old = jnp.exp(cum_last + m_prev - m_new)
    s_new = jnp.exp(a_max - m_new)
    cn_ref[si] = (jnp.concatenate([s_old, s_old], axis=1) * cn_prev
                  + jnp.concatenate([s_new, s_new], axis=1) * _dot_tn(kw, v1))
    m_ref[si] = jnp.broadcast_to(m_new, (SUBLANES, LANES))
    return h_out


def _retention_state_update(k, v, kw, gl, s_ref, si):
    kwv = (k.astype(_F32) * kw).astype(_MXU)
    s_ref[si] = gl * s_ref[si] + _dot_tn(kwv, v)


SEQ_PER_STEP = 4


def _head_cols(h, base=0):
    return slice(base + h * HEAD_DIM, base + (h + 1) * HEAD_DIM)


def _sweep_bwd_kernel(*refs):
    seq_refs = [refs[8 * k:8 * (k + 1)] for k in range(SEQ_PER_STEP)]
    qwb_ref, kwb_ref, gl_ref, spread_ref, yb_ref, s_ref, cn_ref, m_ref = refs[8 * SEQ_PER_STEP:]

    @pl.when(pl.program_id(1) == 0)
    def _():
        _init_state(s_ref, cn_ref, m_ref)

    row = lax.broadcasted_iota(jnp.int32, (CHUNK, CHUNK), 0)
    col = lax.broadcasted_iota(jnp.int32, (CHUNK, CHUNK), 1)
    mask = col >= row
    for h in range(N_HEADS):
        sl = _head_cols(h)
        for k in range(SEQ_PER_STEP):
            rq_ref, rk_ref, rv_ref = seq_refs[k][0:3]
            si = k * N_HEADS + h
            qs = (rq_ref[:, sl].astype(_F32) * qwb_ref[h]).astype(_MXU)
            yb_ref[k, :, sl] = _dot(qs, s_ref[si].astype(_MXU))
            _retention_state_update(rk_ref[:, sl], rv_ref[:, sl], kwb_ref[h], gl_ref[h], s_ref, si)
    cols = [_lane_spread(seq_refs[k][6][...], spread_ref) for k in range(SEQ_PER_STEP)]
    for h in range(N_HEADS):
        sl = _head_cols(h)
        for k in range(SEQ_PER_STEP):
            mq_ref, mk_ref, mv_ref, _, gr_ref = seq_refs[k][3:8]
            h_b = _mlstm_direction(mq_ref[:, sl], mk_ref[:, sl], mv_ref[:, sl],
                                   cols[k][:, _head_cols(h)], cols[k][:, _head_cols(h, SEC)],
                                   cols[k][:, _head_cols(h, 2 * SEC)],
                                   gr_ref[8 + h:9 + h, :], gr_ref[12 + h:13 + h, :], 0, mask,
                                   cn_ref, m_ref, k * N_HEADS + h)
            yb_ref[k, :, _head_cols(h, SEC)] = h_b


def _sweep_fwd_kernel(*refs):
    seq_refs = [refs[10 * k:10 * (k + 1)] for k in range(SEQ_PER_STEP)]
    (rng_ref, mng_ref, dsym_ref, qwf_ref, kwf_ref, gl_ref, spread_ref,
     mixed_ref, s_ref, cn_ref, m_ref) = refs[10 * SEQ_PER_STEP:]

    @pl.when(pl.program_id(1) == 0)
    def _():
        _init_state(s_ref, cn_ref, m_ref)

    row = lax.broadcasted_iota(jnp.int32, (CHUNK, CHUNK), 0)
    col = lax.broadcasted_iota(jnp.int32, (CHUNK, CHUNK), 1)
    mask = col <= row

    pairs = [(h, k) for h in range(N_HEADS) for k in range(SEQ_PER_STEP)]

    def head_norms(ys):
        centred = [y - m for y, m in zip(ys, [jnp.mean(y, axis=1, keepdims=True) for y in ys])]
        var = [jnp.mean(c * c, axis=1, keepdims=True) for c in centred]
        return [c * lax.rsqrt(v + LN_EPS) for c, v in zip(centred, var)]

    qkv = [[seq_refs[k][i][:, _head_cols(h)] for i in range(3)] for h, k in pairs]
    scores = [_dot_nt(q, kk) for q, kk, _ in qkv]
    inter = [_dot((q.astype(_F32) * qwf_ref[h]).astype(_MXU), s_ref[k * N_HEADS + h].astype(_MXU))
             for (h, k), (q, _, _) in zip(pairs, qkv)]
    ys = [_dot((s * dsym_ref[h]).astype(_MXU), v) + it + seq_refs[k][8][:, _head_cols(h)]
          for (h, k), (_, _, v), s, it in zip(pairs, qkv, scores, inter)]
    for (h, k), (_, kk, v) in zip(pairs, qkv):
        _retention_state_update(kk, v, kwf_ref[h], gl_ref[h], s_ref, k * N_HEADS + h)
    for (h, k), yn in zip(pairs, head_norms(ys)):
        sl = _head_cols(h)
        g = seq_refs[k][9][:, sl]
        mixed_ref[k, :, sl] = (yn * rng_ref[:, sl] * (g * _sigmoid(g))).astype(mixed_ref.dtype)

    cols = [_lane_spread(seq_refs[k][6][...], spread_ref) for k in range(SEQ_PER_STEP)]
    ys = []
    for h, k in pairs:
        sl = _head_cols(h)
        mq_ref, mk_ref, mv_ref, _, gr_ref, yb_ref = seq_refs[k][3:9]
        h_f = _mlstm_direction(mq_ref[:, sl], mk_ref[:, sl], mv_ref[:, sl],
                               cols[k][:, _head_cols(h)], cols[k][:, _head_cols(h, SEC)],
                               cols[k][:, _head_cols(h, 2 * SEC)],
                               gr_ref[h:h + 1, :], gr_ref[4 + h:5 + h, :], CHUNK - 1, mask,
                               cn_ref, m_ref, k * N_HEADS + h)
        ys.append(h_f + yb_ref[:, _head_cols(h, SEC)])
    for (h, k), yn in zip(pairs, head_norms(ys)):
        sl2 = _head_cols(h, SEC)
        out = yn * mng_ref[:, _head_cols(h)] * _sigmoid(seq_refs[k][9][:, sl2])
        mixed_ref[k, :, sl2] = out.astype(mixed_ref.dtype)


def _state_scratch():
    return [
        pltpu.VMEM((SEQ_PER_STEP * N_HEADS, HEAD_DIM, HEAD_DIM), _F32),
        pltpu.VMEM((SEQ_PER_STEP * N_HEADS, HEAD_DIM, 2 * HEAD_DIM), _F32),
        pltpu.VMEM((SEQ_PER_STEP * N_HEADS, SUBLANES, LANES), _F32),
    ]


_SPREAD_SPEC = pl.BlockSpec((LANES, 3 * SEC), lambda b, n: (0, 0))


def _sweep_specs(nchunk, reverse, n_wide):
    def chunk(n):
        return (nchunk - 1 - n) if reverse else n

    per_seq = []
    for k in range(SEQ_PER_STEP):
        def rb(b, n, k=k):
            return (b * SEQ_PER_STEP + k) * nchunk + chunk(n)
        per_seq += [pl.BlockSpec((CHUNK, SEC), functools.partial(lambda b, n, s, rb: (rb(b, n), s), s=s, rb=rb))
                    for s in range(6)]
        per_seq.append(pl.BlockSpec((CHUNK, LANES), functools.partial(lambda b, n, rb: (rb(b, n), 0), rb=rb)))
        per_seq.append(pl.BlockSpec((LANES, CHUNK), functools.partial(lambda b, n, rb: (0, rb(b, n)), rb=rb)))
        per_seq += [pl.BlockSpec((CHUNK, 2 * SEC), functools.partial(lambda b, n, rb: (rb(b, n), 0), rb=rb))
                    for _ in range(n_wide)]
    out = pl.BlockSpec((SEQ_PER_STEP, CHUNK, 2 * SEC), lambda b, n: (b, chunk(n), 0))
    tab = pl.BlockSpec((N_HEADS, CHUNK, HEAD_DIM), lambda b, n: (0, 0, 0))
    return per_seq, out, tab


def _sweep_bwd(P, GC, GR, qwb, kwb, gl, spread, batch, seq):
    nchunk = seq // CHUNK
    per_seq, out, tab = _sweep_specs(nchunk, True, 0)
    yb = pl.pallas_call(
        _sweep_bwd_kernel,
        grid=(batch // SEQ_PER_STEP, nchunk),
        in_specs=per_seq + [tab, tab, tab, _SPREAD_SPEC],
        out_specs=out,
        out_shape=jax.ShapeDtypeStruct((batch, seq, 2 * SEC), _F32),
        scratch_shapes=_state_scratch(),
        compiler_params=_params(("parallel", "arbitrary")),
        name="sweep_bwd",
    )(*([P] * 6 + [GC, GR]) * SEQ_PER_STEP, qwb, kwb, gl, spread)
    return yb.reshape(batch * seq, 2 * SEC)


def _sweep_fwd(P, G2, GC, GR, YB, rng, mng, dsym, qwf, kwf, gl, spread, batch, seq):
    nchunk = seq // CHUNK
    per_seq, out, tab = _sweep_specs(nchunk, False, 2)
    gain = pl.BlockSpec((1, SEC), lambda b, n: (0, 0))
    mixed = pl.pallas_call(
        _sweep_fwd_kernel,
        grid=(batch // SEQ_PER_STEP, nchunk),
        in_specs=per_seq + [gain, gain, tab, tab, tab, tab, _SPREAD_SPEC],
        out_specs=out,
        out_shape=jax.ShapeDtypeStruct((batch, seq, 2 * SEC), _MXU),
        scratch_shapes=_state_scratch(),
        compiler_params=_params(("parallel", "arbitrary")),
        name="sweep_fwd",
    )(*([P] * 6 + [GC, GR, YB, G2]) * SEQ_PER_STEP, rng, mng, dsym, qwf, kwf, gl, spread)
    return mixed.reshape(batch * seq, 2 * SEC)


def _outproj_kernel(mixed_ref, h0_ref, wo_ref, lg_ref, lb_ref, wr_ref, h1_hbm, aff_ref, hbuf, sem, *, nsteps):
    z = ALPHA * h0_ref[...] + _dot(mixed_ref[...], wo_ref[...])
    h1 = _layer_norm(z, lg_ref[...], lb_ref[...])
    _pipelined_writeback(hbuf, sem, h1_hbm, h1, pl.program_id(0), nsteps, h1.shape[0])
    logits = _dot(h1.astype(_MXU), wr_ref[...])
    lane = lax.broadcasted_iota(jnp.int32, logits.shape, 1)
    valid = lane < N_EXPERTS
    logits = jnp.where(valid, logits, NEG_BIG)
    e = jnp.exp(logits - jnp.max(logits, axis=1, keepdims=True))
    aff = e / jnp.sum(e, axis=1, keepdims=True)
    aff_ref[...] = jnp.where(valid, aff, 0.0)


def _outproj(mixed, h0, wo, ln_g, ln_b, wr, tm=512):
    T = mixed.shape[0]
    const = lambda i: (0, 0)
    return pl.pallas_call(
        functools.partial(_outproj_kernel, nsteps=T // tm),
        grid=(T // tm,),
        in_specs=[
            pl.BlockSpec((tm, D_MODEL), lambda i: (i, 0)),
            pl.BlockSpec((tm, D_MODEL), lambda i: (i, 0)),
            pl.BlockSpec((D_MODEL, D_MODEL), const),
            pl.BlockSpec((1, D_MODEL), const),
            pl.BlockSpec((1, D_MODEL), const),
            pl.BlockSpec((D_MODEL, LANES), const),
        ],
        out_specs=[
            pl.BlockSpec(memory_space=pl.ANY),
            pl.BlockSpec((tm, LANES), lambda i: (i, 0)),
        ],
        out_shape=[
            jax.ShapeDtypeStruct((T, ROW_TILES, LANES), _F32),
            jax.ShapeDtypeStruct((T, LANES), _F32),
        ],
        scratch_shapes=[pltpu.VMEM((2, tm, D_MODEL), _F32), pltpu.SemaphoreType.DMA((2,))],
        compiler_params=_params(("arbitrary",)),
        name="outproj",
    )(mixed, h0, wo, ln_g, ln_b, wr)


def _thresh_kernel(aff_ref, thr_ref, rem_ref, *, cap):
    rows = aff_ref.shape[0]
    aff = aff_ref[...]

    def count(pred):
        c = jnp.sum(pred.astype(jnp.int32).reshape(rows // SUBLANES, SUBLANES, LANES), axis=0)
        c = jnp.broadcast_to(jnp.sum(c, axis=0, keepdims=True), (SUBLANES, LANES))
        for shift in (64, 32, 16):
            c = c + pltpu.roll(c, shift, 1)
        return c

    def body(i, ans):
        cand = ans | jnp.left_shift(jnp.int32(1), 30 - i)
        c = count(aff >= lax.bitcast_convert_type(cand[0:1, :], _F32))
        return jnp.where(c >= cap, cand, ans)

    ans = lax.fori_loop(0, 31, body, jnp.zeros((SUBLANES, LANES), jnp.int32))
    thr = jnp.where(ans >= MIN_NORMAL_BITS, lax.bitcast_convert_type(ans, _F32), 0.0)
    thr_ref[...] = thr
    rem_ref[...] = cap - count(aff > thr[0:1, :])


def _thresh(affc, cap):
    return pl.pallas_call(
        functools.partial(_thresh_kernel, cap=cap),
        out_shape=[jax.ShapeDtypeStruct((SUBLANES, LANES), _F32),
                   jax.ShapeDtypeStruct((SUBLANES, LANES), jnp.int32)],
        compiler_params=pltpu.CompilerParams(vmem_limit_bytes=VMEM_LIMIT),
        name="thresh",
    )(affc)


SELECT_BLOCKS = 4
SELECT_ROWS = SELECT_BLOCKS * TOK_BLOCK


def _select_kernel(aff_ref, thr_ref, rem_ref, spread_ref, before_ref, gsel_ref, lidx_ref, cnt_ref, off_ref,
                   nsel_ref, neq_ref):
    @pl.when(pl.program_id(0) == 0)
    def _():
        nsel_ref[...] = jnp.zeros(nsel_ref.shape, _F32)
        neq_ref[...] = jnp.zeros(neq_ref.shape, _F32)

    aff = aff_ref[...]
    thr = thr_ref[0:1, :]
    rem = rem_ref[0:1, :].astype(_F32)
    valid = lax.broadcasted_iota(jnp.int32, aff.shape, 1) < N_EXPERTS
    before = before_ref[...]
    gt = (aff > thr) & valid
    eq = (aff == thr) & valid
    eq_before = _dot(before, eq.astype(_MXU)) + neq_ref[0:1, :]
    sel = gt | (eq & (eq_before < rem))
    pos = _dot(before, sel.astype(_MXU))
    neq_ref[0:1, :] = neq_ref[0:1, :] + jnp.sum(eq.astype(_F32), axis=0, keepdims=True)
    gsel_ref[...] = jnp.where(sel, aff, 0.0)

    slot = (lax.broadcasted_iota(jnp.int32, (TOK_BLOCK, N_EXPERTS * TOK_BLOCK), 1) % TOK_BLOCK).astype(_F32)
    tok = lax.broadcasted_iota(jnp.int32, (TOK_BLOCK, N_EXPERTS * TOK_BLOCK), 0)
    start = nsel_ref[0:1, :]
    taken = start
    for q in range(SELECT_BLOCKS):
        rows = slice(q * TOK_BLOCK, (q + 1) * TOK_BLOCK)
        sel_q = sel[rows, :]
        off_ref[q] = taken.astype(jnp.int32)
        cnt = jnp.sum(sel_q.astype(_F32), axis=0, keepdims=True)
        cnt_ref[q] = cnt.astype(jnp.int32)
        ranked = jnp.where(sel_q, pos[rows, :] - (taken - start), -1.0).astype(_MXU)
        spread = _dot(ranked, spread_ref[...])
        tok0 = (pl.program_id(0) * SELECT_BLOCKS + q) * TOK_BLOCK
        lidx_ref[q] = jnp.sum(jnp.where(spread == slot, tok + tok0, 0), axis=0, keepdims=True)
        taken = taken + cnt
    nsel_ref[0:1, :] = taken


def _select(aff, thr, rem, spread, before):
    T = aff.shape[0]
    nb = T // TOK_BLOCK
    const = lambda b: (0, 0)
    return pl.pallas_call(
        _select_kernel,
        grid=(nb // SELECT_BLOCKS,),
        in_specs=[
            pl.BlockSpec((SELECT_ROWS, LANES), lambda b: (b, 0)),
            pl.BlockSpec((SUBLANES, LANES), const),
            pl.BlockSpec((SUBLANES, LANES), const),
            pl.BlockSpec((LANES, N_EXPERTS * TOK_BLOCK), const),
            pl.BlockSpec((SELECT_ROWS, SELECT_ROWS), const),
        ],
        out_specs=[
            pl.BlockSpec((SELECT_ROWS, LANES), lambda b: (b, 0)),
            pl.BlockSpec((SELECT_BLOCKS, 1, N_EXPERTS * TOK_BLOCK), lambda b: (b, 0, 0)),
            pl.BlockSpec((SELECT_BLOCKS, 1, LANES), lambda b: (b, 0, 0)),
            pl.BlockSpec((SELECT_BLOCKS, 1, LANES), lambda b: (b, 0, 0)),
        ],
        out_shape=[
            jax.ShapeDtypeStruct((T, LANES), _F32),
            jax.ShapeDtypeStruct((nb, 1, N_EXPERTS * TOK_BLOCK), jnp.int32),
            jax.ShapeDtypeStruct((nb, 1, LANES), jnp.int32),
            jax.ShapeDtypeStruct((nb, 1, LANES), jnp.int32),
        ],
        scratch_shapes=[pltpu.VMEM((SUBLANES, LANES), _F32), pltpu.VMEM((SUBLANES, LANES), _F32)],
        compiler_params=_params(("arbitrary",)),
        name="select",
    )(aff, thr, rem, spread, before)


def _ffn_kernel(cnt_ref, lidx_hbm, h1_hbm, wg_ref, wu_ref, wd_ref, y_hbm,
                xbuf, ybuf, lidx_smem, walk_ref, sem_idx, sem_rows, sem_out, *, tm, per, nb):
    e = pl.program_id(0)
    j = pl.program_id(1)
    step = e * per + j
    slot = step % 2

    per_expert = nb * TOK_BLOCK

    def idx_copy(en):
        return pltpu.make_async_copy(lidx_hbm.at[en], lidx_smem.at[pl.ds((en % 2) * per_expert, per_expert)],
                                     sem_idx.at[en % 2])

    def issue_tile(en, dst_slot):
        list_base = (en % 2) * per_expert
        group0 = dst_slot * (tm // SUBLANES)
        sem = sem_rows.at[dst_slot]

        def cond(st):
            return st[0] < tm

        def body(st):
            n, b, r = st
            c = cnt_ref[en * nb + b]
            take = jnp.minimum(c - r, tm - n)
            src_minus_dst = list_base + b * TOK_BLOCK + r - n

            def one(m):
                t = lidx_smem[src_minus_dst + m]
                pltpu.make_async_copy(h1_hbm.at[t], _row_view(xbuf, m, group0), sem).start()

            _for_each(n, n + take, one)
            done = r + take >= c
            return n + take, jnp.where(done, b + 1, b), jnp.where(done, 0, r + take)

        _, b, r = lax.while_loop(cond, body, (jnp.int32(0), walk_ref[0], walk_ref[1]))
        walk_ref[0] = b
        walk_ref[1] = r

    def restart_walk():
        walk_ref[0] = jnp.int32(0)
        walk_ref[1] = jnp.int32(0)

    @pl.when(step == 0)
    def _():
        idx_copy(0).start()
        idx_copy(0).wait()
        restart_walk()
        issue_tile(0, 0)

    @pl.when((j == 0) & (e + 1 < N_EXPERTS))
    def _():
        idx_copy(e + 1).start()

    @pl.when(j + 1 < per)
    def _():
        issue_tile(e, 1 - slot)

    @pl.when((j + 1 == per) & (e + 1 < N_EXPERTS))
    def _():
        idx_copy(e + 1).wait()
        restart_walk()
        issue_tile(e + 1, 1 - slot)

    pltpu.make_async_copy(h1_hbm.at[pl.ds(0, tm)], h1_hbm.at[pl.ds(0, tm)], sem_rows.at[slot]).wait()
    x = _matrix_value(xbuf.at[pl.ds(slot * (tm // SUBLANES), tm // SUBLANES)]).astype(_MXU)
    g = _dot(x, wg_ref[0])
    u = _dot(x, wu_ref[0])
    hid = (g * _sigmoid(g) * u).astype(_MXU)
    _pipelined_writeback(ybuf, sem_out, y_hbm, _dot(hid, wd_ref[0]), step, N_EXPERTS * per, tm)


def _ffn(cnt_e, lidx_e, h1, wg, wu, wd, cap, tm):
    per = cap // tm
    nb = lidx_e.shape[1] // TOK_BLOCK
    return pl.pallas_call(
        functools.partial(_ffn_kernel, tm=tm, per=per, nb=nb),
        grid_spec=pltpu.PrefetchScalarGridSpec(
            num_scalar_prefetch=1,
            grid=(N_EXPERTS, per),
            in_specs=[
                pl.BlockSpec(memory_space=pl.ANY),
                pl.BlockSpec(memory_space=pl.ANY),
                pl.BlockSpec((1, D_MODEL, D_FF), lambda e, j, *_: (e, 0, 0)),
                pl.BlockSpec((1, D_MODEL, D_FF), lambda e, j, *_: (e, 0, 0)),
                pl.BlockSpec((1, D_FF, D_MODEL), lambda e, j, *_: (e, 0, 0)),
            ],
            out_specs=pl.BlockSpec(memory_space=pl.ANY),
            scratch_shapes=[
                pltpu.VMEM((2 * tm // SUBLANES, ROW_TILES, SUBLANES, LANES), _F32),
                pltpu.VMEM((2, tm, D_MODEL), _F32),
                pltpu.SMEM((2 * nb * TOK_BLOCK,), jnp.int32),
                pltpu.SMEM((2,), jnp.int32),
                pltpu.SemaphoreType.DMA((2,)),
                pltpu.SemaphoreType.DMA((2,)),
                pltpu.SemaphoreType.DMA((2,)),
            ],
        ),
        out_shape=jax.ShapeDtypeStruct((N_EXPERTS * cap, ROW_TILES, LANES), _F32),
        compiler_params=_params(("arbitrary", "arbitrary")),
        name="ffn",
    )(cnt_e, lidx_e, h1, wg, wu, wd)


SLOT_GROUPS = TOK_BLOCK // SUBLANES
COMBINE_ROWS = 32


def _combine_kernel(cnt_ref, off_ref, lidx_hbm, ye_hbm, h1_hbm, gsel_ref, lg_ref, lb_ref, y_ref,
                    slots_ref, hres_ref, lidx_smem, sem_idx, sem_rows, sem_res, *, cap, nb):
    b = pl.program_id(0)
    slot = b % 2

    per_block = N_EXPERTS * TOK_BLOCK

    def idx_copy(bn):
        return pltpu.make_async_copy(lidx_hbm.at[bn], lidx_smem.at[pl.ds((bn % 2) * per_block, per_block)],
                                     sem_idx.at[bn % 2])

    def res_copies(bn):
        return _matrix_copies(h1_hbm, bn * TOK_BLOCK, hres_ref.at[bn % 2], sem_res.at[bn % 2])

    def issue_block(bn):
        par = bn % 2
        sem = sem_rows.at[par]
        tok0 = bn * TOK_BLOCK
        for c in res_copies(bn):
            c.start()
        for e in range(N_EXPERTS):
            c = cnt_ref[bn * N_EXPERTS + e]
            base = e * cap + off_ref[bn * N_EXPERTS + e]
            list_minus_src = par * per_block + e * TOK_BLOCK - base
            group0 = (par * N_EXPERTS + e) * SLOT_GROUPS

            def one(src_row):
                t = lidx_smem[list_minus_src + src_row] - tok0
                pltpu.make_async_copy(ye_hbm.at[src_row], _row_view(slots_ref, t, group0), sem).start()

            _for_each(base, base + c, one)

    @pl.when(b == 0)
    def _():
        slots_ref[...] = jnp.zeros(slots_ref.shape, _F32)
        idx_copy(0).start()
        idx_copy(0).wait()
        issue_block(0)
        if nb > 1:
            idx_copy(1).start()

    @pl.when(b + 1 < nb)
    def _():
        idx_copy(b + 1).wait()
        issue_block(b + 1)

    @pl.when(b + 2 < nb)
    def _():
        idx_copy(b + 2).start()

    for e in range(N_EXPERTS):
        c = cnt_ref[b * N_EXPERTS + e]

        @pl.when(c > 0)
        def _():
            pltpu.make_async_copy(ye_hbm.at[pl.ds(0, c)], ye_hbm.at[pl.ds(0, c)], sem_rows.at[slot]).wait()

    for c in res_copies(b):
        c.wait()

    groups = COMBINE_ROWS // SUBLANES
    for tg in range(TOK_BLOCK // COMBINE_ROWS):
        rows = slice(tg * COMBINE_ROWS, (tg + 1) * COMBINE_ROWS)
        acc = [ALPHA * hres_ref[slot, rows, j * LANES:(j + 1) * LANES] for j in range(ROW_TILES)]
        for e in range(N_EXPERTS):
            gate = jnp.broadcast_to(gsel_ref[rows, e:e + 1], (COMBINE_ROWS, LANES))
            blk = slots_ref[pl.ds((slot * N_EXPERTS + e) * SLOT_GROUPS + tg * groups, groups)]
            for j in range(ROW_TILES):
                acc[j] = acc[j] + gate * blk[:, j].reshape(COMBINE_ROWS, LANES)
        for j in range(ROW_TILES):
            y_ref[rows, j * LANES:(j + 1) * LANES] = acc[j]
    y_ref[...] = _layer_norm(y_ref[...], lg_ref[...], lb_ref[...])


def _combine(cnt_t, off_t, lidx, ye, gsel, h1, ln_g, ln_b, cap):
    T = h1.shape[0]
    nb = T // TOK_BLOCK
    return pl.pallas_call(
        functools.partial(_combine_kernel, cap=cap, nb=nb),
        grid_spec=pltpu.PrefetchScalarGridSpec(
            num_scalar_prefetch=2,
            grid=(nb,),
            in_specs=[
                pl.BlockSpec(memory_space=pl.ANY),
                pl.BlockSpec(memory_space=pl.ANY),
                pl.BlockSpec(memory_space=pl.ANY),
                pl.BlockSpec((TOK_BLOCK, LANES), lambda b, *_: (b, 0)),
                pl.BlockSpec((1, D_MODEL), lambda b, *_: (0, 0)),
                pl.BlockSpec((1, D_MODEL), lambda b, *_: (0, 0)),
            ],
            out_specs=pl.BlockSpec((TOK_BLOCK, D_MODEL), lambda b, *_: (b, 0)),
            scratch_shapes=[
                pltpu.VMEM((2 * N_EXPERTS * SLOT_GROUPS, ROW_TILES, SUBLANES, LANES), _F32),
                pltpu.VMEM((2, TOK_BLOCK, D_MODEL), _F32),
                pltpu.SMEM((2 * N_EXPERTS * TOK_BLOCK,), jnp.int32),
                pltpu.SemaphoreType.DMA((2,)),
                pltpu.SemaphoreType.DMA((2,)),
                pltpu.SemaphoreType.DMA((2,)),
            ],
        ),
        out_shape=jax.ShapeDtypeStruct((T, D_MODEL), _F32),
        compiler_params=_params(("arbitrary",)),
        name="combine",
    )(cnt_t, off_t, lidx, ye, h1, gsel, ln_g, ln_b)


def _tables(seq):
    half = HEAD_DIM // 2
    inv = 1.0 / (ROPE_BASE ** (jnp.arange(half, dtype=_F32) / half))
    ang = jnp.arange(seq, dtype=_F32)[:, None] * inv[None, :]
    cos = jnp.concatenate([jnp.cos(ang), jnp.cos(ang)], axis=1)
    sin = jnp.concatenate([-jnp.sin(ang), jnp.sin(ang)], axis=1)
    log_g = jnp.log1p(-jnp.exp2(-5.0 - jnp.arange(N_HEADS, dtype=_F32)))[:, None, None]
    pos = jnp.arange(CHUNK, dtype=_F32)
    rows = lambda f: jnp.broadcast_to(jnp.exp(log_g * f[None, :, None]), (N_HEADS, CHUNK, HEAD_DIM))
    dsym = jnp.exp(log_g * jnp.abs(pos[:, None] - pos[None, :])[None])
    tabs = dict(
        cos=cos, sin=sin, dsym=dsym,
        qwf=rows(pos + 1.0), kwf=rows(CHUNK - 1.0 - pos),
        qwb=rows(CHUNK - pos), kwb=rows(pos),
        gl=rows(jnp.full((CHUNK,), float(CHUNK), _F32)),
    )
    spread = lambda chans: (jnp.arange(LANES)[:, None] == jnp.repeat(jnp.asarray(chans), LANES)[None, :]).astype(_MXU)
    tabs["spread"] = spread(range(N_EXPERTS))
    r = jnp.arange(SELECT_ROWS)
    tabs["before"] = (r[None, :] < r[:, None]).astype(_MXU)
    tabs["spread_f"] = spread([0, 1, 2, 3, 4, 5, 6, 7, 16, 17, 18, 19])
    tabs["spread_b"] = spread([8, 9, 10, 11, 12, 13, 14, 15, 24, 25, 26, 27])
    return tabs


def _trunk(x, w):
    batch, seq, _ = x.shape
    T = batch * seq
    nb = T // TOK_BLOCK
    cap = CAP_FACTOR * T // N_EXPERTS
    t = _tables(seq)
    h0, P, G2, GC, GR = _inproj(x.reshape(T, D_MODEL), seq, w["ln_in_g"], w["ln_in_b"], w["w_main"],
                                w["b_main"], w["wg"], w["bg"], t["cos"], t["sin"])
    YB = _sweep_bwd(P, GC, GR, t["qwb"], t["kwb"], t["gl"], t["spread_b"], batch, seq)
    mixed = _sweep_fwd(P, G2, GC, GR, YB, w["ret_g"], w["mlstm_g"], t["dsym"], t["qwf"], t["kwf"], t["gl"],
                       t["spread_f"], batch, seq)
    h1, aff = _outproj(mixed, h0, w["w_o"], w["ln1_g"], w["ln1_b"], w["w_r"])
    affc = aff[:, :N_EXPERTS].reshape(T // SUBLANES, LANES)
    thr, rem = _thresh(affc, cap)
    gsel, lidx, cnt, off = _select(aff, thr, rem, t["spread"], t["before"])
    cnt2 = cnt.reshape(nb, LANES)[:, :N_EXPERTS]
    off2 = off.reshape(nb, LANES)[:, :N_EXPERTS]
    lidx_e = lidx.reshape(nb, N_EXPERTS, TOK_BLOCK).transpose(1, 0, 2).reshape(N_EXPERTS, nb * TOK_BLOCK)
    ye = _ffn(cnt2.T.reshape(-1), lidx_e, h1, w["w_gate"], w["w_up"], w["w_down"], cap, min(256, cap))
    y = _combine(cnt2.reshape(-1), off2.reshape(-1), lidx.reshape(nb, N_EXPERTS * TOK_BLOCK), ye, gsel, h1,
                 w["ln2_g"], w["ln2_b"], cap)
    return y.reshape(batch, seq, D_MODEL)


def _prep_weights(ln_in_g, ln_in_b, w_in, b_in, ret_norm_g, mlstm_norm_g, w_o, ln1_g, ln1_b, w_router,
                  w_gate, w_up, w_down, ln2_g, ln2_b):
    main = 8 * SEC
    ngate = 4 * N_HEADS
    row = lambda v: v.reshape(1, -1).astype(_F32)
    wg = jnp.pad(w_in[0][:, main:main + ngate], ((0, 0), (0, LANES - ngate)))
    bg = jnp.pad(b_in[0][main:main + ngate], (0, LANES - ngate))
    return dict(
        ln_in_g=row(ln_in_g), ln_in_b=row(ln_in_b),
        w_main=w_in[0][:, :main].astype(_MXU), b_main=row(b_in[0][:main]),
        wg=wg.astype(_MXU), bg=row(bg),
        ret_g=row(ret_norm_g[0]), mlstm_g=row(mlstm_norm_g[0]),
        w_o=w_o[0].astype(_MXU), ln1_g=row(ln1_g[0]), ln1_b=row(ln1_b[0]),
        w_r=jnp.pad(w_router[0], ((0, 0), (0, LANES - N_EXPERTS))).astype(_MXU),
        w_gate=w_gate[0].astype(_MXU), w_up=w_up[0].astype(_MXU), w_down=w_down[0].astype(_MXU),
        ln2_g=row(ln2_g[0]), ln2_b=row(ln2_b[0]),
    )


def kernel(x_prompt, x_sample, ln_in_g, ln_in_b, w_in, b_in, ret_norm_g, mlstm_norm_g, w_o, ln1_g, ln1_b,
           w_router, w_gate, w_up, w_down, ln2_g, ln2_b):
    w = _prep_weights(ln_in_g, ln_in_b, w_in, b_in, ret_norm_g, mlstm_norm_g, w_o, ln1_g, ln1_b, w_router,
                      w_gate, w_up, w_down, ln2_g, ln2_b)
    return (_trunk(x_prompt, w), _trunk(x_sample, w))
```

```python
import functools

import jax
import jax.numpy as jnp
from jax import lax
from jax.experimental import pallas as pl
from jax.experimental.pallas import tpu as pltpu

D_MODEL = 1024
N_HEADS = 4
HEAD_DIM = 128
SEC = N_HEADS * HEAD_DIM
CHUNK = 128
N_EXPERTS = 16
D_FF = 2 * D_MODEL
CAP_FACTOR = 2
ROPE_BASE = 10000.0
LN_EPS = 1e-5
NEG_BIG = -1e30
DEPTH = 1
ALPHA = (2.0 * DEPTH) ** 0.25
K_SCALE = HEAD_DIM ** -0.5
LANES = 128
SUBLANES = 8
TOK_BLOCK = 128
MIN_NORMAL_BITS = 0x00800000
VMEM_LIMIT = 56 * 1024 * 1024

_MXU = jnp.bfloat16
_F32 = jnp.float32


def _dot(a, b):
    return jnp.dot(a, b, preferred_element_type=_F32)


def _dot_nt(a, b):
    return lax.dot_general(a, b, (((1,), (1,)), ((), ())), preferred_element_type=_F32)


def _split3(x):
    x1 = x.astype(_MXU)
    r1 = x - x1.astype(_F32)
    x2 = r1.astype(_MXU)
    r2 = r1 - x2.astype(_F32)
    return x1, x2, r2.astype(_MXU)


def _dot01_left(a01, x):
    x1, x2, x3 = _split3(x)
    return _dot(a01, x1) + _dot(a01, x2) + _dot(a01, x3)


def _dot01_right(x, a01):
    x1, x2, x3 = _split3(x)
    return _dot(x1, a01) + _dot(x2, a01) + _dot(x3, a01)


def _layer_norm(x, g, b):
    mu = jnp.mean(x, axis=-1, keepdims=True)
    xc = x - mu
    var = jnp.mean(xc * xc, axis=-1, keepdims=True)
    return xc * lax.rsqrt(var + LN_EPS) * g + b


def _log_sigmoid(x):
    return jnp.minimum(x, 0.0) - jnp.log1p(jnp.exp(-jnp.abs(x)))


def _sigmoid(x):
    return 1.0 / (1.0 + jnp.exp(-x))


def _params(sem):
    return pltpu.CompilerParams(dimension_semantics=sem, vmem_limit_bytes=VMEM_LIMIT)


ROW_TILES = D_MODEL // LANES


ISSUE_UNROLL = 4


def _for_each(lo, hi, body):
    nblk = lax.shift_right_logical(hi - lo, ISSUE_UNROLL.bit_length() - 1)

    def block(k, carry):
        for u in range(ISSUE_UNROLL):
            body(lo + k * ISSUE_UNROLL + u)
        return carry

    def single(i, carry):
        body(i)
        return carry

    lax.fori_loop(0, nblk, block, 0)
    lax.fori_loop(lo + nblk * ISSUE_UNROLL, hi, single, 0)


def _row_view(buf, r, group0=0):
    return buf.at[group0 + lax.shift_right_logical(r, 3), :, jnp.bitwise_and(r, SUBLANES - 1), :]


def _matrix_value(buf):
    rows = buf.shape[0] * SUBLANES
    return jnp.concatenate([buf[:, j].reshape(rows, LANES) for j in range(ROW_TILES)], axis=1)


def _row_tile_copies(mat, hbm, row0, sem):
    rows = mat.shape[0]
    return [pltpu.make_async_copy(mat.at[:, pl.ds(j * LANES, LANES)], hbm.at[pl.ds(row0, rows), j, :], sem)
            for j in range(ROW_TILES)]


def _matrix_copies(hbm, row0, mat, sem):
    rows = mat.shape[0]
    return [pltpu.make_async_copy(hbm.at[pl.ds(row0, rows), j, :], mat.at[:, pl.ds(j * LANES, LANES)], sem)
            for j in range(ROW_TILES)]


def _pipelined_writeback(buf, sem, hbm, value, step, nsteps, rows):
    slot = step % 2

    @pl.when(step >= 2)
    def _():
        for c in _row_tile_copies(buf.at[slot], hbm, (step - 2) * rows, sem.at[slot]):
            c.wait()

    buf[slot] = value
    for c in _row_tile_copies(buf.at[slot], hbm, step * rows, sem.at[slot]):
        c.start()

    @pl.when(step == nsteps - 1)
    def _():
        if nsteps > 1:
            for c in _row_tile_copies(buf.at[1 - slot], hbm, (step - 1) * rows, sem.at[1 - slot]):
                c.wait()
        for c in _row_tile_copies(buf.at[slot], hbm, step * rows, sem.at[slot]):
            c.wait()


def _running_max(x, reverse):
    n = x.shape[0]
    row = lax.broadcasted_iota(jnp.int32, x.shape, 0)
    step = 1
    while step < n:
        if reverse:
            shifted = jnp.where(row < n - step, pltpu.roll(x, n - step, 0), NEG_BIG)
        else:
            shifted = jnp.where(row >= step, pltpu.roll(x, step, 0), NEG_BIG)
        x = jnp.maximum(x, shifted)
        step *= 2
    return x


_P_COL = {0: 0, 1: 1, 2: 2, 4: 3, 5: 4, 6: 5}
_G2_COL = {3: 0, 7: 1}


def _inproj_kernel(x_ref, lg_ref, lb_ref, w_ref, b_ref, wg_ref, bg_ref,
                   cos_ref, sin_ref, h0_ref, p_ref, g2_ref, gc_ref, gr_ref):
    tm = x_ref.shape[0]
    h = _layer_norm(x_ref[...], lg_ref[...], lb_ref[...])
    h0_ref[...] = h
    hb = h.astype(_MXU)
    cos = cos_ref[...]
    sin = sin_ref[...]

    pre = _dot(hb, wg_ref[...]) + bg_ref[...]
    row = lax.broadcasted_iota(jnp.int32, (CHUNK, CHUNK), 0)
    col = lax.broadcasted_iota(jnp.int32, (CHUNK, CHUNK), 1)
    tri_le = (col <= row).astype(_MXU)
    tri_ge = (col >= row).astype(_MXU)
    for c in range(tm // CHUNK):
        sl = slice(c * CHUNK, (c + 1) * CHUNK)
        blk = pre[sl, :]
        ls = _log_sigmoid(blk)
        pref = _dot01_left(tri_le, ls)
        suf = _dot01_left(tri_ge, ls)
        is_cum_f = (col >= 4) & (col < 8)
        is_cum_b = (col >= 12) & (col < 16)
        cum = jnp.where(is_cum_f, pref, jnp.where(is_cum_b, suf, 0.0))
        excess = blk - pltpu.roll(cum, LANES - 4, 1)
        run_max = jnp.where(col < 4, _running_max(excess, False), _running_max(excess, True))
        is_max = ((col >= 16) & (col < 20)) | ((col >= 24) & (col < 28))
        gates = jnp.where(is_cum_f | is_cum_b, cum, jnp.where(is_max, pltpu.roll(run_max, 16, 1), blk))
        gc_ref[sl, :] = gates
        gr_ref[:, sl] = gates.T

    for sec in range(8):
        acc = _dot(hb, w_ref[:, sec * SEC:(sec + 1) * SEC]) + b_ref[:, sec * SEC:(sec + 1) * SEC]
        if sec in (0, 1):
            c0 = _P_COL[sec] * SEC
            for hh in range(N_HEADS):
                s = acc[:, hh * HEAD_DIM:(hh + 1) * HEAD_DIM]
                r = s * cos + pltpu.roll(s, HEAD_DIM // 2, 1) * sin
                if sec == 1:
                    r = r * K_SCALE
                p_ref[:, c0 + hh * HEAD_DIM:c0 + (hh + 1) * HEAD_DIM] = r.astype(p_ref.dtype)
        elif sec in _P_COL:
            if sec == 5:
                acc = acc * K_SCALE
            c0 = _P_COL[sec] * SEC
            p_ref[:, c0:c0 + SEC] = acc.astype(p_ref.dtype)
        else:
            c0 = _G2_COL[sec] * SEC
            g2_ref[:, c0:c0 + SEC] = acc


def _inproj(x2, seq, ln_g, ln_b, w_main, b_main, wg, bg, cos, sin, tm=512):
    T = x2.shape[0]
    nseq = seq // tm
    const = lambda i: (0, 0)
    return pl.pallas_call(
        _inproj_kernel,
        grid=(T // tm,),
        in_specs=[
            pl.BlockSpec((tm, D_MODEL), lambda i: (i, 0)),
            pl.BlockSpec((1, D_MODEL), const),
            pl.BlockSpec((1, D_MODEL), const),
            pl.BlockSpec((D_MODEL, 8 * SEC), const),
            pl.BlockSpec((1, 8 * SEC), const),
            pl.BlockSpec((D_MODEL, LANES), const),
            pl.BlockSpec((1, LANES), const),
            pl.BlockSpec((tm, HEAD_DIM), lambda i: (i % nseq, 0)),
            pl.BlockSpec((tm, HEAD_DIM), lambda i: (i % nseq, 0)),
        ],
        out_specs=[
            pl.BlockSpec((tm, D_MODEL), lambda i: (i, 0)),
            pl.BlockSpec((tm, 6 * SEC), lambda i: (i, 0)),
            pl.BlockSpec((tm, 2 * SEC), lambda i: (i, 0)),
            pl.BlockSpec((tm, LANES), lambda i: (i, 0)),
            pl.BlockSpec((LANES, tm), lambda i: (0, i)),
        ],
        out_shape=[
            jax.ShapeDtypeStruct((T, D_MODEL), _F32),
            jax.ShapeDtypeStruct((T, 6 * SEC), _MXU),
            jax.ShapeDtypeStruct((T, 2 * SEC), _F32),
            jax.ShapeDtypeStruct((T, LANES), _F32),
            jax.ShapeDtypeStruct((LANES, T), _F32),
        ],
        compiler_params=_params(("parallel",)),
        name="inproj",
    )(x2, ln_g, ln_b, w_main, b_main, wg, bg, cos, sin)


def _init_state(s_ref, cn_ref, m_ref):
    s_ref[...] = jnp.zeros(s_ref.shape, _F32)
    cn_ref[...] = jnp.zeros(cn_ref.shape, _F32)
    m_ref[...] = jnp.full(m_ref.shape, NEG_BIG, _F32)


def _dot_tn(a, b):
    return lax.dot_general(a, b, (((0,), (0,)), ((), ())), preferred_element_type=_F32)


def _lane_spread(gc, spread_ref):
    return _dot01_right(gc, spread_ref[...])


def _mlstm_direction(q, k, v, li_b, cum_b, max_b, li_row, cum_row, last_lane, mask, cn_ref, m_ref, si):
    cum_last = cum_row[:, last_lane:last_lane + 1]
    m_prev = m_ref[si, 0:1, :]
    cn_prev = cn_ref[si]
    ones = jnp.ones((CHUNK, HEAD_DIM), _MXU)
    v1 = jnp.concatenate([v, ones], axis=1)

    m_row = cum_b + jnp.maximum(max_b, m_prev)
    log_d = jnp.where(mask, cum_b - cum_row + li_row, NEG_BIG)
    d_w = jnp.exp(log_d - m_row)
    s_inter = jnp.exp(cum_b + m_prev - m_row)
    qk = _dot_nt(q, k) * d_w
    qk_hi = qk.astype(_MXU)
    qk_lo = (qk - qk_hi.astype(_F32)).astype(_MXU)
    intra = _dot(qk_hi, v1)
    inter = _dot(q, cn_prev.astype(_MXU))
    num = intra[:, :HEAD_DIM] + s_inter * inter[:, :HEAD_DIM]
    den = intra[:, HEAD_DIM:] + _dot(qk_lo, ones) + s_inter * inter[:, HEAD_DIM:]
    h_out = num / jnp.maximum(jnp.abs(den), jnp.exp(-m_row))

    a_max = jnp.max(cum_last - cum_row + li_row, axis=1, keepdims=True)
    kw = (k.astype(_F32) * jnp.exp(cum_last - cum_b + li_b - a_max)).astype(_MXU)
    m_new = jnp.maximum(cum_last + m_prev, a_max)
    s_old = jnp.exp(cum_last + m_prev - m_new)
    s_new = jnp.exp(a_max - m_new)
    cn_ref[si] = (jnp.concatenate([s_old, s_old], axis=1) * cn_prev
                  + jnp.concatenate([s_new, s_new], axis=1) * _dot_tn(kw, v1))
    m_ref[si] = jnp.broadcast_to(m_new, (SUBLANES, LANES))
    return h_out


def _retention_state_update(k, v, kw, gl, s_ref, si):
    kwv = (k.astype(_F32) * kw).astype(_MXU)
    s_ref[si] = gl * s_ref[si] + _dot_tn(kwv, v)


SEQ_PER_STEP = 4


def _head_cols(h, base=0):
    return slice(base + h * HEAD_DIM, base + (h + 1) * HEAD_DIM)


def _sweep_bwd_kernel(*refs):
    seq_refs = [refs[8 * k:8 * (k + 1)] for k in range(SEQ_PER_STEP)]
    qwb_ref, kwb_ref, gl_ref, spread_ref, yb_ref, s_ref, cn_ref, m_ref = refs[8 * SEQ_PER_STEP:]

    @pl.when(pl.program_id(1) == 0)
    def _():
        _init_state(s_ref, cn_ref, m_ref)

    row = lax.broadcasted_iota(jnp.int32, (CHUNK, CHUNK), 0)
    col = lax.broadcasted_iota(jnp.int32, (CHUNK, CHUNK), 1)
    mask = col >= row
    pairs = [(h, k) for h in range(N_HEADS) for k in range(SEQ_PER_STEP)]
    for h, k in pairs:
        sl = _head_cols(h)
        qs = (seq_refs[k][0][:, sl].astype(_F32) * qwb_ref[h]).astype(_MXU)
        yb_ref[k, :, sl] = _dot(qs, s_ref[k * N_HEADS + h].astype(_MXU))
    for h, k in pairs:
        sl = _head_cols(h)
        _retention_state_update(seq_refs[k][1][:, sl], seq_refs[k][2][:, sl], kwb_ref[h], gl_ref[h], s_ref,
                                k * N_HEADS + h)
    cols = [_lane_spread(seq_refs[k][6][...], spread_ref) for k in range(SEQ_PER_STEP)]
    for h in range(N_HEADS):
        sl = _head_cols(h)
        for k in range(SEQ_PER_STEP):
            mq_ref, mk_ref, mv_ref, _, gr_ref = seq_refs[k][3:8]
            h_b = _mlstm_direction(mq_ref[:, sl], mk_ref[:, sl], mv_ref[:, sl],
                                   cols[k][:, _head_cols(h)], cols[k][:, _head_cols(h, SEC)],
                                   cols[k][:, _head_cols(h, 2 * SEC)],
                                   gr_ref[8 + h:9 + h, :], gr_ref[12 + h:13 + h, :], 0, mask,
                                   cn_ref, m_ref, k * N_HEADS + h)
            yb_ref[k, :, _head_cols(h, SEC)] = h_b


def _sweep_fwd_kernel(*refs):
    seq_refs = [refs[10 * k:10 * (k + 1)] for k in range(SEQ_PER_STEP)]
    (rng_ref, mng_ref, dsym_ref, qwf_ref, kwf_ref, gl_ref, spread_ref,
     mixed_ref, s_ref, cn_ref, m_ref) = refs[10 * SEQ_PER_STEP:]

    @pl.when(pl.program_id(1) == 0)
    def _():
        _init_state(s_ref, cn_ref, m_ref)

    row = lax.broadcasted_iota(jnp.int32, (CHUNK, CHUNK), 0)
    col = lax.broadcasted_iota(jnp.int32, (CHUNK, CHUNK), 1)
    mask = col <= row

    pairs = [(h, k) for h in range(N_HEADS) for k in range(SEQ_PER_STEP)]

    def head_norms(ys):
        centred = [y - m for y, m in zip(ys, [jnp.mean(y, axis=1, keepdims=True) for y in ys])]
        var = [jnp.mean(c * c, axis=1, keepdims=True) for c in centred]
        return [c * lax.rsqrt(v + LN_EPS) for c, v in zip(centred, var)]

    qkv = [[seq_refs[k][i][:, _head_cols(h)] for i in range(3)] for h, k in pairs]
    scores = [_dot_nt(q, kk) for q, kk, _ in qkv]
    inter = [_dot((q.astype(_F32) * qwf_ref[h]).astype(_MXU), s_ref[k * N_HEADS + h].astype(_MXU))
             for (h, k), (q, _, _) in zip(pairs, qkv)]
    ys = [_dot((s * dsym_ref[h]).astype(_MXU), v) + it + seq_refs[k][8][:, _head_cols(h)]
          for (h, k), (_, _, v), s, it in zip(pairs, qkv, scores, inter)]
    for (h, k), (_, kk, v) in zip(pairs, qkv):
        _retention_state_update(kk, v, kwf_ref[h], gl_ref[h], s_ref, k * N_HEADS + h)
    for (h, k), yn in zip(pairs, head_norms(ys)):
        sl = _head_cols(h)
        g = seq_refs[k][9][:, sl]
        mixed_ref[k, :, sl] = (yn * rng_ref[:, sl] * (g * _sigmoid(g))).astype(mixed_ref.dtype)

    cols = [_lane_spread(seq_refs[k][6][...], spread_ref) for k in range(SEQ_PER_STEP)]
    ys = []
    for h, k in pairs:
        sl = _head_cols(h)
        mq_ref, mk_ref, mv_ref, _, gr_ref, yb_ref = seq_refs[k][3:9]
        h_f = _mlstm_direction(mq_ref[:, sl], mk_ref[:, sl], mv_ref[:, sl],
                               cols[k][:, _head_cols(h)], cols[k][:, _head_cols(h, SEC)],
                               cols[k][:, _head_cols(h, 2 * SEC)],
                               gr_ref[h:h + 1, :], gr_ref[4 + h:5 + h, :], CHUNK - 1, mask,
                               cn_ref, m_ref, k * N_HEADS + h)
        ys.append(h_f + yb_ref[:, _head_cols(h, SEC)])
    for (h, k), yn in zip(pairs, head_norms(ys)):
        sl2 = _head_cols(h, SEC)
        out = yn * mng_ref[:, _head_cols(h)] * _sigmoid(seq_refs[k][9][:, sl2])
        mixed_ref[k, :, sl2] = out.astype(mixed_ref.dtype)


def _state_scratch():
    return [
        pltpu.VMEM((SEQ_PER_STEP * N_HEADS, HEAD_DIM, HEAD_DIM), _F32),
        pltpu.VMEM((SEQ_PER_STEP * N_HEADS, HEAD_DIM, 2 * HEAD_DIM), _F32),
        pltpu.VMEM((SEQ_PER_STEP * N_HEADS, SUBLANES, LANES), _F32),
    ]


_SPREAD_SPEC = pl.BlockSpec((LANES, 3 * SEC), lambda b, n: (0, 0))


def _sweep_specs(nchunk, reverse, n_wide):
    def chunk(n):
        return (nchunk - 1 - n) if reverse else n

    per_seq = []
    for k in range(SEQ_PER_STEP):
        def rb(b, n, k=k):
            return (b * SEQ_PER_STEP + k) * nchunk + chunk(n)
        per_seq += [pl.BlockSpec((CHUNK, SEC), functools.partial(lambda b, n, s, rb: (rb(b, n), s), s=s, rb=rb))
                    for s in range(6)]
        per_seq.append(pl.BlockSpec((CHUNK, LANES), functools.partial(lambda b, n, rb: (rb(b, n), 0), rb=rb)))
        per_seq.append(pl.BlockSpec((LANES, CHUNK), functools.partial(lambda b, n, rb: (0, rb(b, n)), rb=rb)))
        per_seq += [pl.BlockSpec((CHUNK, 2 * SEC), functools.partial(lambda b, n, rb: (rb(b, n), 0), rb=rb))
                    for _ in range(n_wide)]
    out = pl.BlockSpec((SEQ_PER_STEP, CHUNK, 2 * SEC), lambda b, n: (b, chunk(n), 0))
    tab = pl.BlockSpec((N_HEADS, CHUNK, HEAD_DIM), lambda b, n: (0, 0, 0))
    return per_seq, out, tab


def _sweep_bwd(P, GC, GR, qwb, kwb, gl, spread, batch, seq):
    nchunk = seq // CHUNK
    per_seq, out, tab = _sweep_specs(nchunk, True, 0)
    yb = pl.pallas_call(
        _sweep_bwd_kernel,
        grid=(batch // SEQ_PER_STEP, nchunk),
        in_specs=per_seq + [tab, tab, tab, _SPREAD_SPEC],
        out_specs=out,
        out_shape=jax.ShapeDtypeStruct((batch, seq, 2 * SEC), _F32),
        scratch_shapes=_state_scratch(),
        compiler_params=_params(("parallel", "arbitrary")),
        name="sweep_bwd",
    )(*([P] * 6 + [GC, GR]) * SEQ_PER_STEP, qwb, kwb, gl, spread)
    return yb.reshape(batch * seq, 2 * SEC)


def _sweep_fwd(P, G2, GC, GR, YB, rng, mng, dsym, qwf, kwf, gl, spread, batch, seq):
    nchunk = seq // CHUNK
    per_seq, out, tab = _sweep_specs(nchunk, False, 2)
    gain = pl.BlockSpec((1, SEC), lambda b, n: (0, 0))
    mixed = pl.pallas_call(
        _sweep_fwd_kernel,
        grid=(batch // SEQ_PER_STEP, nchunk),
        in_specs=per_seq + [gain, gain, tab, tab, tab, tab, _SPREAD_SPEC],
        out_specs=out,
        out_shape=jax.ShapeDtypeStruct((batch, seq, 2 * SEC), _MXU),
        scratch_shapes=_state_scratch(),
        compiler_params=_params(("parallel", "arbitrary")),
        name="sweep_fwd",
    )(*([P] * 6 + [GC, GR, YB, G2]) * SEQ_PER_STEP, rng, mng, dsym, qwf, kwf, gl, spread)
    return mixed.reshape(batch * seq, 2 * SEC)


def _outproj_kernel(mixed_ref, h0_ref, wo_ref, lg_ref, lb_ref, wr_ref, h1_hbm, aff_ref, hbuf, sem, *, nsteps):
    tm = mixed_ref.shape[0]
    parts = [slice(i * tm // 2, (i + 1) * tm // 2) for i in range(2)]
    zs = [ALPHA * h0_ref[p, :] + _dot(mixed_ref[p, :], wo_ref[...]) for p in parts]
    h1s = [_layer_norm(z, lg_ref[...], lb_ref[...]) for z in zs]
    _pipelined_writeback(hbuf, sem, h1_hbm, jnp.concatenate(h1s, axis=0), pl.program_id(0), nsteps, tm)
    logits = [_dot(h1.astype(_MXU), wr_ref[...]) for h1 in h1s]
    valid = lax.broadcasted_iota(jnp.int32, logits[0].shape, 1) < N_EXPERTS
    logits = [jnp.where(valid, lg, NEG_BIG) for lg in logits]
    es = [jnp.exp(lg - jnp.max(lg, axis=1, keepdims=True)) for lg in logits]
    for p, e in zip(parts, es):
        aff_ref[p, :] = jnp.where(valid, e / jnp.sum(e, axis=1, keepdims=True), 0.0)


def _outproj(mixed, h0, wo, ln_g, ln_b, wr, tm=512):
    T = mixed.shape[0]
    const = lambda i: (0, 0)
    return pl.pallas_call(
        functools.partial(_outproj_kernel, nsteps=T // tm),
        grid=(T // tm,),
        in_specs=[
            pl.BlockSpec((tm, D_MODEL), lambda i: (i, 0)),
            pl.BlockSpec((tm, D_MODEL), lambda i: (i, 0)),
            pl.BlockSpec((D_MODEL, D_MODEL), const),
            pl.BlockSpec((1, D_MODEL), const),
            pl.BlockSpec((1, D_MODEL), const),
            pl.BlockSpec((D_MODEL, LANES), const),
        ],
        out_specs=[
            pl.BlockSpec(memory_space=pl.ANY),
            pl.BlockSpec((tm, LANES), lambda i: (i, 0)),
        ],
        out_shape=[
            jax.ShapeDtypeStruct((T, ROW_TILES, LANES), _F32),
            jax.ShapeDtypeStruct((T, LANES), _F32),
        ],
        scratch_shapes=[pltpu.VMEM((2, tm, D_MODEL), _F32), pltpu.SemaphoreType.DMA((2,))],
        compiler_params=_params(("arbitrary",)),
        name="outproj",
    )(mixed, h0, wo, ln_g, ln_b, wr)


def _thresh_kernel(aff_ref, thr_ref, rem_ref, *, cap):
    rows = aff_ref.shape[0]
    aff = aff_ref[...]

    def count(pred):
        c = jnp.sum(pred.astype(jnp.int32).reshape(rows // SUBLANES, SUBLANES, LANES), axis=0)
        c = jnp.broadcast_to(jnp.sum(c, axis=0, keepdims=True), (SUBLANES, LANES))
        for shift in (64, 32, 16):
            c = c + pltpu.roll(c, shift, 1)
        return c

    def body(i, ans):
        cand = ans | jnp.left_shift(jnp.int32(1), 30 - i)
        c = count(aff >= lax.bitcast_convert_type(cand[0:1, :], _F32))
        return jnp.where(c >= cap, cand, ans)

    ans = lax.fori_loop(0, 31, body, jnp.zeros((SUBLANES, LANES), jnp.int32))
    thr = jnp.where(ans >= MIN_NORMAL_BITS, lax.bitcast_convert_type(ans, _F32), 0.0)
    thr_ref[...] = thr
    rem_ref[...] = cap - count(aff > thr[0:1, :])


def _thresh(affc, cap):
    return pl.pallas_call(
        functools.partial(_thresh_kernel, cap=cap),
        out_shape=[jax.ShapeDtypeStruct((SUBLANES, LANES), _F32),
                   jax.ShapeDtypeStruct((SUBLANES, LANES), jnp.int32)],
        compiler_params=pltpu.CompilerParams(vmem_limit_bytes=VMEM_LIMIT),
        name="thresh",
    )(affc)


SELECT_BLOCKS = 4
SELECT_ROWS = SELECT_BLOCKS * TOK_BLOCK


def _select_kernel(aff_ref, thr_ref, rem_ref, spread_ref, before_ref, gsel_ref, lidx_ref, cnt_ref, off_ref,
                   nsel_ref, neq_ref):
    @pl.when(pl.program_id(0) == 0)
    def _():
        nsel_ref[...] = jnp.zeros(nsel_ref.shape, _F32)
        neq_ref[...] = jnp.zeros(neq_ref.shape, _F32)

    aff = aff_ref[...]
    thr = thr_ref[0:1, :]
    rem = rem_ref[0:1, :].astype(_F32)
    valid = lax.broadcasted_iota(jnp.int32, aff.shape, 1) < N_EXPERTS
    before = before_ref[...]
    gt = (aff > thr) & valid
    eq = (aff == thr) & valid
    eq_before = _dot(before, eq.astype(_MXU)) + neq_ref[0:1, :]
    sel = gt | (eq & (eq_before < rem))
    pos = _dot(before, sel.astype(_MXU))
    neq_ref[0:1, :] = neq_ref[0:1, :] + jnp.sum(eq.astype(_F32), axis=0, keepdims=True)
    gsel_ref[...] = jnp.where(sel, aff, 0.0)

    slot = (lax.broadcasted_iota(jnp.int32, (TOK_BLOCK, N_EXPERTS * TOK_BLOCK), 1) % TOK_BLOCK).astype(_F32)
    tok = lax.broadcasted_iota(jnp.int32, (TOK_BLOCK, N_EXPERTS * TOK_BLOCK), 0)
    start = nsel_ref[0:1, :]
    taken = start
    for q in range(SELECT_BLOCKS):
        rows = slice(q * TOK_BLOCK, (q + 1) * TOK_BLOCK)
        sel_q = sel[rows, :]
        off_ref[q] = taken.astype(jnp.int32)
        cnt = jnp.sum(sel_q.astype(_F32), axis=0, keepdims=True)
        cnt_ref[q] = cnt.astype(jnp.int32)
        ranked = jnp.where(sel_q, pos[rows, :] - (taken - start), -1.0).astype(_MXU)
        spread = _dot(ranked, spread_ref[...])
        tok0 = (pl.program_id(0) * SELECT_BLOCKS + q) * TOK_BLOCK
        lidx_ref[q] = jnp.sum(jnp.where(spread == slot, tok + tok0, 0), axis=0, keepdims=True)
        taken = taken + cnt
    nsel_ref[0:1, :] = taken


def _select(aff, thr, rem, spread, before):
    T = aff.shape[0]
    nb = T // TOK_BLOCK
    const = lambda b: (0, 0)
    return pl.pallas_call(
        _select_kernel,
        grid=(nb // SELECT_BLOCKS,),
        in_specs=[
            pl.BlockSpec((SELECT_ROWS, LANES), lambda b: (b, 0)),
            pl.BlockSpec((SUBLANES, LANES), const),
            pl.BlockSpec((SUBLANES, LANES), const),
            pl.BlockSpec((LANES, N_EXPERTS * TOK_BLOCK), const),
            pl.BlockSpec((SELECT_ROWS, SELECT_ROWS), const),
        ],
        out_specs=[
            pl.BlockSpec((SELECT_ROWS, LANES), lambda b: (b, 0)),
            pl.BlockSpec((SELECT_BLOCKS, 1, N_EXPERTS * TOK_BLOCK), lambda b: (b, 0, 0)),
            pl.BlockSpec((SELECT_BLOCKS, 1, LANES), lambda b: (b, 0, 0)),
            pl.BlockSpec((SELECT_BLOCKS, 1, LANES), lambda b: (b, 0, 0)),
        ],
        out_shape=[
            jax.ShapeDtypeStruct((T, LANES), _F32),
            jax.ShapeDtypeStruct((nb, 1, N_EXPERTS * TOK_BLOCK), jnp.int32),
            jax.ShapeDtypeStruct((nb, 1, LANES), jnp.int32),
            jax.ShapeDtypeStruct((nb, 1, LANES), jnp.int32),
        ],
        scratch_shapes=[pltpu.VMEM((SUBLANES, LANES), _F32), pltpu.VMEM((SUBLANES, LANES), _F32)],
        compiler_params=_params(("arbitrary",)),
        name="select",
    )(aff, thr, rem, spread, before)


FFN_ROWS = 512


def _ffn_kernel(cnt_ref, lidx_hbm, h1_hbm, wg_ref, wu_ref, wd_ref, y_hbm,
                xbuf, ybuf, lidx_smem, walk_ref, sem_idx, sem_rows, sem_out, *, tm, per, nb):
    e = pl.program_id(0)
    j = pl.program_id(1)
    step = e * per + j
    slot = step % 2

    per_expert = nb * TOK_BLOCK

    def idx_copy(en):
        return pltpu.make_async_copy(lidx_hbm.at[en], lidx_smem.at[pl.ds((en % 2) * per_expert, per_expert)],
                                     sem_idx.at[en % 2])

    def issue_tile(en, dst_slot):
        list_base = (en % 2) * per_expert
        group0 = dst_slot * (tm // SUBLANES)
        sem = sem_rows.at[dst_slot]

        def cond(st):
            return st[0] < tm

        def body(st):
            n, b, r = st
            c = cnt_ref[en * nb + b]
            take = jnp.minimum(c - r, tm - n)
            src_minus_dst = list_base + b * TOK_BLOCK + r - n

            def one(m):
                t = lidx_smem[src_minus_dst + m]
                pltpu.make_async_copy(h1_hbm.at[t], _row_view(xbuf, m, group0), sem).start()

            _for_each(n, n + take, one)
            done = r + take >= c
            return n + take, jnp.where(done, b + 1, b), jnp.where(done, 0, r + take)

        _, b, r = lax.while_loop(cond, body, (jnp.int32(0), walk_ref[0], walk_ref[1]))
        walk_ref[0] = b
        walk_ref[1] = r

    def restart_walk():
        walk_ref[0] = jnp.int32(0)
        walk_ref[1] = jnp.int32(0)

    @pl.when(step == 0)
    def _():
        idx_copy(0).start()
        idx_copy(0).wait()
        restart_walk()
        issue_tile(0, 0)

    @pl.when((j == 0) & (e + 1 < N_EXPERTS))
    def _():
        idx_copy(e + 1).start()

    @pl.when(j + 1 < per)
    def _():
        issue_tile(e, 1 - slot)

    @pl.when((j + 1 == per) & (e + 1 < N_EXPERTS))
    def _():
        idx_copy(e + 1).wait()
        restart_walk()
        issue_tile(e + 1, 1 - slot)

    pltpu.make_async_copy(h1_hbm.at[pl.ds(0, tm)], h1_hbm.at[pl.ds(0, tm)], sem_rows.at[slot]).wait()
    x = _matrix_value(xbuf.at[pl.ds(slot * (tm // SUBLANES), tm // SUBLANES)]).astype(_MXU)
    g = _dot(x, wg_ref[0])
    u = _dot(x, wu_ref[0])
    hid = (g * _sigmoid(g) * u).astype(_MXU)
    _pipelined_writeback(ybuf, sem_out, y_hbm, _dot(hid, wd_ref[0]), step, N_EXPERTS * per, tm)


def _ffn(cnt_e, lidx_e, h1, wg, wu, wd, cap, tm):
    per = cap // tm
    nb = lidx_e.shape[1] // TOK_BLOCK
    return pl.pallas_call(
        functools.partial(_ffn_kernel, tm=tm, per=per, nb=nb),
        grid_spec=pltpu.PrefetchScalarGridSpec(
            num_scalar_prefetch=1,
            grid=(N_EXPERTS, per),
            in_specs=[
                pl.BlockSpec(memory_space=pl.ANY),
                pl.BlockSpec(memory_space=pl.ANY),
                pl.BlockSpec((1, D_MODEL, D_FF), lambda e, j, *_: (e, 0, 0)),
                pl.BlockSpec((1, D_MODEL, D_FF), lambda e, j, *_: (e, 0, 0)),
                pl.BlockSpec((1, D_FF, D_MODEL), lambda e, j, *_: (e, 0, 0)),
            ],
            out_specs=pl.BlockSpec(memory_space=pl.ANY),
            scratch_shapes=[
                pltpu.VMEM((2 * tm // SUBLANES, ROW_TILES, SUBLANES, LANES), _F32),
                pltpu.VMEM((2, tm, D_MODEL), _F32),
                pltpu.SMEM((2 * nb * TOK_BLOCK,), jnp.int32),
                pltpu.SMEM((2,), jnp.int32),
                pltpu.SemaphoreType.DMA((2,)),
                pltpu.SemaphoreType.DMA((2,)),
                pltpu.SemaphoreType.DMA((2,)),
            ],
        ),
        out_shape=jax.ShapeDtypeStruct((N_EXPERTS * cap, ROW_TILES, LANES), _F32),
        compiler_params=_params(("arbitrary", "arbitrary")),
        name="ffn",
    )(cnt_e, lidx_e, h1, wg, wu, wd)


SLOT_GROUPS = TOK_BLOCK // SUBLANES
COMBINE_ROWS = 32


def _combine_kernel(cnt_ref, off_ref, lidx_hbm, ye_hbm, h1_hbm, gsel_ref, lg_ref, lb_ref, y_ref,
                    slots_ref, hres_ref, lidx_smem, sem_idx, sem_rows, sem_res, *, cap, nb):
    b = pl.program_id(0)
    slot = b % 2

    per_block = N_EXPERTS * TOK_BLOCK

    def idx_copy(bn):
        return pltpu.make_async_copy(lidx_hbm.at[bn], lidx_smem.at[pl.ds((bn % 2) * per_block, per_block)],
                                     sem_idx.at[bn % 2])

    def res_copies(bn):
        return _matrix_copies(h1_hbm, bn * TOK_BLOCK, hres_ref.at[bn % 2], sem_res.at[bn % 2])

    def issue_block(bn):
        par = bn % 2
        sem = sem_rows.at[par]
        tok0 = bn * TOK_BLOCK
        for c in res_copies(bn):
            c.start()
        for e in range(N_EXPERTS):
            c = cnt_ref[bn * N_EXPERTS + e]
            base = e * cap + off_ref[bn * N_EXPERTS + e]
            list_minus_src = par * per_block + e * TOK_BLOCK - base
            group0 = (par * N_EXPERTS + e) * SLOT_GROUPS

            def one(src_row):
                t = lidx_smem[list_minus_src + src_row] - tok0
                pltpu.make_async_copy(ye_hbm.at[src_row], _row_view(slots_ref, t, group0), sem).start()

            _for_each(base, base + c, one)

    @pl.when(b == 0)
    def _():
        slots_ref[...] = jnp.zeros(slots_ref.shape, _F32)
        idx_copy(0).start()
        idx_copy(0).wait()
        issue_block(0)
        if nb > 1:
            idx_copy(1).start()

    @pl.when(b + 1 < nb)
    def _():
        idx_copy(b + 1).wait()
        issue_block(b + 1)

    @pl.when(b + 2 < nb)
    def _():
        idx_copy(b + 2).start()

    for e in range(N_EXPERTS):
        c = cnt_ref[b * N_EXPERTS + e]

        @pl.when(c > 0)
        def _():
            pltpu.make_async_copy(ye_hbm.at[pl.ds(0, c)], ye_hbm.at[pl.ds(0, c)], sem_rows.at[slot]).wait()

    for c in res_copies(b):
        c.wait()

    groups = COMBINE_ROWS // SUBLANES
    for tg in range(TOK_BLOCK // COMBINE_ROWS):
        rows = slice(tg * COMBINE_ROWS, (tg + 1) * COMBINE_ROWS)
        acc = [ALPHA * hres_ref[slot, rows, j * LANES:(j + 1) * LANES] for j in range(ROW_TILES)]
        for e in range(N_EXPERTS):
            gate = jnp.broadcast_to(gsel_ref[rows, e:e + 1], (COMBINE_ROWS, LANES))
            blk = slots_ref[pl.ds((slot * N_EXPERTS + e) * SLOT_GROUPS + tg * groups, groups)]
            for j in range(ROW_TILES):
                acc[j] = acc[j] + gate * blk[:, j].reshape(COMBINE_ROWS, LANES)
        for j in range(ROW_TILES):
            y_ref[rows, j * LANES:(j + 1) * LANES] = acc[j]
    y_ref[...] = _layer_norm(y_ref[...], lg_ref[...], lb_ref[...])


def _combine(cnt_t, off_t, lidx, ye, gsel, h1, ln_g, ln_b, cap):
    T = h1.shape[0]
    nb = T // TOK_BLOCK
    return pl.pallas_call(
        functools.partial(_combine_kernel, cap=cap, nb=nb),
        grid_spec=pltpu.PrefetchScalarGridSpec(
            num_scalar_prefetch=2,
            grid=(nb,),
            in_specs=[
                pl.BlockSpec(memory_space=pl.ANY),
                pl.BlockSpec(memory_space=pl.ANY),
                pl.BlockSpec(memory_space=pl.ANY),
                pl.BlockSpec((TOK_BLOCK, LANES), lambda b, *_: (b, 0)),
                pl.BlockSpec((1, D_MODEL), lambda b, *_: (0, 0)),
                pl.BlockSpec((1, D_MODEL), lambda b, *_: (0, 0)),
            ],
            out_specs=pl.BlockSpec((TOK_BLOCK, D_MODEL), lambda b, *_: (b, 0)),
            scratch_shapes=[
                pltpu.VMEM((2 * N_EXPERTS * SLOT_GROUPS, ROW_TILES, SUBLANES, LANES), _F32),
                pltpu.VMEM((2, TOK_BLOCK, D_MODEL), _F32),
                pltpu.SMEM((2 * N_EXPERTS * TOK_BLOCK,), jnp.int32),
                pltpu.SemaphoreType.DMA((2,)),
                pltpu.SemaphoreType.DMA((2,)),
                pltpu.SemaphoreType.DMA((2,)),
            ],
        ),
        out_shape=jax.ShapeDtypeStruct((T, D_MODEL), _F32),
        compiler_params=_params(("arbitrary",)),
        name="combine",
    )(cnt_t, off_t, lidx, ye, h1, gsel, ln_g, ln_b)


def _tables(seq):
    half = HEAD_DIM // 2
    inv = 1.0 / (ROPE_BASE ** (jnp.arange(half, dtype=_F32) / half))
    ang = jnp.arange(seq, dtype=_F32)[:, None] * inv[None, :]
    cos = jnp.concatenate([jnp.cos(ang), jnp.cos(ang)], axis=1)
    sin = jnp.concatenate([-jnp.sin(ang), jnp.sin(ang)], axis=1)
    log_g = jnp.log1p(-jnp.exp2(-5.0 - jnp.arange(N_HEADS, dtype=_F32)))[:, None, None]
    pos = jnp.arange(CHUNK, dtype=_F32)
    rows = lambda f: jnp.broadcast_to(jnp.exp(log_g * f[None, :, None]), (N_HEADS, CHUNK, HEAD_DIM))
    dsym = jnp.exp(log_g * jnp.abs(pos[:, None] - pos[None, :])[None])
    tabs = dict(
        cos=cos, sin=sin, dsym=dsym,
        qwf=rows(pos + 1.0), kwf=rows(CHUNK - 1.0 - pos),
        qwb=rows(CHUNK - pos), kwb=rows(pos),
        gl=rows(jnp.full((CHUNK,), float(CHUNK), _F32)),
    )
    spread = lambda chans: (jnp.arange(LANES)[:, None] == jnp.repeat(jnp.asarray(chans), LANES)[None, :]).astype(_MXU)
    tabs["spread"] = spread(range(N_EXPERTS))
    r = jnp.arange(SELECT_ROWS)
    tabs["before"] = (r[None, :] < r[:, None]).astype(_MXU)
    tabs["spread_f"] = spread([0, 1, 2, 3, 4, 5, 6, 7, 16, 17, 18, 19])
    tabs["spread_b"] = spread([8, 9, 10, 11, 12, 13, 14, 15, 24, 25, 26, 27])
    return tabs


def _trunk(x, w):
    batch, seq, _ = x.shape
    T = batch * seq
    nb = T // TOK_BLOCK
    cap = CAP_FACTOR * T // N_EXPERTS
    t = _tables(seq)
    h0, P, G2, GC, GR = _inproj(x.reshape(T, D_MODEL), seq, w["ln_in_g"], w["ln_in_b"], w["w_main"],
                                w["b_main"], w["wg"], w["bg"], t["cos"], t["sin"])
    YB = _sweep_bwd(P, GC, GR, t["qwb"], t["kwb"], t["gl"], t["spread_b"], batch, seq)
    mixed = _sweep_fwd(P, G2, GC, GR, YB, w["ret_g"], w["mlstm_g"], t["dsym"], t["qwf"], t["kwf"], t["gl"],
                       t["spread_f"], batch, seq)
    h1, aff = _outproj(mixed, h0, w["w_o"], w["ln1_g"], w["ln1_b"], w["w_r"])
    affc = aff[:, :N_EXPERTS].reshape(T // SUBLANES, LANES)
    thr, rem = _thresh(affc, cap)
    gsel, lidx, cnt, off = _select(aff, thr, rem, t["spread"], t["before"])
    cnt2 = cnt.reshape(nb, LANES)[:, :N_EXPERTS]
    off2 = off.reshape(nb, LANES)[:, :N_EXPERTS]
    lidx_e = lidx.reshape(nb, N_EXPERTS, TOK_BLOCK).transpose(1, 0, 2).reshape(N_EXPERTS, nb * TOK_BLOCK)
    ye = _ffn(cnt2.T.reshape(-1), lidx_e, h1, w["w_gate"], w["w_up"], w["w_down"], cap, min(FFN_ROWS, cap))
    y = _combine(cnt2.reshape(-1), off2.reshape(-1), lidx.reshape(nb, N_EXPERTS * TOK_BLOCK), ye, gsel, h1,
                 w["ln2_g"], w["ln2_b"], cap)
    return y.reshape(batch, seq, D_MODEL)


def _prep_weights(ln_in_g, ln_in_b, w_in, b_in, ret_norm_g, mlstm_norm_g, w_o, ln1_g, ln1_b, w_router,
                  w_gate, w_up, w_down, ln2_g, ln2_b):
    main = 8 * SEC
    ngate = 4 * N_HEADS
    row = lambda v: v.reshape(1, -1).astype(_F32)
    wg = jnp.pad(w_in[0][:, main:main + ngate], ((0, 0), (0, LANES - ngate)))
    bg = jnp.pad(b_in[0][main:main + ngate], (0, LANES - ngate))
    return dict(
        ln_in_g=row(ln_in_g), ln_in_b=row(ln_in_b),
        w_main=w_in[0][:, :main].astype(_MXU), b_main=row(b_in[0][:main]),
        wg=wg.astype(_MXU), bg=row(bg),
        ret_g=row(ret_norm_g[0]), mlstm_g=row(mlstm_norm_g[0]),
        w_o=w_o[0].astype(_MXU), ln1_g=row(ln1_g[0]), ln1_b=row(ln1_b[0]),
        w_r=jnp.pad(w_router[0], ((0, 0), (0, LANES - N_EXPERTS))).astype(_MXU),
        w_gate=w_gate[0].astype(_MXU), w_up=w_up[0].astype(_MXU), w_down=w_down[0].astype(_MXU),
        ln2_g=row(ln2_g[0]), ln2_b=row(ln2_b[0]),
    )


def kernel(x_prompt, x_sample, ln_in_g, ln_in_b, w_in, b_in, ret_norm_g, mlstm_norm_g, w_o, ln1_g, ln1_b,
           w_router, w_gate, w_up, w_down, ln2_g, ln2_b):
    w = _prep_weights(ln_in_g, ln_in_b, w_in, b_in, ret_norm_g, mlstm_norm_g, w_o, ln1_g, ln1_b, w_router,
                      w_gate, w_up, w_down, ln2_g, ln2_b)
    return (_trunk(x_prompt, w), _trunk(x_sample, w))
```

```python
import functools

import jax
import jax.numpy as jnp
from jax import lax
from jax.experimental import pallas as pl
from jax.experimental.pallas import tpu as pltpu

D_MODEL = 1024
N_HEADS = 4
HEAD_DIM = 128
SEC = N_HEADS * HEAD_DIM
CHUNK = 128
N_EXPERTS = 16
D_FF = 2 * D_MODEL
CAP_FACTOR = 2
ROPE_BASE = 10000.0
LN_EPS = 1e-5
NEG_BIG = -1e30
DEPTH = 1
ALPHA = (2.0 * DEPTH) ** 0.25
K_SCALE = HEAD_DIM ** -0.5
LANES = 128
SUBLANES = 8
TOK_BLOCK = 128
MIN_NORMAL_BITS = 0x00800000
VMEM_LIMIT = 56 * 1024 * 1024

_MXU = jnp.bfloat16
_F32 = jnp.float32


def _dot(a, b):
    return jnp.dot(a, b, preferred_element_type=_F32)


def _dot_nt(a, b):
    return lax.dot_general(a, b, (((1,), (1,)), ((), ())), preferred_element_type=_F32)


def _split3(x):
    x1 = x.astype(_MXU)
    r1 = x - x1.astype(_F32)
    x2 = r1.astype(_MXU)
    r2 = r1 - x2.astype(_F32)
    return x1, x2, r2.astype(_MXU)


def _dot01_left(a01, x):
    x1, x2, x3 = _split3(x)
    return _dot(a01, x1) + _dot(a01, x2) + _dot(a01, x3)


def _dot01_right(x, a01):
    x1, x2, x3 = _split3(x)
    return _dot(x1, a01) + _dot(x2, a01) + _dot(x3, a01)


def _layer_norm(x, g, b):
    mu = jnp.mean(x, axis=-1, keepdims=True)
    xc = x - mu
    var = jnp.mean(xc * xc, axis=-1, keepdims=True)
    return xc * lax.rsqrt(var + LN_EPS) * g + b


def _log_sigmoid(x):
    return jnp.minimum(x, 0.0) - jnp.log1p(jnp.exp(-jnp.abs(x)))


def _sigmoid(x):
    return 1.0 / (1.0 + jnp.exp(-x))


def _params(sem):
    return pltpu.CompilerParams(dimension_semantics=sem, vmem_limit_bytes=VMEM_LIMIT)


ROW_TILES = D_MODEL // LANES


ISSUE_UNROLL = 4


def _for_each(lo, hi, body):
    nblk = lax.shift_right_logical(hi - lo, ISSUE_UNROLL.bit_length() - 1)

    def block(k, carry):
        for u in range(ISSUE_UNROLL):
            body(lo + k * ISSUE_UNROLL + u, u % 2)
        return carry

    def single(i, carry):
        body(i, 0)
        return carry

    lax.fori_loop(0, nblk, block, 0)
    lax.fori_loop(lo + nblk * ISSUE_UNROLL, hi, single, 0)


def _row_view(buf, r, group0=0):
    return buf.at[group0 + lax.shift_right_logical(r, 3), :, jnp.bitwise_and(r, SUBLANES - 1), :]


def _matrix_value(buf):
    rows = buf.shape[0] * SUBLANES
    return jnp.concatenate([buf[:, j].reshape(rows, LANES) for j in range(ROW_TILES)], axis=1)


def _row_tile_copies(mat, hbm, row0, sem):
    rows = mat.shape[0]
    return [pltpu.make_async_copy(mat.at[:, pl.ds(j * LANES, LANES)], hbm.at[pl.ds(row0, rows), j, :], sem)
            for j in range(ROW_TILES)]


def _matrix_copies(hbm, row0, mat, sem):
    rows = mat.shape[0]
    return [pltpu.make_async_copy(hbm.at[pl.ds(row0, rows), j, :], mat.at[:, pl.ds(j * LANES, LANES)], sem)
            for j in range(ROW_TILES)]


def _pipelined_writeback(buf, sem, hbm, value, step, nsteps, rows):
    slot = step % 2

    @pl.when(step >= 2)
    def _():
        for c in _row_tile_copies(buf.at[slot], hbm, (step - 2) * rows, sem.at[slot]):
            c.wait()

    buf[slot] = value
    for c in _row_tile_copies(buf.at[slot], hbm, step * rows, sem.at[slot]):
        c.start()

    @pl.when(step == nsteps - 1)
    def _():
        if nsteps > 1:
            for c in _row_tile_copies(buf.at[1 - slot], hbm, (step - 1) * rows, sem.at[1 - slot]):
                c.wait()
        for c in _row_tile_copies(buf.at[slot], hbm, step * rows, sem.at[slot]):
            c.wait()


def _running_max(x, reverse):
    n = x.shape[0]
    row = lax.broadcasted_iota(jnp.int32, x.shape, 0)
    step = 1
    while step < n:
        if reverse:
            shifted = jnp.where(row < n - step, pltpu.roll(x, n - step, 0), NEG_BIG)
        else:
            shifted = jnp.where(row >= step, pltpu.roll(x, step, 0), NEG_BIG)
        x = jnp.maximum(x, shifted)
        step *= 2
    return x


_P_COL = {0: 0, 1: 1, 2: 2, 4: 3, 5: 4, 6: 5}
_G2_COL = {3: 0, 7: 1}


def _inproj_kernel(x_ref, lg_ref, lb_ref, w_ref, b_ref, wg_ref, bg_ref,
                   cos_ref, sin_ref, h0_ref, p_ref, g2_ref, gc_ref, gr_ref):
    tm = x_ref.shape[0]
    h = _layer_norm(x_ref[...], lg_ref[...], lb_ref[...])
    h0_ref[...] = h
    hb = h.astype(_MXU)
    cos = cos_ref[...]
    sin = sin_ref[...]

    pre = _dot(hb, wg_ref[...]) + bg_ref[...]
    row = lax.broadcasted_iota(jnp.int32, (CHUNK, CHUNK), 0)
    col = lax.broadcasted_iota(jnp.int32, (CHUNK, CHUNK), 1)
    tri_le = (col <= row).astype(_MXU)
    tri_ge = (col >= row).astype(_MXU)
    for c in range(tm // CHUNK):
        sl = slice(c * CHUNK, (c + 1) * CHUNK)
        blk = pre[sl, :]
        ls = _log_sigmoid(blk)
        pref = _dot01_left(tri_le, ls)
        suf = _dot01_left(tri_ge, ls)
        is_cum_f = (col >= 4) & (col < 8)
        is_cum_b = (col >= 12) & (col < 16)
        cum = jnp.where(is_cum_f, pref, jnp.where(is_cum_b, suf, 0.0))
        excess = blk - pltpu.roll(cum, LANES - 4, 1)
        run_max = jnp.where(col < 4, _running_max(excess, False), _running_max(excess, True))
        is_max = ((col >= 16) & (col < 20)) | ((col >= 24) & (col < 28))
        gates = jnp.where(is_cum_f | is_cum_b, cum, jnp.where(is_max, pltpu.roll(run_max, 16, 1), blk))
        gc_ref[sl, :] = gates
        gr_ref[:, sl] = gates.T

    for sec in range(8):
        acc = _dot(hb, w_ref[:, sec * SEC:(sec + 1) * SEC]) + b_ref[:, sec * SEC:(sec + 1) * SEC]
        if sec in (0, 1):
            c0 = _P_COL[sec] * SEC
            for hh in range(N_HEADS):
                s = acc[:, hh * HEAD_DIM:(hh + 1) * HEAD_DIM]
                r = s * cos + pltpu.roll(s, HEAD_DIM // 2, 1) * sin
                if sec == 1:
                    r = r * K_SCALE
                p_ref[:, c0 + hh * HEAD_DIM:c0 + (hh + 1) * HEAD_DIM] = r.astype(p_ref.dtype)
        elif sec in _P_COL:
            if sec == 5:
                acc = acc * K_SCALE
            c0 = _P_COL[sec] * SEC
            p_ref[:, c0:c0 + SEC] = acc.astype(p_ref.dtype)
        else:
            c0 = _G2_COL[sec] * SEC
            g2_ref[:, c0:c0 + SEC] = acc


def _inproj(x2, seq, ln_g, ln_b, w_main, b_main, wg, bg, cos, sin, tm=512):
    T = x2.shape[0]
    nseq = seq // tm
    const = lambda i: (0, 0)
    return pl.pallas_call(
        _inproj_kernel,
        grid=(T // tm,),
        in_specs=[
            pl.BlockSpec((tm, D_MODEL), lambda i: (i, 0)),
            pl.BlockSpec((1, D_MODEL), const),
            pl.BlockSpec((1, D_MODEL), const),
            pl.BlockSpec((D_MODEL, 8 * SEC), const),
            pl.BlockSpec((1, 8 * SEC), const),
            pl.BlockSpec((D_MODEL, LANES), const),
            pl.BlockSpec((1, LANES), const),
            pl.BlockSpec((tm, HEAD_DIM), lambda i: (i % nseq, 0)),
            pl.BlockSpec((tm, HEAD_DIM), lambda i: (i % nseq, 0)),
        ],
        out_specs=[
            pl.BlockSpec((tm, D_MODEL), lambda i: (i, 0)),
            pl.BlockSpec((tm, 6 * SEC), lambda i: (i, 0)),
            pl.BlockSpec((tm, 2 * SEC), lambda i: (i, 0)),
            pl.BlockSpec((tm, LANES), lambda i: (i, 0)),
            pl.BlockSpec((LANES, tm), lambda i: (0, i)),
        ],
        out_shape=[
            jax.ShapeDtypeStruct((T, D_MODEL), _F32),
            jax.ShapeDtypeStruct((T, 6 * SEC), _MXU),
            jax.ShapeDtypeStruct((T, 2 * SEC), _F32),
            jax.ShapeDtypeStruct((T, LANES), _F32),
            jax.ShapeDtypeStruct((LANES, T), _F32),
        ],
        compiler_params=_params(("parallel",)),
        name="inproj",
    )(x2, ln_g, ln_b, w_main, b_main, wg, bg, cos, sin)


def _init_state(s_ref, cn_ref, m_ref):
    s_ref[...] = jnp.zeros(s_ref.shape, _F32)
    cn_ref[...] = jnp.zeros(cn_ref.shape, _F32)
    m_ref[...] = jnp.full(m_ref.shape, NEG_BIG, _F32)


def _dot_tn(a, b):
    return lax.dot_general(a, b, (((0,), (0,)), ((), ())), preferred_element_type=_F32)


def _lane_spread(gc, spread_ref):
    return _dot01_right(gc, spread_ref[...])


def _mlstm_direction(q, k, v, li_b, cum_b, max_b, li_row, cum_row, last_lane, mask, cn_ref, m_ref, si):
    cum_last = cum_row[:, last_lane:last_lane + 1]
    m_prev = m_ref[si, 0:1, :]
    cn_prev = cn_ref[si]
    ones = jnp.ones((CHUNK, HEAD_DIM), _MXU)
    v1 = jnp.concatenate([v, ones], axis=1)

    m_row = cum_b + jnp.maximum(max_b, m_prev)
    log_d = jnp.where(mask, cum_b - cum_row + li_row, NEG_BIG)
    d_w = jnp.exp(log_d - m_row)
    s_inter = jnp.exp(cum_b + m_prev - m_row)
    qk = _dot_nt(q, k) * d_w
    qk_hi = qk.astype(_MXU)
    qk_lo = (qk - qk_hi.astype(_F32)).astype(_MXU)
    intra = _dot(qk_hi, v1)
    inter = _dot(q, cn_prev.astype(_MXU))
    num = intra[:, :HEAD_DIM] + s_inter * inter[:, :HEAD_DIM]
    den = intra[:, HEAD_DIM:] + _dot(qk_lo, ones) + s_inter * inter[:, HEAD_DIM:]
    h_out = num / jnp.maximum(jnp.abs(den), jnp.exp(-m_row))

    a_max = jnp.max(cum_last - cum_row + li_row, axis=1, keepdims=True)
    kw = (k.astype(_F32) * jnp.exp(cum_last - cum_b + li_b - a_max)).astype(_MXU)
    m_new = jnp.maximum(cum_last + m_prev, a_max)
    s_old = jnp.exp(cum_last + m_prev - m_new)
    s_new = jnp.exp(a_max - m_new)
    cn_ref[si] = (jnp.concatenate([s_old, s_old], axis=1) * cn_prev
                  + jnp.concatenate([s_new, s_new], axis=1) * _dot_tn(kw, v1))
    m_ref[si] = jnp.broadcast_to(m_new, (SUBLANES, LANES))
    return h_out


def _retention_state_update(k, v, kw, gl, s_ref, si):
    kwv = (k.astype(_F32) * kw).astype(_MXU)
    s_ref[si] = gl * s_ref[si] + _dot_tn(kwv, v)


SEQ_PER_STEP = 4


def _head_cols(h, base=0):
    return slice(base + h * HEAD_DIM, base + (h + 1) * HEAD_DIM)


def _sweep_bwd_kernel(*refs):
    seq_refs = [refs[8 * k:8 * (k + 1)] for k in range(SEQ_PER_STEP)]
    qwb_ref, kwb_ref, gl_ref, spread_ref, yb_ref, s_ref, cn_ref, m_ref = refs[8 * SEQ_PER_STEP:]

    @pl.when(pl.program_id(1) == 0)
    def _():
        _init_state(s_ref, cn_ref, m_ref)

    row = lax.broadcasted_iota(jnp.int32, (CHUNK, CHUNK), 0)
    col = lax.broadcasted_iota(jnp.int32, (CHUNK, CHUNK), 1)
    mask = col >= row
    pairs = [(h, k) for h in range(N_HEADS) for k in range(SEQ_PER_STEP)]
    for h, k in pairs:
        sl = _head_cols(h)
        qs = (seq_refs[k][0][:, sl].astype(_F32) * qwb_ref[h]).astype(_MXU)
        yb_ref[k, :, sl] = _dot(qs, s_ref[k * N_HEADS + h].astype(_MXU))
    for h, k in pairs:
        sl = _head_cols(h)
        _retention_state_update(seq_refs[k][1][:, sl], seq_refs[k][2][:, sl], kwb_ref[h], gl_ref[h], s_ref,
                                k * N_HEADS + h)
    cols = [_lane_spread(seq_refs[k][6][...], spread_ref) for k in range(SEQ_PER_STEP)]
    for h in range(N_HEADS):
        sl = _head_cols(h)
        for k in range(SEQ_PER_STEP):
            mq_ref, mk_ref, mv_ref, _, gr_ref = seq_refs[k][3:8]
            h_b = _mlstm_direction(mq_ref[:, sl], mk_ref[:, sl], mv_ref[:, sl],
                                   cols[k][:, _head_cols(h)], cols[k][:, _head_cols(h, SEC)],
                                   cols[k][:, _head_cols(h, 2 * SEC)],
                                   gr_ref[8 + h:9 + h, :], gr_ref[12 + h:13 + h, :], 0, mask,
                                   cn_ref, m_ref, k * N_HEADS + h)
            yb_ref[k, :, _head_cols(h, SEC)] = h_b


def _sweep_fwd_kernel(*refs):
    seq_refs = [refs[10 * k:10 * (k + 1)] for k in range(SEQ_PER_STEP)]
    (rng_ref, mng_ref, dsym_ref, qwf_ref, kwf_ref, gl_ref, spread_ref,
     mixed_ref, s_ref, cn_ref, m_ref) = refs[10 * SEQ_PER_STEP:]

    @pl.when(pl.program_id(1) == 0)
    def _():
        _init_state(s_ref, cn_ref, m_ref)

    row = lax.broadcasted_iota(jnp.int32, (CHUNK, CHUNK), 0)
    col = lax.broadcasted_iota(jnp.int32, (CHUNK, CHUNK), 1)
    mask = col <= row

    pairs = [(h, k) for h in range(N_HEADS) for k in range(SEQ_PER_STEP)]

    def head_norms(ys):
        centred = [y - m for y, m in zip(ys, [jnp.mean(y, axis=1, keepdims=True) for y in ys])]
        var = [jnp.mean(c * c, axis=1, keepdims=True) for c in centred]
        return [c * lax.rsqrt(v + LN_EPS) for c, v in zip(centred, var)]

    qkv = [[seq_refs[k][i][:, _head_cols(h)] for i in range(3)] for h, k in pairs]
    scores = [_dot_nt(q, kk) for q, kk, _ in qkv]
    inter = [_dot((q.astype(_F32) * qwf_ref[h]).astype(_MXU), s_ref[k * N_HEADS + h].astype(_MXU))
             for (h, k), (q, _, _) in zip(pairs, qkv)]
    ys = [_dot((s * dsym_ref[h]).astype(_MXU), v) + it + seq_refs[k][8][:, _head_cols(h)]
          for (h, k), (_, _, v), s, it in zip(pairs, qkv, scores, inter)]
    for (h, k), (_, kk, v) in zip(pairs, qkv):
        _retention_state_update(kk, v, kwf_ref[h], gl_ref[h], s_ref, k * N_HEADS + h)
    for (h, k), yn in zip(pairs, head_norms(ys)):
        sl = _head_cols(h)
        g = seq_refs[k][9][:, sl]
        mixed_ref[k, :, sl] = (yn * rng_ref[:, sl] * (g * _sigmoid(g))).astype(mixed_ref.dtype)

    cols = [_lane_spread(seq_refs[k][6][...], spread_ref) for k in range(SEQ_PER_STEP)]
    ys = []
    for h, k in pairs:
        sl = _head_cols(h)
        mq_ref, mk_ref, mv_ref, _, gr_ref, yb_ref = seq_refs[k][3:9]
        h_f = _mlstm_direction(mq_ref[:, sl], mk_ref[:, sl], mv_ref[:, sl],
                               cols[k][:, _head_cols(h)], cols[k][:, _head_cols(h, SEC)],
                               cols[k][:, _head_cols(h, 2 * SEC)],
                               gr_ref[h:h + 1, :], gr_ref[4 + h:5 + h, :], CHUNK - 1, mask,
                               cn_ref, m_ref, k * N_HEADS + h)
        ys.append(h_f + yb_ref[:, _head_cols(h, SEC)])
    for (h, k), yn in zip(pairs, head_norms(ys)):
        sl2 = _head_cols(h, SEC)
        out = yn * mng_ref[:, _head_cols(h)] * _sigmoid(seq_refs[k][9][:, sl2])
        mixed_ref[k, :, sl2] = out.astype(mixed_ref.dtype)


def _state_scratch():
    return [
        pltpu.VMEM((SEQ_PER_STEP * N_HEADS, HEAD_DIM, HEAD_DIM), _F32),
        pltpu.VMEM((SEQ_PER_STEP * N_HEADS, HEAD_DIM, 2 * HEAD_DIM), _F32),
        pltpu.VMEM((SEQ_PER_STEP * N_HEADS, SUBLANES, LANES), _F32),
    ]


_SPREAD_SPEC = pl.BlockSpec((LANES, 3 * SEC), lambda b, n: (0, 0))


def _sweep_specs(nchunk, reverse, n_wide):
    def chunk(n):
        return (nchunk - 1 - n) if reverse else n

    per_seq = []
    for k in range(SEQ_PER_STEP):
        def rb(b, n, k=k):
            return (b * SEQ_PER_STEP + k) * nchunk + chunk(n)
        per_seq += [pl.BlockSpec((CHUNK, SEC), functools.partial(lambda b, n, s, rb: (rb(b, n), s), s=s, rb=rb))
                    for s in range(6)]
        per_seq.append(pl.BlockSpec((CHUNK, LANES), functools.partial(lambda b, n, rb: (rb(b, n), 0), rb=rb)))
        per_seq.append(pl.BlockSpec((LANES, CHUNK), functools.partial(lambda b, n, rb: (0, rb(b, n)), rb=rb)))
        per_seq += [pl.BlockSpec((CHUNK, 2 * SEC), functools.partial(lambda b, n, rb: (rb(b, n), 0), rb=rb))
                    for _ in range(n_wide)]
    out = pl.BlockSpec((SEQ_PER_STEP, CHUNK, 2 * SEC), lambda b, n: (b, chunk(n), 0))
    tab = pl.BlockSpec((N_HEADS, CHUNK, HEAD_DIM), lambda b, n: (0, 0, 0))
    return per_seq, out, tab


def _sweep_bwd(P, GC, GR, qwb, kwb, gl, spread, batch, seq):
    nchunk = seq // CHUNK
    per_seq, out, tab = _sweep_specs(nchunk, True, 0)
    yb = pl.pallas_call(
        _sweep_bwd_kernel,
        grid=(batch // SEQ_PER_STEP, nchunk),
        in_specs=per_seq + [tab, tab, tab, _SPREAD_SPEC],
        out_specs=out,
        out_shape=jax.ShapeDtypeStruct((batch, seq, 2 * SEC), _F32),
        scratch_shapes=_state_scratch(),
        compiler_params=_params(("parallel", "arbitrary")),
        name="sweep_bwd",
    )(*([P] * 6 + [GC, GR]) * SEQ_PER_STEP, qwb, kwb, gl, spread)
    return yb.reshape(batch * seq, 2 * SEC)


def _sweep_fwd(P, G2, GC, GR, YB, rng, mng, dsym, qwf, kwf, gl, spread, batch, seq):
    nchunk = seq // CHUNK
    per_seq, out, tab = _sweep_specs(nchunk, False, 2)
    gain = pl.BlockSpec((1, SEC), lambda b, n: (0, 0))
    mixed = pl.pallas_call(
        _sweep_fwd_kernel,
        grid=(batch // SEQ_PER_STEP, nchunk),
        in_specs=per_seq + [gain, gain, tab, tab, tab, tab, _SPREAD_SPEC],
        out_specs=out,
        out_shape=jax.ShapeDtypeStruct((batch, seq, 2 * SEC), _MXU),
        scratch_shapes=_state_scratch(),
        compiler_params=_params(("parallel", "arbitrary")),
        name="sweep_fwd",
    )(*([P] * 6 + [GC, GR, YB, G2]) * SEQ_PER_STEP, rng, mng, dsym, qwf, kwf, gl, spread)
    return mixed.reshape(batch * seq, 2 * SEC)


def _outproj_kernel(mixed_ref, h0_ref, wo_ref, lg_ref, lb_ref, wr_ref, h1_hbm, aff_ref, hbuf, sem, *, nsteps):
    tm = mixed_ref.shape[0]
    parts = [slice(i * tm // 2, (i + 1) * tm // 2) for i in range(2)]
    zs = [ALPHA * h0_ref[p, :] + _dot(mixed_ref[p, :], wo_ref[...]) for p in parts]
    h1s = [_layer_norm(z, lg_ref[...], lb_ref[...]) for z in zs]
    _pipelined_writeback(hbuf, sem, h1_hbm, jnp.concatenate(h1s, axis=0), pl.program_id(0), nsteps, tm)
    logits = [_dot(h1.astype(_MXU), wr_ref[...]) for h1 in h1s]
    valid = lax.broadcasted_iota(jnp.int32, logits[0].shape, 1) < N_EXPERTS
    logits = [jnp.where(valid, lg, NEG_BIG) for lg in logits]
    es = [jnp.exp(lg - jnp.max(lg, axis=1, keepdims=True)) for lg in logits]
    for p, e in zip(parts, es):
        aff_ref[p, :] = jnp.where(valid, e / jnp.sum(e, axis=1, keepdims=True), 0.0)


def _outproj(mixed, h0, wo, ln_g, ln_b, wr, tm=512):
    T = mixed.shape[0]
    const = lambda i: (0, 0)
    return pl.pallas_call(
        functools.partial(_outproj_kernel, nsteps=T // tm),
        grid=(T // tm,),
        in_specs=[
            pl.BlockSpec((tm, D_MODEL), lambda i: (i, 0)),
            pl.BlockSpec((tm, D_MODEL), lambda i: (i, 0)),
            pl.BlockSpec((D_MODEL, D_MODEL), const),
            pl.BlockSpec((1, D_MODEL), const),
            pl.BlockSpec((1, D_MODEL), const),
            pl.BlockSpec((D_MODEL, LANES), const),
        ],
        out_specs=[
            pl.BlockSpec(memory_space=pl.ANY),
            pl.BlockSpec((tm, LANES), lambda i: (i, 0)),
        ],
        out_shape=[
            jax.ShapeDtypeStruct((T, ROW_TILES, LANES), _F32),
            jax.ShapeDtypeStruct((T, LANES), _F32),
        ],
        scratch_shapes=[pltpu.VMEM((2, tm, D_MODEL), _F32), pltpu.SemaphoreType.DMA((2,))],
        compiler_params=_params(("arbitrary",)),
        name="outproj",
    )(mixed, h0, wo, ln_g, ln_b, wr)


def _thresh_kernel(aff_ref, thr_ref, rem_ref, *, cap):
    rows = aff_ref.shape[0]
    aff = aff_ref[...]

    def count(pred):
        c = jnp.sum(pred.astype(jnp.int32).reshape(rows // SUBLANES, SUBLANES, LANES), axis=0)
        c = jnp.broadcast_to(jnp.sum(c, axis=0, keepdims=True), (SUBLANES, LANES))
        for shift in (64, 32, 16):
            c = c + pltpu.roll(c, shift, 1)
        return c

    def body(i, ans):
        cand = ans | jnp.left_shift(jnp.int32(1), 30 - i)
        c = count(aff >= lax.bitcast_convert_type(cand[0:1, :], _F32))
        return jnp.where(c >= cap, cand, ans)

    ans = lax.fori_loop(0, 31, body, jnp.zeros((SUBLANES, LANES), jnp.int32))
    thr = jnp.where(ans >= MIN_NORMAL_BITS, lax.bitcast_convert_type(ans, _F32), 0.0)
    thr_ref[...] = thr
    rem_ref[...] = cap - count(aff > thr[0:1, :])


def _thresh(affc, cap):
    return pl.pallas_call(
        functools.partial(_thresh_kernel, cap=cap),
        out_shape=[jax.ShapeDtypeStruct((SUBLANES, LANES), _F32),
                   jax.ShapeDtypeStruct((SUBLANES, LANES), jnp.int32)],
        compiler_params=pltpu.CompilerParams(vmem_limit_bytes=VMEM_LIMIT),
        name="thresh",
    )(affc)


SELECT_BLOCKS = 4
SELECT_ROWS = SELECT_BLOCKS * TOK_BLOCK


def _select_kernel(aff_ref, thr_ref, rem_ref, spread_ref, before_ref, gsel_ref, lidx_ref, cnt_ref, off_ref,
                   nsel_ref, neq_ref):
    @pl.when(pl.program_id(0) == 0)
    def _():
        nsel_ref[...] = jnp.zeros(nsel_ref.shape, _F32)
        neq_ref[...] = jnp.zeros(neq_ref.shape, _F32)

    aff = aff_ref[...]
    thr = thr_ref[0:1, :]
    rem = rem_ref[0:1, :].astype(_F32)
    valid = lax.broadcasted_iota(jnp.int32, aff.shape, 1) < N_EXPERTS
    before = before_ref[...]
    gt = (aff > thr) & valid
    eq = (aff == thr) & valid
    eq_before = _dot(before, eq.astype(_MXU)) + neq_ref[0:1, :]
    sel = gt | (eq & (eq_before < rem))
    pos = _dot(before, sel.astype(_MXU))
    neq_ref[0:1, :] = neq_ref[0:1, :] + jnp.sum(eq.astype(_F32), axis=0, keepdims=True)
    gsel_ref[...] = jnp.where(sel, aff, 0.0)

    slot = (lax.broadcasted_iota(jnp.int32, (TOK_BLOCK, N_EXPERTS * TOK_BLOCK), 1) % TOK_BLOCK).astype(_F32)
    tok = lax.broadcasted_iota(jnp.int32, (TOK_BLOCK, N_EXPERTS * TOK_BLOCK), 0)
    start = nsel_ref[0:1, :]
    taken = start
    for q in range(SELECT_BLOCKS):
        rows = slice(q * TOK_BLOCK, (q + 1) * TOK_BLOCK)
        sel_q = sel[rows, :]
        off_ref[q] = taken.astype(jnp.int32)
        cnt = jnp.sum(sel_q.astype(_F32), axis=0, keepdims=True)
        cnt_ref[q] = cnt.astype(jnp.int32)
        ranked = jnp.where(sel_q, pos[rows, :] - (taken - start), -1.0).astype(_MXU)
        spread = _dot(ranked, spread_ref[...])
        tok0 = (pl.program_id(0) * SELECT_BLOCKS + q) * TOK_BLOCK
        lidx_ref[q] = jnp.sum(jnp.where(spread == slot, tok + tok0, 0), axis=0, keepdims=True)
        taken = taken + cnt
    nsel_ref[0:1, :] = taken


def _select(aff, thr, rem, spread, before):
    T = aff.shape[0]
    nb = T // TOK_BLOCK
    const = lambda b: (0, 0)
    return pl.pallas_call(
        _select_kernel,
        grid=(nb // SELECT_BLOCKS,),
        in_specs=[
            pl.BlockSpec((SELECT_ROWS, LANES), lambda b: (b, 0)),
            pl.BlockSpec((SUBLANES, LANES), const),
            pl.BlockSpec((SUBLANES, LANES), const),
            pl.BlockSpec((LANES, N_EXPERTS * TOK_BLOCK), const),
            pl.BlockSpec((SELECT_ROWS, SELECT_ROWS), const),
        ],
        out_specs=[
            pl.BlockSpec((SELECT_ROWS, LANES), lambda b: (b, 0)),
            pl.BlockSpec((SELECT_BLOCKS, 1, N_EXPERTS * TOK_BLOCK), lambda b: (b, 0, 0)),
            pl.BlockSpec((SELECT_BLOCKS, 1, LANES), lambda b: (b, 0, 0)),
            pl.BlockSpec((SELECT_BLOCKS, 1, LANES), lambda b: (b, 0, 0)),
        ],
        out_shape=[
            jax.ShapeDtypeStruct((T, LANES), _F32),
            jax.ShapeDtypeStruct((nb, 1, N_EXPERTS * TOK_BLOCK), jnp.int32),
            jax.ShapeDtypeStruct((nb, 1, LANES), jnp.int32),
            jax.ShapeDtypeStruct((nb, 1, LANES), jnp.int32),
        ],
        scratch_shapes=[pltpu.VMEM((SUBLANES, LANES), _F32), pltpu.VMEM((SUBLANES, LANES), _F32)],
        compiler_params=_params(("arbitrary",)),
        name="select",
    )(aff, thr, rem, spread, before)


FFN_ROWS = 512


def _ffn_kernel(cnt_ref, lidx_hbm, h1_hbm, wg_ref, wu_ref, wd_ref, y_hbm,
                xbuf, ybuf, lidx_smem, walk_ref, sem_idx, sem_rows, sem_out, *, tm, per, nb):
    e = pl.program_id(0)
    j = pl.program_id(1)
    step = e * per + j
    slot = step % 2

    per_expert = nb * TOK_BLOCK

    def idx_copy(en):
        return pltpu.make_async_copy(lidx_hbm.at[en], lidx_smem.at[pl.ds((en % 2) * per_expert, per_expert)],
                                     sem_idx.at[en % 2])

    def issue_tile(en, dst_slot):
        list_base = (en % 2) * per_expert
        group0 = dst_slot * (tm // SUBLANES)
        sem = sem_rows.at[dst_slot]

        def cond(st):
            return st[0] < tm

        def body(st):
            n, b, r = st
            c = cnt_ref[en * nb + b]
            take = jnp.minimum(c - r, tm - n)
            src_minus_dst = list_base + b * TOK_BLOCK + r - n

            def one(m, queue):
                t = lidx_smem[src_minus_dst + m]
                pltpu.make_async_copy(h1_hbm.at[t], _row_view(xbuf, m, group0), sem).start(priority=queue)

            _for_each(n, n + take, one)
            done = r + take >= c
            return n + take, jnp.where(done, b + 1, b), jnp.where(done, 0, r + take)

        _, b, r = lax.while_loop(cond, body, (jnp.int32(0), walk_ref[0], walk_ref[1]))
        walk_ref[0] = b
        walk_ref[1] = r

    def restart_walk():
        walk_ref[0] = jnp.int32(0)
        walk_ref[1] = jnp.int32(0)

    @pl.when(step == 0)
    def _():
        idx_copy(0).start()
        idx_copy(0).wait()
        restart_walk()
        issue_tile(0, 0)

    @pl.when((j == 0) & (e + 1 < N_EXPERTS))
    def _():
        idx_copy(e + 1).start()

    @pl.when(j + 1 < per)
    def _():
        issue_tile(e, 1 - slot)

    @pl.when((j + 1 == per) & (e + 1 < N_EXPERTS))
    def _():
        idx_copy(e + 1).wait()
        restart_walk()
        issue_tile(e + 1, 1 - slot)

    pltpu.make_async_copy(h1_hbm.at[pl.ds(0, tm)], h1_hbm.at[pl.ds(0, tm)], sem_rows.at[slot]).wait()
    x = _matrix_value(xbuf.at[pl.ds(slot * (tm // SUBLANES), tm // SUBLANES)]).astype(_MXU)
    g = _dot(x, wg_ref[0])
    u = _dot(x, wu_ref[0])
    hid = (g * _sigmoid(g) * u).astype(_MXU)
    _pipelined_writeback(ybuf, sem_out, y_hbm, _dot(hid, wd_ref[0]), step, N_EXPERTS * per, tm)


def _ffn(cnt_e, lidx_e, h1, wg, wu, wd, cap, tm):
    per = cap // tm
    nb = lidx_e.shape[1] // TOK_BLOCK
    return pl.pallas_call(
        functools.partial(_ffn_kernel, tm=tm, per=per, nb=nb),
        grid_spec=pltpu.PrefetchScalarGridSpec(
            num_scalar_prefetch=1,
            grid=(N_EXPERTS, per),
            in_specs=[
                pl.BlockSpec(memory_space=pl.ANY),
                pl.BlockSpec(memory_space=pl.ANY),
                pl.BlockSpec((1, D_MODEL, D_FF), lambda e, j, *_: (e, 0, 0)),
                pl.BlockSpec((1, D_MODEL, D_FF), lambda e, j, *_: (e, 0, 0)),
                pl.BlockSpec((1, D_FF, D_MODEL), lambda e, j, *_: (e, 0, 0)),
            ],
            out_specs=pl.BlockSpec(memory_space=pl.ANY),
            scratch_shapes=[
                pltpu.VMEM((2 * tm // SUBLANES, ROW_TILES, SUBLANES, LANES), _F32),
                pltpu.VMEM((2, tm, D_MODEL), _F32),
                pltpu.SMEM((2 * nb * TOK_BLOCK,), jnp.int32),
                pltpu.SMEM((2,), jnp.int32),
                pltpu.SemaphoreType.DMA((2,)),
                pltpu.SemaphoreType.DMA((2,)),
                pltpu.SemaphoreType.DMA((2,)),
            ],
        ),
        out_shape=jax.ShapeDtypeStruct((N_EXPERTS * cap, ROW_TILES, LANES), _F32),
        compiler_params=_params(("arbitrary", "arbitrary")),
        name="ffn",
    )(cnt_e, lidx_e, h1, wg, wu, wd)


SLOT_GROUPS = TOK_BLOCK // SUBLANES
COMBINE_ROWS = 32


def _combine_kernel(cnt_ref, off_ref, lidx_hbm, ye_hbm, h1_hbm, gsel_ref, lg_ref, lb_ref, y_ref,
                    slots_ref, hres_ref, lidx_smem, sem_idx, sem_rows, sem_res, *, cap, nb):
    b = pl.program_id(0)
    slot = b % 2

    per_block = N_EXPERTS * TOK_BLOCK

    def idx_copy(bn):
        return pltpu.make_async_copy(lidx_hbm.at[bn], lidx_smem.at[pl.ds((bn % 2) * per_block, per_block)],
                                     sem_idx.at[bn % 2])

    def res_copies(bn):
        return _matrix_copies(h1_hbm, bn * TOK_BLOCK, hres_ref.at[bn % 2], sem_res.at[bn % 2])

    def issue_block(bn):
        par = bn % 2
        sem = sem_rows.at[par]
        tok0 = bn * TOK_BLOCK
        for c in res_copies(bn):
            c.start()
        for e in range(N_EXPERTS):
            c = cnt_ref[bn * N_EXPERTS + e]
            base = e * cap + off_ref[bn * N_EXPERTS + e]
            list_minus_src = par * per_block + e * TOK_BLOCK - base
            group0 = (par * N_EXPERTS + e) * SLOT_GROUPS

            def one(src_row, queue):
                t = lidx_smem[list_minus_src + src_row] - tok0
                pltpu.make_async_copy(ye_hbm.at[src_row], _row_view(slots_ref, t, group0),
                                      sem).start(priority=queue)

            _for_each(base, base + c, one)

    @pl.when(b == 0)
    def _():
        slots_ref[...] = jnp.zeros(slots_ref.shape, _F32)
        idx_copy(0).start()
        idx_copy(0).wait()
        issue_block(0)
        if nb > 1:
            idx_copy(1).start()

    @pl.when(b + 1 < nb)
    def _():
        idx_copy(b + 1).wait()
        issue_block(b + 1)

    @pl.when(b + 2 < nb)
    def _():
        idx_copy(b + 2).start()

    for e in range(N_EXPERTS):
        c = cnt_ref[b * N_EXPERTS + e]

        @pl.when(c > 0)
        def _():
            pltpu.make_async_copy(ye_hbm.at[pl.ds(0, c)], ye_hbm.at[pl.ds(0, c)], sem_rows.at[slot]).wait()

    for c in res_copies(b):
        c.wait()

    groups = COMBINE_ROWS // SUBLANES
    for tg in range(TOK_BLOCK // COMBINE_ROWS):
        rows = slice(tg * COMBINE_ROWS, (tg + 1) * COMBINE_ROWS)
        acc = [ALPHA * hres_ref[slot, rows, j * LANES:(j + 1) * LANES] for j in range(ROW_TILES)]
        for e in range(N_EXPERTS):
            gate = jnp.broadcast_to(gsel_ref[rows, e:e + 1], (COMBINE_ROWS, LANES))
            blk = slots_ref[pl.ds((slot * N_EXPERTS + e) * SLOT_GROUPS + tg * groups, groups)]
            for j in range(ROW_TILES):
                acc[j] = acc[j] + gate * blk[:, j].reshape(COMBINE_ROWS, LANES)
        for j in range(ROW_TILES):
            y_ref[rows, j * LANES:(j + 1) * LANES] = acc[j]
    y_ref[...] = _layer_norm(y_ref[...], lg_ref[...], lb_ref[...])


def _combine(cnt_t, off_t, lidx, ye, gsel, h1, ln_g, ln_b, cap):
    T = h1.shape[0]
    nb = T // TOK_BLOCK
    return pl.pallas_call(
        functools.partial(_combine_kernel, cap=cap, nb=nb),
        grid_spec=pltpu.PrefetchScalarGridSpec(
            num_scalar_prefetch=2,
            grid=(nb,),
            in_specs=[
                pl.BlockSpec(memory_space=pl.ANY),
                pl.BlockSpec(memory_space=pl.ANY),
                pl.BlockSpec(memory_space=pl.ANY),
                pl.BlockSpec((TOK_BLOCK, LANES), lambda b, *_: (b, 0)),
                pl.BlockSpec((1, D_MODEL), lambda b, *_: (0, 0)),
                pl.BlockSpec((1, D_MODEL), lambda b, *_: (0, 0)),
            ],
            out_specs=pl.BlockSpec((TOK_BLOCK, D_MODEL), lambda b, *_: (b, 0)),
            scratch_shapes=[
                pltpu.VMEM((2 * N_EXPERTS * SLOT_GROUPS, ROW_TILES, SUBLANES, LANES), _F32),
                pltpu.VMEM((2, TOK_BLOCK, D_MODEL), _F32),
                pltpu.SMEM((2 * N_EXPERTS * TOK_BLOCK,), jnp.int32),
                pltpu.SemaphoreType.DMA((2,)),
                pltpu.SemaphoreType.DMA((2,)),
                pltpu.SemaphoreType.DMA((2,)),
            ],
        ),
        out_shape=jax.ShapeDtypeStruct((T, D_MODEL), _F32),
        compiler_params=_params(("arbitrary",)),
        name="combine",
    )(cnt_t, off_t, lidx, ye, h1, gsel, ln_g, ln_b)


def _tables(seq):
    half = HEAD_DIM // 2
    inv = 1.0 / (ROPE_BASE ** (jnp.arange(half, dtype=_F32) / half))
    ang = jnp.arange(seq, dtype=_F32)[:, None] * inv[None, :]
    cos = jnp.concatenate([jnp.cos(ang), jnp.cos(ang)], axis=1)
    sin = jnp.concatenate([-jnp.sin(ang), jnp.sin(ang)], axis=1)
    log_g = jnp.log1p(-jnp.exp2(-5.0 - jnp.arange(N_HEADS, dtype=_F32)))[:, None, None]
    pos = jnp.arange(CHUNK, dtype=_F32)
    rows = lambda f: jnp.broadcast_to(jnp.exp(log_g * f[None, :, None]), (N_HEADS, CHUNK, HEAD_DIM))
    dsym = jnp.exp(log_g * jnp.abs(pos[:, None] - pos[None, :])[None])
    tabs = dict(
        cos=cos, sin=sin, dsym=dsym,
        qwf=rows(pos + 1.0), kwf=rows(CHUNK - 1.0 - pos),
        qwb=rows(CHUNK - pos), kwb=rows(pos),
        gl=rows(jnp.full((CHUNK,), float(CHUNK), _F32)),
    )
    spread = lambda chans: (jnp.arange(LANES)[:, None] == jnp.repeat(jnp.asarray(chans), LANES)[None, :]).astype(_MXU)
    tabs["spread"] = spread(range(N_EXPERTS))
    r = jnp.arange(SELECT_ROWS)
    tabs["before"] = (r[None, :] < r[:, None]).astype(_MXU)
    tabs["spread_f"] = spread([0, 1, 2, 3, 4, 5, 6, 7, 16, 17, 18, 19])
    tabs["spread_b"] = spread([8, 9, 10, 11, 12, 13, 14, 15, 24, 25, 26, 27])
    return tabs


def _trunk(x, w):
    batch, seq, _ = x.shape
    T = batch * seq
    nb = T // TOK_BLOCK
    cap = CAP_FACTOR * T // N_EXPERTS
    t = _tables(seq)
    h0, P, G2, GC, GR = _inproj(x.reshape(T, D_MODEL), seq, w["ln_in_g"], w["ln_in_b"], w["w_main"],
                                w["b_main"], w["wg"], w["bg"], t["cos"], t["sin"])
    YB = _sweep_bwd(P, GC, GR, t["qwb"], t["kwb"], t["gl"], t["spread_b"], batch, seq)
    mixed = _sweep_fwd(P, G2, GC, GR, YB, w["ret_g"], w["mlstm_g"], t["dsym"], t["qwf"], t["kwf"], t["gl"],
                       t["spread_f"], batch, seq)
    h1, aff = _outproj(mixed, h0, w["w_o"], w["ln1_g"], w["ln1_b"], w["w_r"])
    affc = aff[:, :N_EXPERTS].reshape(T // SUBLANES, LANES)
    thr, rem = _thresh(affc, cap)
    gsel, lidx, cnt, off = _select(aff, thr, rem, t["spread"], t["before"])
    cnt2 = cnt.reshape(nb, LANES)[:, :N_EXPERTS]
    off2 = off.reshape(nb, LANES)[:, :N_EXPERTS]
    lidx_e = lidx.reshape(nb, N_EXPERTS, TOK_BLOCK).transpose(1, 0, 2).reshape(N_EXPERTS, nb * TOK_BLOCK)
    ye = _ffn(cnt2.T.reshape(-1), lidx_e, h1, w["w_gate"], w["w_up"], w["w_down"], cap, min(FFN_ROWS, cap))
    y = _combine(cnt2.reshape(-1), off2.reshape(-1), lidx.reshape(nb, N_EXPERTS * TOK_BLOCK), ye, gsel, h1,
                 w["ln2_g"], w["ln2_b"], cap)
    return y.reshape(batch, seq, D_MODEL)


def _prep_weights(ln_in_g, ln_in_b, w_in, b_in, ret_norm_g, mlstm_norm_g, w_o, ln1_g, ln1_b, w_router,
                  w_gate, w_up, w_down, ln2_g, ln2_b):
    main = 8 * SEC
    ngate = 4 * N_HEADS
    row = lambda v: v.reshape(1, -1).astype(_F32)
    wg = jnp.pad(w_in[0][:, main:main + ngate], ((0, 0), (0, LANES - ngate)))
    bg = jnp.pad(b_in[0][main:main + ngate], (0, LANES - ngate))
    return dict(
        ln_in_g=row(ln_in_g), ln_in_b=row(ln_in_b),
        w_main=w_in[0][:, :main].astype(_MXU), b_main=row(b_in[0][:main]),
        wg=wg.astype(_MXU), bg=row(bg),
        ret_g=row(ret_norm_g[0]), mlstm_g=row(mlstm_norm_g[0]),
        w_o=w_o[0].astype(_MXU), ln1_g=row(ln1_g[0]), ln1_b=row(ln1_b[0]),
        w_r=jnp.pad(w_router[0], ((0, 0), (0, LANES - N_EXPERTS))).astype(_MXU),
        w_gate=w_gate[0].astype(_MXU), w_up=w_up[0].astype(_MXU), w_down=w_down[0].astype(_MXU),
        ln2_g=row(ln2_g[0]), ln2_b=row(ln2_b[0]),
    )


def kernel(x_prompt, x_sample, ln_in_g, ln_in_b, w_in, b_in, ret_norm_g, mlstm_norm_g, w_o, ln1_g, ln1_b,
           w_router, w_gate, w_up, w_down, ln2_g, ln2_b):
    w = _prep_weights(ln_in_g, ln_in_b, w_in, b_in, ret_norm_g, mlstm_norm_g, w_o, ln1_g, ln1_b, w_router,
                      w_gate, w_up, w_down, ln2_g, ln2_b)
    return (_trunk(x_prompt, w), _trunk(x_sample, w))
```

```python
import functools

import jax
import jax.numpy as jnp
from jax import lax
from jax.experimental import pallas as pl
from jax.experimental.pallas import tpu as pltpu

D_MODEL = 1024
N_HEADS = 4
HEAD_DIM = 128
SEC = N_HEADS * HEAD_DIM
CHUNK = 128
N_EXPERTS = 16
D_FF = 2 * D_MODEL
CAP_FACTOR = 2
ROPE_BASE = 10000.0
LN_EPS = 1e-5
NEG_BIG = -1e30
DEPTH = 1
ALPHA = (2.0 * DEPTH) ** 0.25
K_SCALE = HEAD_DIM ** -0.5
LANES = 128
SUBLANES = 8
SUBLANE_BITS = 3
TOK_BLOCK = 128
PROJ_ROWS = 512
MIN_NORMAL_BITS = 0x00800000
VMEM_LIMIT = 56 * 1024 * 1024

CH_LI_F, CH_CUM_F = 0, 4
CH_LI_B, CH_CUM_B = 8, 12
CH_MAX_F, CH_MAX_B = 16, 24

_MXU = jnp.bfloat16
_F32 = jnp.float32


def _dot(a, b):
    return jnp.dot(a, b, preferred_element_type=_F32)


def _dot_nt(a, b):
    return lax.dot_general(a, b, (((1,), (1,)), ((), ())), preferred_element_type=_F32)


def _split3(x):
    x1 = x.astype(_MXU)
    r1 = x - x1.astype(_F32)
    x2 = r1.astype(_MXU)
    r2 = r1 - x2.astype(_F32)
    return x1, x2, r2.astype(_MXU)


def _dot01_left(a01, x):
    x1, x2, x3 = _split3(x)
    return _dot(a01, x1) + _dot(a01, x2) + _dot(a01, x3)


def _dot01_right(x, a01):
    x1, x2, x3 = _split3(x)
    return _dot(x1, a01) + _dot(x2, a01) + _dot(x3, a01)


def _layer_norm(x, g, b):
    mu = jnp.mean(x, axis=-1, keepdims=True)
    xc = x - mu
    var = jnp.mean(xc * xc, axis=-1, keepdims=True)
    return xc * lax.rsqrt(var + LN_EPS) * g + b


def _log_sigmoid(x):
    return jnp.minimum(x, 0.0) - jnp.log1p(jnp.exp(-jnp.abs(x)))


def _sigmoid(x):
    return 1.0 / (1.0 + jnp.exp(-x))


def _params(sem):
    return pltpu.CompilerParams(dimension_semantics=sem, vmem_limit_bytes=VMEM_LIMIT)


ROW_TILES = D_MODEL // LANES


ISSUE_UNROLL = 4


def _for_each(lo, hi, body):
    nblk = lax.shift_right_logical(hi - lo, ISSUE_UNROLL.bit_length() - 1)

    def block(k, carry):
        for u in range(ISSUE_UNROLL):
            body(lo + k * ISSUE_UNROLL + u)
        return carry

    def single(i, carry):
        body(i)
        return carry

    lax.fori_loop(0, nblk, block, 0)
    lax.fori_loop(lo + nblk * ISSUE_UNROLL, hi, single, 0)


def _row_view(buf, r, group0=0):
    return buf.at[group0 + lax.shift_right_logical(r, SUBLANE_BITS), :, jnp.bitwise_and(r, SUBLANES - 1), :]


def _matrix_value(buf):
    rows = buf.shape[0] * SUBLANES
    return jnp.concatenate([buf[:, j].reshape(rows, LANES) for j in range(ROW_TILES)], axis=1)


def _row_tile_copies(mat, hbm, row0, sem):
    rows = mat.shape[0]
    return [pltpu.make_async_copy(mat.at[:, pl.ds(j * LANES, LANES)], hbm.at[pl.ds(row0, rows), j, :], sem)
            for j in range(ROW_TILES)]


def _matrix_copies(hbm, row0, mat, sem):
    rows = mat.shape[0]
    return [pltpu.make_async_copy(hbm.at[pl.ds(row0, rows), j, :], mat.at[:, pl.ds(j * LANES, LANES)], sem)
            for j in range(ROW_TILES)]


def _pipelined_writeback(buf, sem, hbm, value, step, nsteps, rows):
    slot = step % 2

    @pl.when(step >= 2)
    def _():
        for c in _row_tile_copies(buf.at[slot], hbm, (step - 2) * rows, sem.at[slot]):
            c.wait()

    buf[slot] = value
    for c in _row_tile_copies(buf.at[slot], hbm, step * rows, sem.at[slot]):
        c.start()

    @pl.when(step == nsteps - 1)
    def _():
        if nsteps > 1:
            for c in _row_tile_copies(buf.at[1 - slot], hbm, (step - 1) * rows, sem.at[1 - slot]):
                c.wait()
        for c in _row_tile_copies(buf.at[slot], hbm, step * rows, sem.at[slot]):
            c.wait()


def _running_max(x, reverse):
    n = x.shape[0]
    row = lax.broadcasted_iota(jnp.int32, x.shape, 0)
    step = 1
    while step < n:
        if reverse:
            shifted = jnp.where(row < n - step, pltpu.roll(x, n - step, 0), NEG_BIG)
        else:
            shifted = jnp.where(row >= step, pltpu.roll(x, step, 0), NEG_BIG)
        x = jnp.maximum(x, shifted)
        step *= 2
    return x


_P_COL = {0: 0, 1: 1, 2: 2, 4: 3, 5: 4, 6: 5}
_G2_COL = {3: 0, 7: 1}


def _inproj_kernel(x_ref, lg_ref, lb_ref, w_ref, b_ref, wg_ref, bg_ref,
                   cos_ref, sin_ref, h0_ref, p_ref, g2_ref, gc_ref, gr_ref):
    tm = x_ref.shape[0]
    h = _layer_norm(x_ref[...], lg_ref[...], lb_ref[...])
    h0_ref[...] = h
    hb = h.astype(_MXU)
    cos = cos_ref[...]
    sin = sin_ref[...]

    pre = _dot(hb, wg_ref[...]) + bg_ref[...]
    row = lax.broadcasted_iota(jnp.int32, (CHUNK, CHUNK), 0)
    col = lax.broadcasted_iota(jnp.int32, (CHUNK, CHUNK), 1)
    tri_le = (col <= row).astype(_MXU)
    tri_ge = (col >= row).astype(_MXU)
    for c in range(tm // CHUNK):
        sl = slice(c * CHUNK, (c + 1) * CHUNK)
        blk = pre[sl, :]
        ls = _log_sigmoid(blk)
        pref = _dot01_left(tri_le, ls)
        suf = _dot01_left(tri_ge, ls)
        is_cum_f = (col >= CH_CUM_F) & (col < CH_CUM_F + N_HEADS)
        is_cum_b = (col >= CH_CUM_B) & (col < CH_CUM_B + N_HEADS)
        cum = jnp.where(is_cum_f, pref, jnp.where(is_cum_b, suf, 0.0))
        excess = blk - pltpu.roll(cum, LANES - (CH_CUM_F - CH_LI_F), 1)
        run_max = jnp.where(col < CH_LI_F + N_HEADS, _running_max(excess, False), _running_max(excess, True))
        is_max = (((col >= CH_MAX_F) & (col < CH_MAX_F + N_HEADS))
                  | ((col >= CH_MAX_B) & (col < CH_MAX_B + N_HEADS)))
        gates = jnp.where(is_cum_f | is_cum_b, cum,
                          jnp.where(is_max, pltpu.roll(run_max, CH_MAX_F - CH_LI_F, 1), blk))
        gc_ref[sl, :] = gates
        gr_ref[:, sl] = gates.T

    for sec in range(8):
        acc = _dot(hb, w_ref[:, sec * SEC:(sec + 1) * SEC]) + b_ref[:, sec * SEC:(sec + 1) * SEC]
        if sec in (0, 1):
            c0 = _P_COL[sec] * SEC
            for hh in range(N_HEADS):
                s = acc[:, hh * HEAD_DIM:(hh + 1) * HEAD_DIM]
                r = s * cos + pltpu.roll(s, HEAD_DIM // 2, 1) * sin
                if sec == 1:
                    r = r * K_SCALE
                p_ref[:, c0 + hh * HEAD_DIM:c0 + (hh + 1) * HEAD_DIM] = r.astype(p_ref.dtype)
        elif sec in _P_COL:
            if sec == 5:
                acc = acc * K_SCALE
            c0 = _P_COL[sec] * SEC
            p_ref[:, c0:c0 + SEC] = acc.astype(p_ref.dtype)
        else:
            c0 = _G2_COL[sec] * SEC
            g2_ref[:, c0:c0 + SEC] = acc


def _inproj(x2, seq, ln_g, ln_b, w_main, b_main, wg, bg, cos, sin, tm=PROJ_ROWS):
    T = x2.shape[0]
    nseq = seq // tm
    const = lambda i: (0, 0)
    return pl.pallas_call(
        _inproj_kernel,
        grid=(T // tm,),
        in_specs=[
            pl.BlockSpec((tm, D_MODEL), lambda i: (i, 0)),
            pl.BlockSpec((1, D_MODEL), const),
            pl.BlockSpec((1, D_MODEL), const),
            pl.BlockSpec((D_MODEL, 8 * SEC), const),
            pl.BlockSpec((1, 8 * SEC), const),
            pl.BlockSpec((D_MODEL, LANES), const),
            pl.BlockSpec((1, LANES), const),
            pl.BlockSpec((tm, HEAD_DIM), lambda i: (i % nseq, 0)),
            pl.BlockSpec((tm, HEAD_DIM), lambda i: (i % nseq, 0)),
        ],
        out_specs=[
            pl.BlockSpec((tm, D_MODEL), lambda i: (i, 0)),
            pl.BlockSpec((tm, 6 * SEC), lambda i: (i, 0)),
            pl.BlockSpec((tm, 2 * SEC), lambda i: (i, 0)),
            pl.BlockSpec((tm, LANES), lambda i: (i, 0)),
            pl.BlockSpec((LANES, tm), lambda i: (0, i)),
        ],
        out_shape=[
            jax.ShapeDtypeStruct((T, D_MODEL), _F32),
            jax.ShapeDtypeStruct((T, 6 * SEC), _MXU),
            jax.ShapeDtypeStruct((T, 2 * SEC), _F32),
            jax.ShapeDtypeStruct((T, LANES), _F32),
            jax.ShapeDtypeStruct((LANES, T), _F32),
        ],
        compiler_params=_params(("parallel",)),
        name="inproj",
    )(x2, ln_g, ln_b, w_main, b_main, wg, bg, cos, sin)


def _init_state(s_ref, cn_ref, m_ref):
    s_ref[...] = jnp.zeros(s_ref.shape, _F32)
    cn_ref[...] = jnp.zeros(cn_ref.shape, _F32)
    m_ref[...] = jnp.full(m_ref.shape, NEG_BIG, _F32)


def _dot_tn(a, b):
    return lax.dot_general(a, b, (((0,), (0,)), ((), ())), preferred_element_type=_F32)


def _lane_spread(gc, spread_ref):
    return _dot01_right(gc, spread_ref[...])


def _mlstm_direction(q, k, v, li_b, cum_b, max_b, li_row, cum_row, last_lane, mask, cn_ref, m_ref, si):
    cum_last = cum_row[:, last_lane:last_lane + 1]
    m_prev = m_ref[si, 0:1, :]
    cn_prev = cn_ref[si]
    ones = jnp.ones((CHUNK, HEAD_DIM), _MXU)
    v1 = jnp.concatenate([v, ones], axis=1)

    m_row = cum_b + jnp.maximum(max_b, m_prev)
    log_d = jnp.where(mask, cum_b - cum_row + li_row, NEG_BIG)
    d_w = jnp.exp(log_d - m_row)
    s_inter = jnp.exp(cum_b + m_prev - m_row)
    qk = _dot_nt(q, k) * d_w
    qk_hi = qk.astype(_MXU)
    qk_lo = (qk - qk_hi.astype(_F32)).astype(_MXU)
    intra = _dot(qk_hi, v1)
    inter = _dot(q, cn_prev.astype(_MXU))
    num = intra[:, :HEAD_DIM] + s_inter * inter[:, :HEAD_DIM]
    den = intra[:, HEAD_DIM:] + _dot(qk_lo, ones) + s_inter * inter[:, HEAD_DIM:]
    h_out = num / jnp.maximum(jnp.abs(den), jnp.exp(-m_row))

    a_max = jnp.max(cum_last - cum_row + li_row, axis=1, keepdims=True)
    kw = (k.astype(_F32) * jnp.exp(cum_last - cum_b + li_b - a_max)).astype(_MXU)
    m_new = jnp.maximum(cum_last + m_prev, a_max)
    s_old = jnp.exp(cum_last + m_prev - m_new)
    s_new = jnp.exp(a_max - m_new)
    cn_ref[si] = (jnp.concatenate([s_old, s_old], axis=1) * cn_prev
                  + jnp.concatenate([s_new, s_new], axis=1) * _dot_tn(kw, v1))
    m_ref[si] = jnp.broadcast_to(m_new, (SUBLANES, LANES))
    return h_out


def _retention_state_update(k, v, kw, gl, s_ref, si):
    kwv = (k.astype(_F32) * kw).astype(_MXU)
    s_ref[si] = gl * s_ref[si] + _dot_tn(kwv, v)


SEQ_PER_STEP = 4


def _head_cols(h, base=0):
    return slice(base + h * HEAD_DIM, base + (h + 1) * HEAD_DIM)


def _sweep_bwd_kernel(*refs):
    seq_refs = [refs[8 * k:8 * (k + 1)] for k in range(SEQ_PER_STEP)]
    qwb_ref, kwb_ref, gl_ref, spread_ref, yb_ref, s_ref, cn_ref, m_ref = refs[8 * SEQ_PER_STEP:]

    @pl.when(pl.program_id(1) == 0)
    def _():
        _init_state(s_ref, cn_ref, m_ref)

    row = lax.broadcasted_iota(jnp.int32, (CHUNK, CHUNK), 0)
    col = lax.broadcasted_iota(jnp.int32, (CHUNK, CHUNK), 1)
    mask = col >= row
    pairs = [(h, k) for h in range(N_HEADS) for k in range(SEQ_PER_STEP)]
    for h, k in pairs:
        sl = _head_cols(h)
        qs = (seq_refs[k][0][:, sl].astype(_F32) * qwb_ref[h]).astype(_MXU)
        yb_ref[k, :, sl] = _dot(qs, s_ref[k * N_HEADS + h].astype(_MXU))
    for h, k in pairs:
        sl = _head_cols(h)
        _retention_state_update(seq_refs[k][1][:, sl], seq_refs[k][2][:, sl], kwb_ref[h], gl_ref[h], s_ref,
                                k * N_HEADS + h)
    cols = [_lane_spread(seq_refs[k][6][...], spread_ref) for k in range(SEQ_PER_STEP)]
    for h in range(N_HEADS):
        sl = _head_cols(h)
        for k in range(SEQ_PER_STEP):
            mq_ref, mk_ref, mv_ref, _, gr_ref = seq_refs[k][3:8]
            h_b = _mlstm_direction(mq_ref[:, sl], mk_ref[:, sl], mv_ref[:, sl],
                                   cols[k][:, _head_cols(h)], cols[k][:, _head_cols(h, SEC)],
                                   cols[k][:, _head_cols(h, 2 * SEC)],
                                   gr_ref[CH_LI_B + h:CH_LI_B + h + 1, :],
                                   gr_ref[CH_CUM_B + h:CH_CUM_B + h + 1, :], 0, mask,
                                   cn_ref, m_ref, k * N_HEADS + h)
            yb_ref[k, :, _head_cols(h, SEC)] = h_b


def _sweep_fwd_kernel(*refs):
    seq_refs = [refs[10 * k:10 * (k + 1)] for k in range(SEQ_PER_STEP)]
    (rng_ref, mng_ref, dsym_ref, qwf_ref, kwf_ref, gl_ref, spread_ref,
     mixed_ref, s_ref, cn_ref, m_ref) = refs[10 * SEQ_PER_STEP:]

    @pl.when(pl.program_id(1) == 0)
    def _():
        _init_state(s_ref, cn_ref, m_ref)

    row = lax.broadcasted_iota(jnp.int32, (CHUNK, CHUNK), 0)
    col = lax.broadcasted_iota(jnp.int32, (CHUNK, CHUNK), 1)
    mask = col <= row

    pairs = [(h, k) for h in range(N_HEADS) for k in range(SEQ_PER_STEP)]

    def head_norms(ys):
        centred = [y - m for y, m in zip(ys, [jnp.mean(y, axis=1, keepdims=True) for y in ys])]
        var = [jnp.mean(c * c, axis=1, keepdims=True) for c in centred]
        return [c * lax.rsqrt(v + LN_EPS) for c, v in zip(centred, var)]

    qkv = [[seq_refs[k][i][:, _head_cols(h)] for i in range(3)] for h, k in pairs]
    scores = [_dot_nt(q, kk) for q, kk, _ in qkv]
    inter = [_dot((q.astype(_F32) * qwf_ref[h]).astype(_MXU), s_ref[k * N_HEADS + h].astype(_MXU))
             for (h, k), (q, _, _) in zip(pairs, qkv)]
    ys = [_dot((s * dsym_ref[h]).astype(_MXU), v) + it + seq_refs[k][8][:, _head_cols(h)]
          for (h, k), (_, _, v), s, it in zip(pairs, qkv, scores, inter)]
    for (h, k), (_, kk, v) in zip(pairs, qkv):
        _retention_state_update(kk, v, kwf_ref[h], gl_ref[h], s_ref, k * N_HEADS + h)
    for (h, k), yn in zip(pairs, head_norms(ys)):
        sl = _head_cols(h)
        g = seq_refs[k][9][:, sl]
        mixed_ref[k, :, sl] = (yn * rng_ref[:, sl] * (g * _sigmoid(g))).astype(mixed_ref.dtype)

    cols = [_lane_spread(seq_refs[k][6][...], spread_ref) for k in range(SEQ_PER_STEP)]
    ys = []
    for h, k in pairs:
        sl = _head_cols(h)
        mq_ref, mk_ref, mv_ref, _, gr_ref, yb_ref = seq_refs[k][3:9]
        h_f = _mlstm_direction(mq_ref[:, sl], mk_ref[:, sl], mv_ref[:, sl],
                               cols[k][:, _head_cols(h)], cols[k][:, _head_cols(h, SEC)],
                               cols[k][:, _head_cols(h, 2 * SEC)],
                               gr_ref[CH_LI_F + h:CH_LI_F + h + 1, :],
                               gr_ref[CH_CUM_F + h:CH_CUM_F + h + 1, :], CHUNK - 1, mask,
                               cn_ref, m_ref, k * N_HEADS + h)
        ys.append(h_f + yb_ref[:, _head_cols(h, SEC)])
    for (h, k), yn in zip(pairs, head_norms(ys)):
        sl2 = _head_cols(h, SEC)
        out = yn * mng_ref[:, _head_cols(h)] * _sigmoid(seq_refs[k][9][:, sl2])
        mixed_ref[k, :, sl2] = out.astype(mixed_ref.dtype)


def _state_scratch():
    return [
        pltpu.VMEM((SEQ_PER_STEP * N_HEADS, HEAD_DIM, HEAD_DIM), _F32),
        pltpu.VMEM((SEQ_PER_STEP * N_HEADS, HEAD_DIM, 2 * HEAD_DIM), _F32),
        pltpu.VMEM((SEQ_PER_STEP * N_HEADS, SUBLANES, LANES), _F32),
    ]


_SPREAD_SPEC = pl.BlockSpec((LANES, 3 * SEC), lambda b, n: (0, 0))


def _sweep_specs(nchunk, reverse, n_wide):
    def chunk(n):
        return (nchunk - 1 - n) if reverse else n

    per_seq = []
    for k in range(SEQ_PER_STEP):
        def rb(b, n, k=k):
            return (b * SEQ_PER_STEP + k) * nchunk + chunk(n)
        per_seq += [pl.BlockSpec((CHUNK, SEC), functools.partial(lambda b, n, s, rb: (rb(b, n), s), s=s, rb=rb))
                    for s in range(6)]
        per_seq.append(pl.BlockSpec((CHUNK, LANES), functools.partial(lambda b, n, rb: (rb(b, n), 0), rb=rb)))
        per_seq.append(pl.BlockSpec((LANES, CHUNK), functools.partial(lambda b, n, rb: (0, rb(b, n)), rb=rb)))
        per_seq += [pl.BlockSpec((CHUNK, 2 * SEC), functools.partial(lambda b, n, rb: (rb(b, n), 0), rb=rb))
                    for _ in range(n_wide)]
    out = pl.BlockSpec((SEQ_PER_STEP, CHUNK, 2 * SEC), lambda b, n: (b, chunk(n), 0))
    tab = pl.BlockSpec((N_HEADS, CHUNK, HEAD_DIM), lambda b, n: (0, 0, 0))
    return per_seq, out, tab


def _sweep_bwd(P, GC, GR, qwb, kwb, gl, spread, batch, seq):
    nchunk = seq // CHUNK
    per_seq, out, tab = _sweep_specs(nchunk, True, 0)
    yb = pl.pallas_call(
        _sweep_bwd_kernel,
        grid=(batch // SEQ_PER_STEP, nchunk),
        in_specs=per_seq + [tab, tab, tab, _SPREAD_SPEC],
        out_specs=out,
        out_shape=jax.ShapeDtypeStruct((batch, seq, 2 * SEC), _F32),
        scratch_shapes=_state_scratch(),
        compiler_params=_params(("parallel", "arbitrary")),
        name="sweep_bwd",
    )(*([P] * 6 + [GC, GR]) * SEQ_PER_STEP, qwb, kwb, gl, spread)
    return yb.reshape(batch * seq, 2 * SEC)


def _sweep_fwd(P, G2, GC, GR, YB, rng, mng, dsym, qwf, kwf, gl, spread, batch, seq):
    nchunk = seq // CHUNK
    per_seq, out, tab = _sweep_specs(nchunk, False, 2)
    gain = pl.BlockSpec((1, SEC), lambda b, n: (0, 0))
    mixed = pl.pallas_call(
        _sweep_fwd_kernel,
        grid=(batch // SEQ_PER_STEP, nchunk),
        in_specs=per_seq + [gain, gain, tab, tab, tab, tab, _SPREAD_SPEC],
        out_specs=out,
        out_shape=jax.ShapeDtypeStruct((batch, seq, 2 * SEC), _MXU),
        scratch_shapes=_state_scratch(),
        compiler_params=_params(("parallel", "arbitrary")),
        name="sweep_fwd",
    )(*([P] * 6 + [GC, GR, YB, G2]) * SEQ_PER_STEP, rng, mng, dsym, qwf, kwf, gl, spread)
    return mixed.reshape(batch * seq, 2 * SEC)


def _outproj_kernel(mixed_ref, h0_ref, wo_ref, lg_ref, lb_ref, wr_ref, h1_hbm, aff_ref, hbuf, sem, *, nsteps):
    tm = mixed_ref.shape[0]
    parts = [slice(i * tm // 2, (i + 1) * tm // 2) for i in range(2)]
    zs = [ALPHA * h0_ref[p, :] + _dot(mixed_ref[p, :], wo_ref[...]) for p in parts]
    h1s = [_layer_norm(z, lg_ref[...], lb_ref[...]) for z in zs]
    _pipelined_writeback(hbuf, sem, h1_hbm, jnp.concatenate(h1s, axis=0), pl.program_id(0), nsteps, tm)
    logits = [_dot(h1.astype(_MXU), wr_ref[...]) for h1 in h1s]
    valid = lax.broadcasted_iota(jnp.int32, logits[0].shape, 1) < N_EXPERTS
    logits = [jnp.where(valid, lg, NEG_BIG) for lg in logits]
    es = [jnp.exp(lg - jnp.max(lg, axis=1, keepdims=True)) for lg in logits]
    for p, e in zip(parts, es):
        aff_ref[p, :] = jnp.where(valid, e / jnp.sum(e, axis=1, keepdims=True), 0.0)


def _outproj(mixed, h0, wo, ln_g, ln_b, wr, tm=PROJ_ROWS):
    T = mixed.shape[0]
    const = lambda i: (0, 0)
    return pl.pallas_call(
        functools.partial(_outproj_kernel, nsteps=T // tm),
        grid=(T // tm,),
        in_specs=[
            pl.BlockSpec((tm, D_MODEL), lambda i: (i, 0)),
            pl.BlockSpec((tm, D_MODEL), lambda i: (i, 0)),
            pl.BlockSpec((D_MODEL, D_MODEL), const),
            pl.BlockSpec((1, D_MODEL), const),
            pl.BlockSpec((1, D_MODEL), const),
            pl.BlockSpec((D_MODEL, LANES), const),
        ],
        out_specs=[
            pl.BlockSpec(memory_space=pl.ANY),
            pl.BlockSpec((tm, LANES), lambda i: (i, 0)),
        ],
        out_shape=[
            jax.ShapeDtypeStruct((T, ROW_TILES, LANES), _F32),
            jax.ShapeDtypeStruct((T, LANES), _F32),
        ],
        scratch_shapes=[pltpu.VMEM((2, tm, D_MODEL), _F32), pltpu.SemaphoreType.DMA((2,))],
        compiler_params=_params(("arbitrary",)),
        name="outproj",
    )(mixed, h0, wo, ln_g, ln_b, wr)


def _thresh_kernel(aff_ref, thr_ref, rem_ref, *, cap):
    rows = aff_ref.shape[0]
    aff = aff_ref[...]

    def count(pred):
        c = jnp.sum(pred.astype(jnp.int32).reshape(rows // SUBLANES, SUBLANES, LANES), axis=0)
        c = jnp.broadcast_to(jnp.sum(c, axis=0, keepdims=True), (SUBLANES, LANES))
        shift = LANES // 2
        while shift >= N_EXPERTS:
            c = c + pltpu.roll(c, shift, 1)
            shift //= 2
        return c

    def body(i, ans):
        cand = ans | jnp.left_shift(jnp.int32(1), 30 - i)
        c = count(aff >= lax.bitcast_convert_type(cand[0:1, :], _F32))
        return jnp.where(c >= cap, cand, ans)

    ans = lax.fori_loop(0, 31, body, jnp.zeros((SUBLANES, LANES), jnp.int32))
    thr = jnp.where(ans >= MIN_NORMAL_BITS, lax.bitcast_convert_type(ans, _F32), 0.0)
    thr_ref[...] = thr
    rem_ref[...] = cap - count(aff > thr[0:1, :])


def _thresh(affc, cap):
    return pl.pallas_call(
        functools.partial(_thresh_kernel, cap=cap),
        out_shape=[jax.ShapeDtypeStruct((SUBLANES, LANES), _F32),
                   jax.ShapeDtypeStruct((SUBLANES, LANES), jnp.int32)],
        compiler_params=pltpu.CompilerParams(vmem_limit_bytes=VMEM_LIMIT),
        name="thresh",
    )(affc)


SELECT_BLOCKS = 4
SELECT_ROWS = SELECT_BLOCKS * TOK_BLOCK


def _select_kernel(aff_ref, thr_ref, rem_ref, spread_ref, before_ref, gsel_ref, lidx_ref, cnt_ref, off_ref,
                   nsel_ref, neq_ref):
    @pl.when(pl.program_id(0) == 0)
    def _():
        nsel_ref[...] = jnp.zeros(nsel_ref.shape, _F32)
        neq_ref[...] = jnp.zeros(neq_ref.shape, _F32)

    aff = aff_ref[...]
    thr = thr_ref[0:1, :]
    rem = rem_ref[0:1, :].astype(_F32)
    valid = lax.broadcasted_iota(jnp.int32, aff.shape, 1) < N_EXPERTS
    before = before_ref[...]
    gt = (aff > thr) & valid
    eq = (aff == thr) & valid
    eq_before = _dot(before, eq.astype(_MXU)) + neq_ref[0:1, :]
    sel = gt | (eq & (eq_before < rem))
    pos = _dot(before, sel.astype(_MXU))
    neq_ref[0:1, :] = neq_ref[0:1, :] + jnp.sum(eq.astype(_F32), axis=0, keepdims=True)
    gsel_ref[...] = jnp.where(sel, aff, 0.0)

    slot = (lax.broadcasted_iota(jnp.int32, (TOK_BLOCK, N_EXPERTS * TOK_BLOCK), 1) % TOK_BLOCK).astype(_F32)
    tok = lax.broadcasted_iota(jnp.int32, (TOK_BLOCK, N_EXPERTS * TOK_BLOCK), 0)
    start = nsel_ref[0:1, :]
    taken = start
    for q in range(SELECT_BLOCKS):
        rows = slice(q * TOK_BLOCK, (q + 1) * TOK_BLOCK)
        sel_q = sel[rows, :]
        off_ref[q] = taken.astype(jnp.int32)
        cnt = jnp.sum(sel_q.astype(_F32), axis=0, keepdims=True)
        cnt_ref[q] = cnt.astype(jnp.int32)
        ranked = jnp.where(sel_q, pos[rows, :] - (taken - start), -1.0).astype(_MXU)
        spread = _dot(ranked, spread_ref[...])
        tok0 = (pl.program_id(0) * SELECT_BLOCKS + q) * TOK_BLOCK
        lidx_ref[q] = jnp.sum(jnp.where(spread == slot, tok + tok0, 0), axis=0, keepdims=True)
        taken = taken + cnt
    nsel_ref[0:1, :] = taken


def _select(aff, thr, rem, spread, before):
    T = aff.shape[0]
    nb = T // TOK_BLOCK
    const = lambda b: (0, 0)
    return pl.pallas_call(
        _select_kernel,
        grid=(nb // SELECT_BLOCKS,),
        in_specs=[
            pl.BlockSpec((SELECT_ROWS, LANES), lambda b: (b, 0)),
            pl.BlockSpec((SUBLANES, LANES), const),
            pl.BlockSpec((SUBLANES, LANES), const),
            pl.BlockSpec((LANES, N_EXPERTS * TOK_BLOCK), const),
            pl.BlockSpec((SELECT_ROWS, SELECT_ROWS), const),
        ],
        out_specs=[
            pl.BlockSpec((SELECT_ROWS, LANES), lambda b: (b, 0)),
            pl.BlockSpec((SELECT_BLOCKS, 1, N_EXPERTS * TOK_BLOCK), lambda b: (b, 0, 0)),
            pl.BlockSpec((SELECT_BLOCKS, 1, LANES), lambda b: (b, 0, 0)),
            pl.BlockSpec((SELECT_BLOCKS, 1, LANES), lambda b: (b, 0, 0)),
        ],
        out_shape=[
            jax.ShapeDtypeStruct((T, LANES), _F32),
            jax.ShapeDtypeStruct((nb, 1, N_EXPERTS * TOK_BLOCK), jnp.int32),
            jax.ShapeDtypeStruct((nb, 1, LANES), jnp.int32),
            jax.ShapeDtypeStruct((nb, 1, LANES), jnp.int32),
        ],
        scratch_shapes=[pltpu.VMEM((SUBLANES, LANES), _F32), pltpu.VMEM((SUBLANES, LANES), _F32)],
        compiler_params=_params(("arbitrary",)),
        name="select",
    )(aff, thr, rem, spread, before)


FFN_ROWS = 512


def _ffn_kernel(cnt_ref, lidx_hbm, h1_hbm, wg_ref, wu_ref, wd_ref, y_hbm,
                xbuf, ybuf, lidx_smem, walk_ref, sem_idx, sem_rows, sem_out, *, tm, per, nb):
    e = pl.program_id(0)
    j = pl.program_id(1)
    step = e * per + j
    slot = step % 2

    per_expert = nb * TOK_BLOCK

    def idx_copy(en):
        return pltpu.make_async_copy(lidx_hbm.at[en], lidx_smem.at[pl.ds((en % 2) * per_expert, per_expert)],
                                     sem_idx.at[en % 2])

    def issue_tile(en, dst_slot):
        list_base = (en % 2) * per_expert
        group0 = dst_slot * (tm // SUBLANES)
        sem = sem_rows.at[dst_slot]

        def cond(st):
            return st[0] < tm

        def body(st):
            n, b, r = st
            c = cnt_ref[en * nb + b]
            take = jnp.minimum(c - r, tm - n)
            src_minus_dst = list_base + b * TOK_BLOCK + r - n

            def one(m):
                t = lidx_smem[src_minus_dst + m]
                pltpu.make_async_copy(h1_hbm.at[t], _row_view(xbuf, m, group0), sem).start()

            _for_each(n, n + take, one)
            done = r + take >= c
            return n + take, jnp.where(done, b + 1, b), jnp.where(done, 0, r + take)

        _, b, r = lax.while_loop(cond, body, (jnp.int32(0), walk_ref[0], walk_ref[1]))
        walk_ref[0] = b
        walk_ref[1] = r

    def restart_walk():
        walk_ref[0] = jnp.int32(0)
        walk_ref[1] = jnp.int32(0)

    @pl.when(step == 0)
    def _():
        idx_copy(0).start()
        idx_copy(0).wait()
        restart_walk()
        issue_tile(0, 0)

    @pl.when((j == 0) & (e + 1 < N_EXPERTS))
    def _():
        idx_copy(e + 1).start()

    @pl.when(j + 1 < per)
    def _():
        issue_tile(e, 1 - slot)

    @pl.when((j + 1 == per) & (e + 1 < N_EXPERTS))
    def _():
        idx_copy(e + 1).wait()
        restart_walk()
        issue_tile(e + 1, 1 - slot)

    pltpu.make_async_copy(h1_hbm.at[pl.ds(0, tm)], h1_hbm.at[pl.ds(0, tm)], sem_rows.at[slot]).wait()
    x = _matrix_value(xbuf.at[pl.ds(slot * (tm // SUBLANES), tm // SUBLANES)]).astype(_MXU)
    g = _dot(x, wg_ref[0])
    u = _dot(x, wu_ref[0])
    hid = (g * _sigmoid(g) * u).astype(_MXU)
    _pipelined_writeback(ybuf, sem_out, y_hbm, _dot(hid, wd_ref[0]), step, N_EXPERTS * per, tm)


def _ffn(cnt_e, lidx_e, h1, wg, wu, wd, cap, tm):
    per = cap // tm
    nb = lidx_e.shape[1] // TOK_BLOCK
    return pl.pallas_call(
        functools.partial(_ffn_kernel, tm=tm, per=per, nb=nb),
        grid_spec=pltpu.PrefetchScalarGridSpec(
            num_scalar_prefetch=1,
            grid=(N_EXPERTS, per),
            in_specs=[
                pl.BlockSpec(memory_space=pl.ANY),
                pl.BlockSpec(memory_space=pl.ANY),
                pl.BlockSpec((1, D_MODEL, D_FF), lambda e, j, *_: (e, 0, 0)),
                pl.BlockSpec((1, D_MODEL, D_FF), lambda e, j, *_: (e, 0, 0)),
                pl.BlockSpec((1, D_FF, D_MODEL), lambda e, j, *_: (e, 0, 0)),
            ],
            out_specs=pl.BlockSpec(memory_space=pl.ANY),
            scratch_shapes=[
                pltpu.VMEM((2 * tm // SUBLANES, ROW_TILES, SUBLANES, LANES), _F32),
                pltpu.VMEM((2, tm, D_MODEL), _F32),
                pltpu.SMEM((2 * nb * TOK_BLOCK,), jnp.int32),
                pltpu.SMEM((2,), jnp.int32),
                pltpu.SemaphoreType.DMA((2,)),
                pltpu.SemaphoreType.DMA((2,)),
                pltpu.SemaphoreType.DMA((2,)),
            ],
        ),
        out_shape=jax.ShapeDtypeStruct((N_EXPERTS * cap, ROW_TILES, LANES), _F32),
        compiler_params=_params(("arbitrary", "arbitrary")),
        name="ffn",
    )(cnt_e, lidx_e, h1, wg, wu, wd)


SLOT_GROUPS = TOK_BLOCK // SUBLANES
COMBINE_ROWS = 32


def _combine_kernel(cnt_ref, off_ref, lidx_hbm, ye_hbm, h1_hbm, gsel_ref, lg_ref, lb_ref, y_ref,
                    slots_ref, hres_ref, lidx_smem, sem_idx, sem_rows, sem_res, *, cap, nb):
    b = pl.program_id(0)
    slot = b % 2

    per_block = N_EXPERTS * TOK_BLOCK

    def idx_copy(bn):
        return pltpu.make_async_copy(lidx_hbm.at[bn], lidx_smem.at[pl.ds((bn % 2) * per_block, per_block)],
                                     sem_idx.at[bn % 2])

    def res_copies(bn):
        return _matrix_copies(h1_hbm, bn * TOK_BLOCK, hres_ref.at[bn % 2], sem_res.at[bn % 2])

    def issue_block(bn):
        par = bn % 2
        sem = sem_rows.at[par]
        tok0 = bn * TOK_BLOCK
        for c in res_copies(bn):
            c.start()
        for e in range(N_EXPERTS):
            c = cnt_ref[bn * N_EXPERTS + e]
            base = e * cap + off_ref[bn * N_EXPERTS + e]
            list_minus_src = par * per_block + e * TOK_BLOCK - base
            group0 = (par * N_EXPERTS + e) * SLOT_GROUPS

            def one(src_row):
                t = lidx_smem[list_minus_src + src_row] - tok0
                pltpu.make_async_copy(ye_hbm.at[src_row], _row_view(slots_ref, t, group0), sem).start()

            _for_each(base, base + c, one)

    @pl.when(b == 0)
    def _():
        slots_ref[...] = jnp.zeros(slots_ref.shape, _F32)
        idx_copy(0).start()
        idx_copy(0).wait()
        issue_block(0)
        if nb > 1:
            idx_copy(1).start()

    @pl.when(b + 1 < nb)
    def _():
        idx_copy(b + 1).wait()
        issue_block(b + 1)

    @pl.when(b + 2 < nb)
    def _():
        idx_copy(b + 2).start()

    for e in range(N_EXPERTS):
        c = cnt_ref[b * N_EXPERTS + e]

        @pl.when(c > 0)
        def _():
            pltpu.make_async_copy(ye_hbm.at[pl.ds(0, c)], ye_hbm.at[pl.ds(0, c)], sem_rows.at[slot]).wait()

    for c in res_copies(b):
        c.wait()

    groups = COMBINE_ROWS // SUBLANES
    for tg in range(TOK_BLOCK // COMBINE_ROWS):
        rows = slice(tg * COMBINE_ROWS, (tg + 1) * COMBINE_ROWS)
        acc = [ALPHA * hres_ref[slot, rows, j * LANES:(j + 1) * LANES] for j in range(ROW_TILES)]
        for e in range(N_EXPERTS):
            gate = jnp.broadcast_to(gsel_ref[rows, e:e + 1], (COMBINE_ROWS, LANES))
            blk = slots_ref[pl.ds((slot * N_EXPERTS + e) * SLOT_GROUPS + tg * groups, groups)]
            for j in range(ROW_TILES):
                acc[j] = acc[j] + gate * blk[:, j].reshape(COMBINE_ROWS, LANES)
        for j in range(ROW_TILES):
            y_ref[rows, j * LANES:(j + 1) * LANES] = acc[j]
    y_ref[...] = _layer_norm(y_ref[...], lg_ref[...], lb_ref[...])


def _combine(cnt_t, off_t, lidx, ye, gsel, h1, ln_g, ln_b, cap):
    T = h1.shape[0]
    nb = T // TOK_BLOCK
    return pl.pallas_call(
        functools.partial(_combine_kernel, cap=cap, nb=nb),
        grid_spec=pltpu.PrefetchScalarGridSpec(
            num_scalar_prefetch=2,
            grid=(nb,),
            in_specs=[
                pl.BlockSpec(memory_space=pl.ANY),
                pl.BlockSpec(memory_space=pl.ANY),
                pl.BlockSpec(memory_space=pl.ANY),
                pl.BlockSpec((TOK_BLOCK, LANES), lambda b, *_: (b, 0)),
                pl.BlockSpec((1, D_MODEL), lambda b, *_: (0, 0)),
                pl.BlockSpec((1, D_MODEL), lambda b, *_: (0, 0)),
            ],
            out_specs=pl.BlockSpec((TOK_BLOCK, D_MODEL), lambda b, *_: (b, 0)),
            scratch_shapes=[
                pltpu.VMEM((2 * N_EXPERTS * SLOT_GROUPS, ROW_TILES, SUBLANES, LANES), _F32),
                pltpu.VMEM((2, TOK_BLOCK, D_MODEL), _F32),
                pltpu.SMEM((2 * N_EXPERTS * TOK_BLOCK,), jnp.int32),
                pltpu.SemaphoreType.DMA((2,)),
                pltpu.SemaphoreType.DMA((2,)),
                pltpu.SemaphoreType.DMA((2,)),
            ],
        ),
        out_shape=jax.ShapeDtypeStruct((T, D_MODEL), _F32),
        compiler_params=_params(("arbitrary",)),
        name="combine",
    )(cnt_t, off_t, lidx, ye, h1, gsel, ln_g, ln_b)


def _tables(seq):
    half = HEAD_DIM // 2
    inv = 1.0 / (ROPE_BASE ** (jnp.arange(half, dtype=_F32) / half))
    ang = jnp.arange(seq, dtype=_F32)[:, None] * inv[None, :]
    cos = jnp.concatenate([jnp.cos(ang), jnp.cos(ang)], axis=1)
    sin = jnp.concatenate([-jnp.sin(ang), jnp.sin(ang)], axis=1)
    log_g = jnp.log1p(-jnp.exp2(-5.0 - jnp.arange(N_HEADS, dtype=_F32)))[:, None, None]
    pos = jnp.arange(CHUNK, dtype=_F32)
    rows = lambda f: jnp.broadcast_to(jnp.exp(log_g * f[None, :, None]), (N_HEADS, CHUNK, HEAD_DIM))
    dsym = jnp.exp(log_g * jnp.abs(pos[:, None] - pos[None, :])[None])
    tabs = dict(
        cos=cos, sin=sin, dsym=dsym,
        qwf=rows(pos + 1.0), kwf=rows(CHUNK - 1.0 - pos),
        qwb=rows(CHUNK - pos), kwb=rows(pos),
        gl=rows(jnp.full((CHUNK,), float(CHUNK), _F32)),
    )
    spread = lambda chans: (jnp.arange(LANES)[:, None] == jnp.repeat(jnp.asarray(chans), LANES)[None, :]).astype(_MXU)
    tabs["spread"] = spread(range(N_EXPERTS))
    r = jnp.arange(SELECT_ROWS)
    tabs["before"] = (r[None, :] < r[:, None]).astype(_MXU)
    heads = lambda ch: list(range(ch, ch + N_HEADS))
    tabs["spread_f"] = spread(heads(CH_LI_F) + heads(CH_CUM_F) + heads(CH_MAX_F))
    tabs["spread_b"] = spread(heads(CH_LI_B) + heads(CH_CUM_B) + heads(CH_MAX_B))
    return tabs


def _trunk(x, w):
    batch, seq, _ = x.shape
    T = batch * seq
    nb = T // TOK_BLOCK
    cap = CAP_FACTOR * T // N_EXPERTS
    t = _tables(seq)
    h0, P, G2, GC, GR = _inproj(x.reshape(T, D_MODEL), seq, w["ln_in_g"], w["ln_in_b"], w["w_main"],
                                w["b_main"], w["wg"], w["bg"], t["cos"], t["sin"])
    YB = _sweep_bwd(P, GC, GR, t["qwb"], t["kwb"], t["gl"], t["spread_b"], batch, seq)
    mixed = _sweep_fwd(P, G2, GC, GR, YB, w["ret_g"], w["mlstm_g"], t["dsym"], t["qwf"], t["kwf"], t["gl"],
                       t["spread_f"], batch, seq)
    h1, aff = _outproj(mixed, h0, w["w_o"], w["ln1_g"], w["ln1_b"], w["w_r"])
    affc = aff[:, :N_EXPERTS].reshape(T // SUBLANES, LANES)
    thr, rem = _thresh(affc, cap)
    gsel, lidx, cnt, off = _select(aff, thr, rem, t["spread"], t["before"])
    cnt2 = cnt.reshape(nb, LANES)[:, :N_EXPERTS]
    off2 = off.reshape(nb, LANES)[:, :N_EXPERTS]
    lidx_e = lidx.reshape(nb, N_EXPERTS, TOK_BLOCK).transpose(1, 0, 2).reshape(N_EXPERTS, nb * TOK_BLOCK)
    ye = _ffn(cnt2.T.reshape(-1), lidx_e, h1, w["w_gate"], w["w_up"], w["w_down"], cap, min(FFN_ROWS, cap))
    y = _combine(cnt2.reshape(-1), off2.reshape(-1), lidx.reshape(nb, N_EXPERTS * TOK_BLOCK), ye, gsel, h1,
                 w["ln2_g"], w["ln2_b"], cap)
    return y.reshape(batch, seq, D_MODEL)


def _prep_weights(ln_in_g, ln_in_b, w_in, b_in, ret_norm_g, mlstm_norm_g, w_o, ln1_g, ln1_b, w_router,
                  w_gate, w_up, w_down, ln2_g, ln2_b):
    main = 8 * SEC
    ngate = 4 * N_HEADS
    row = lambda v: v.reshape(1, -1).astype(_F32)
    wg = jnp.pad(w_in[0][:, main:main + ngate], ((0, 0), (0, LANES - ngate)))
    bg = jnp.pad(b_in[0][main:main + ngate], (0, LANES - ngate))
    return dict(
        ln_in_g=row(ln_in_g), ln_in_b=row(ln_in_b),
        w_main=w_in[0][:, :main].astype(_MXU), b_main=row(b_in[0][:main]),
        wg=wg.astype(_MXU), bg=row(bg),
        ret_g=row(ret_norm_g[0]), mlstm_g=row(mlstm_norm_g[0]),
        w_o=w_o[0].astype(_MXU), ln1_g=row(ln1_g[0]), ln1_b=row(ln1_b[0]),
        w_r=jnp.pad(w_router[0], ((0, 0), (0, LANES - N_EXPERTS))).astype(_MXU),
        w_gate=w_gate[0].astype(_MXU), w_up=w_up[0].astype(_MXU), w_down=w_down[0].astype(_MXU),
        ln2_g=row(ln2_g[0]), ln2_b=row(ln2_b[0]),
    )


def kernel(x_prompt, x_sample, ln_in_g, ln_in_b, w_in, b_in, ret_norm_g, mlstm_norm_g, w_o, ln1_g, ln1_b,
           w_router, w_gate, w_up, w_down, ln2_g, ln2_b):
    w = _prep_weights(ln_in_g, ln_in_b, w_in, b_in, ret_norm_g, mlstm_norm_g, w_o, ln1_g, ln1_b, w_router,
                      w_gate, w_up, w_down, ln2_g, ln2_b)
    return (_trunk(x_prompt, w), _trunk(x_sample, w))
```

```python
import functools

import jax
import jax.numpy as jnp
from jax import lax
from jax.experimental import pallas as pl
from jax.experimental.pallas import tpu as pltpu

D_MODEL = 1024
N_HEADS = 4
HEAD_DIM = 128
SEC = N_HEADS * HEAD_DIM
CHUNK = 128
N_EXPERTS = 16
D_FF = 2 * D_MODEL
CAP_FACTOR = 2
ROPE_BASE = 10000.0
LN_EPS = 1e-5
NEG_BIG = -1e30
DEPTH = 1
ALPHA = (2.0 * DEPTH) ** 0.25
K_SCALE = HEAD_DIM ** -0.5
LANES = 128
SUBLANES = 8
SUBLANE_BITS = 3
TOK_BLOCK = 128
PROJ_ROWS = 512
MIN_NORMAL_BITS = 0x00800000
VMEM_LIMIT = 56 * 1024 * 1024

CH_LI_F, CH_CUM_F = 0, 4
CH_LI_B, CH_CUM_B = 8, 12
CH_MAX_F, CH_MAX_B = 16, 24

_MXU = jnp.bfloat16
_F32 = jnp.float32


def _dot(a, b):
    return jnp.dot(a, b, preferred_element_type=_F32)


def _dot_nt(a, b):
    return lax.dot_general(a, b, (((1,), (1,)), ((), ())), preferred_element_type=_F32)


def _split3(x):
    x1 = x.astype(_MXU)
    r1 = x - x1.astype(_F32)
    x2 = r1.astype(_MXU)
    r2 = r1 - x2.astype(_F32)
    return x1, x2, r2.astype(_MXU)


def _dot01_left(a01, x):
    x1, x2, x3 = _split3(x)
    return _dot(a01, x1) + _dot(a01, x2) + _dot(a01, x3)


def _dot01_right(x, a01):
    x1, x2, x3 = _split3(x)
    return _dot(x1, a01) + _dot(x2, a01) + _dot(x3, a01)


def _layer_norm(x, g, b):
    mu = jnp.mean(x, axis=-1, keepdims=True)
    xc = x - mu
    var = jnp.mean(xc * xc, axis=-1, keepdims=True)
    return xc * lax.rsqrt(var + LN_EPS) * g + b


def _log_sigmoid(x):
    return jnp.minimum(x, 0.0) - jnp.log1p(jnp.exp(-jnp.abs(x)))


def _sigmoid(x):
    return 1.0 / (1.0 + jnp.exp(-x))


def _params(sem):
    return pltpu.CompilerParams(dimension_semantics=sem, vmem_limit_bytes=VMEM_LIMIT)


ROW_TILES = D_MODEL // LANES


ISSUE_UNROLL = 4


def _for_each(lo, hi, body):
    nblk = lax.shift_right_logical(hi - lo, ISSUE_UNROLL.bit_length() - 1)

    def block(k, carry):
        for u in range(ISSUE_UNROLL):
            body(lo + k * ISSUE_UNROLL + u)
        return carry

    def single(i, carry):
        body(i)
        return carry

    lax.fori_loop(0, nblk, block, 0)
    lax.fori_loop(lo + nblk * ISSUE_UNROLL, hi, single, 0)


def _row_view(buf, r, group0=0):
    return buf.at[group0 + lax.shift_right_logical(r, SUBLANE_BITS), :, jnp.bitwise_and(r, SUBLANES - 1), :]


def _matrix_value(buf):
    rows = buf.shape[0] * SUBLANES
    return jnp.concatenate([buf[:, j].reshape(rows, LANES) for j in range(ROW_TILES)], axis=1)


def _row_tile_copies(mat, hbm, row0, sem):
    rows = mat.shape[0]
    return [pltpu.make_async_copy(mat.at[:, pl.ds(j * LANES, LANES)], hbm.at[pl.ds(row0, rows), j, :], sem)
            for j in range(ROW_TILES)]


def _matrix_copies(hbm, row0, mat, sem):
    rows = mat.shape[0]
    return [pltpu.make_async_copy(hbm.at[pl.ds(row0, rows), j, :], mat.at[:, pl.ds(j * LANES, LANES)], sem)
            for j in range(ROW_TILES)]


def _pipelined_writeback(buf, sem, hbm, value, step, nsteps, rows):
    slot = step % 2

    @pl.when(step >= 2)
    def _():
        for c in _row_tile_copies(buf.at[slot], hbm, (step - 2) * rows, sem.at[slot]):
            c.wait()

    buf[slot] = value
    for c in _row_tile_copies(buf.at[slot], hbm, step * rows, sem.at[slot]):
        c.start()

    @pl.when(step == nsteps - 1)
    def _():
        if nsteps > 1:
            for c in _row_tile_copies(buf.at[1 - slot], hbm, (step - 1) * rows, sem.at[1 - slot]):
                c.wait()
        for c in _row_tile_copies(buf.at[slot], hbm, step * rows, sem.at[slot]):
            c.wait()


def _running_max(x, reverse):
    n = x.shape[0]
    row = lax.broadcasted_iota(jnp.int32, x.shape, 0)
    step = 1
    while step < n:
        if reverse:
            shifted = jnp.where(row < n - step, pltpu.roll(x, n - step, 0), NEG_BIG)
        else:
            shifted = jnp.where(row >= step, pltpu.roll(x, step, 0), NEG_BIG)
        x = jnp.maximum(x, shifted)
        step *= 2
    return x


_P_COL = {0: 0, 1: 1, 2: 2, 4: 3, 5: 4, 6: 5}
_G2_COL = {3: 0, 7: 1}


def _inproj_kernel(x_ref, lg_ref, lb_ref, w_ref, b_ref, wg_ref, bg_ref,
                   cos_ref, sin_ref, h0_ref, p_ref, g2_ref, gc_ref, gr_ref):
    tm = x_ref.shape[0]
    h = _layer_norm(x_ref[...], lg_ref[...], lb_ref[...])
    h0_ref[...] = h
    hb = h.astype(_MXU)
    cos = cos_ref[...]
    sin = sin_ref[...]

    pre = _dot(hb, wg_ref[...]) + bg_ref[...]
    row = lax.broadcasted_iota(jnp.int32, (CHUNK, CHUNK), 0)
    col = lax.broadcasted_iota(jnp.int32, (CHUNK, CHUNK), 1)
    tri_le = (col <= row).astype(_MXU)
    tri_ge = (col >= row).astype(_MXU)
    for c in range(tm // CHUNK):
        sl = slice(c * CHUNK, (c + 1) * CHUNK)
        blk = pre[sl, :]
        ls = _log_sigmoid(blk)
        pref = _dot01_left(tri_le, ls)
        suf = _dot01_left(tri_ge, ls)
        is_cum_f = (col >= CH_CUM_F) & (col < CH_CUM_F + N_HEADS)
        is_cum_b = (col >= CH_CUM_B) & (col < CH_CUM_B + N_HEADS)
        cum = jnp.where(is_cum_f, pref, jnp.where(is_cum_b, suf, 0.0))
        excess = blk - pltpu.roll(cum, LANES - (CH_CUM_F - CH_LI_F), 1)
        run_max = jnp.where(col < CH_LI_F + N_HEADS, _running_max(excess, False), _running_max(excess, True))
        is_max = (((col >= CH_MAX_F) & (col < CH_MAX_F + N_HEADS))
                  | ((col >= CH_MAX_B) & (col < CH_MAX_B + N_HEADS)))
        gates = jnp.where(is_cum_f | is_cum_b, cum,
                          jnp.where(is_max, pltpu.roll(run_max, CH_MAX_F - CH_LI_F, 1), blk))
        gc_ref[sl, :] = gates
        gr_ref[:, sl] = gates.T

    for sec in range(8):
        acc = _dot(hb, w_ref[:, sec * SEC:(sec + 1) * SEC]) + b_ref[:, sec * SEC:(sec + 1) * SEC]
        if sec in (0, 1):
            c0 = _P_COL[sec] * SEC
            for hh in range(N_HEADS):
                s = acc[:, hh * HEAD_DIM:(hh + 1) * HEAD_DIM]
                r = s * cos + pltpu.roll(s, HEAD_DIM // 2, 1) * sin
                if sec == 1:
                    r = r * K_SCALE
                p_ref[:, c0 + hh * HEAD_DIM:c0 + (hh + 1) * HEAD_DIM] = r.astype(p_ref.dtype)
        elif sec in _P_COL:
            if sec == 5:
                acc = acc * K_SCALE
            c0 = _P_COL[sec] * SEC
            p_ref[:, c0:c0 + SEC] = acc.astype(p_ref.dtype)
        else:
            c0 = _G2_COL[sec] * SEC
            g2_ref[:, c0:c0 + SEC] = acc


def _inproj(x2, seq, ln_g, ln_b, w_main, b_main, wg, bg, cos, sin, tm=PROJ_ROWS):
    T = x2.shape[0]
    nseq = seq // tm
    const = lambda i: (0, 0)
    return pl.pallas_call(
        _inproj_kernel,
        grid=(T // tm,),
        in_specs=[
            pl.BlockSpec((tm, D_MODEL), lambda i: (i, 0)),
            pl.BlockSpec((1, D_MODEL), const),
            pl.BlockSpec((1, D_MODEL), const),
            pl.BlockSpec((D_MODEL, 8 * SEC), const),
            pl.BlockSpec((1, 8 * SEC), const),
            pl.BlockSpec((D_MODEL, LANES), const),
            pl.BlockSpec((1, LANES), const),
            pl.BlockSpec((tm, HEAD_DIM), lambda i: (i % nseq, 0)),
            pl.BlockSpec((tm, HEAD_DIM), lambda i: (i % nseq, 0)),
        ],
        out_specs=[
            pl.BlockSpec((tm, D_MODEL), lambda i: (i, 0)),
            pl.BlockSpec((tm, 6 * SEC), lambda i: (i, 0)),
            pl.BlockSpec((tm, 2 * SEC), lambda i: (i, 0)),
            pl.BlockSpec((tm, LANES), lambda i: (i, 0)),
            pl.BlockSpec((LANES, tm), lambda i: (0, i)),
        ],
        out_shape=[
            jax.ShapeDtypeStruct((T, D_MODEL), _F32),
            jax.ShapeDtypeStruct((T, 6 * SEC), _MXU),
            jax.ShapeDtypeStruct((T, 2 * SEC), _F32),
            jax.ShapeDtypeStruct((T, LANES), _F32),
            jax.ShapeDtypeStruct((LANES, T), _F32),
        ],
        compiler_params=_params(("parallel",)),
        name="inproj",
    )(x2, ln_g, ln_b, w_main, b_main, wg, bg, cos, sin)


def _init_state(s_ref, cn_ref, m_ref):
    s_ref[...] = jnp.zeros(s_ref.shape, _F32)
    cn_ref[...] = jnp.zeros(cn_ref.shape, _F32)
    m_ref[...] = jnp.full(m_ref.shape, NEG_BIG, _F32)


def _dot_tn(a, b):
    return lax.dot_general(a, b, (((0,), (0,)), ((), ())), preferred_element_type=_F32)


def _lane_spread(gc, spread_ref):
    return _dot01_right(gc, spread_ref[...])


def _mlstm_direction(q, k, v, li_b, cum_b, max_b, li_row, cum_row, last_lane, mask, cn_ref, m_ref, si):
    cum_last = cum_row[:, last_lane:last_lane + 1]
    m_prev = m_ref[si, 0:1, :]
    cn_prev = cn_ref[si]
    ones = jnp.ones((CHUNK, HEAD_DIM), _MXU)
    v1 = jnp.concatenate([v, ones], axis=1)

    m_row = cum_b + jnp.maximum(max_b, m_prev)
    log_d = jnp.where(mask, cum_b - cum_row + li_row, NEG_BIG)
    d_w = jnp.exp(log_d - m_row)
    s_inter = jnp.exp(cum_b + m_prev - m_row)
    qk = _dot_nt(q, k) * d_w
    qk_hi = qk.astype(_MXU)
    qk_lo = (qk - qk_hi.astype(_F32)).astype(_MXU)
    intra = _dot(qk_hi, v1)
    inter = _dot(q, cn_prev.astype(_MXU))
    num = intra[:, :HEAD_DIM] + s_inter * inter[:, :HEAD_DIM]
    den = intra[:, HEAD_DIM:] + _dot(qk_lo, ones) + s_inter * inter[:, HEAD_DIM:]
    h_out = num / jnp.maximum(jnp.abs(den), jnp.exp(-m_row))

    a_max = jnp.max(cum_last - cum_row + li_row, axis=1, keepdims=True)
    kw = (k.astype(_F32) * jnp.exp(cum_last - cum_b + li_b - a_max)).astype(_MXU)
    m_new = jnp.maximum(cum_last + m_prev, a_max)
    s_old = jnp.exp(cum_last + m_prev - m_new)
    s_new = jnp.exp(a_max - m_new)
    cn_ref[si] = (jnp.concatenate([s_old, s_old], axis=1) * cn_prev
                  + jnp.concatenate([s_new, s_new], axis=1) * _dot_tn(kw, v1))
    m_ref[si] = jnp.broadcast_to(m_new, (SUBLANES, LANES))
    return h_out


def _retention_state_update(k, v, kw, gl, s_ref, si):
    kwv = (k.astype(_F32) * kw).astype(_MXU)
    s_ref[si] = gl * s_ref[si] + _dot_tn(kwv, v)


SEQ_PER_STEP = 4


def _head_cols(h, base=0):
    return slice(base + h * HEAD_DIM, base + (h + 1) * HEAD_DIM)


def _sweep_bwd_kernel(*refs):
    seq_refs = [refs[8 * k:8 * (k + 1)] for k in range(SEQ_PER_STEP)]
    qwb_ref, kwb_ref, gl_ref, spread_ref, yb_ref, s_ref, cn_ref, m_ref = refs[8 * SEQ_PER_STEP:]

    @pl.when(pl.program_id(1) == 0)
    def _():
        _init_state(s_ref, cn_ref, m_ref)

    row = lax.broadcasted_iota(jnp.int32, (CHUNK, CHUNK), 0)
    col = lax.broadcasted_iota(jnp.int32, (CHUNK, CHUNK), 1)
    mask = col >= row
    pairs = [(h, k) for h in range(N_HEADS) for k in range(SEQ_PER_STEP)]
    for h, k in pairs:
        sl = _head_cols(h)
        qs = (seq_refs[k][0][:, sl].astype(_F32) * qwb_ref[h]).astype(_MXU)
        yb_ref[k, :, sl] = _dot(qs, s_ref[k * N_HEADS + h].astype(_MXU))
    for h, k in pairs:
        sl = _head_cols(h)
        _retention_state_update(seq_refs[k][1][:, sl], seq_refs[k][2][:, sl], kwb_ref[h], gl_ref[h], s_ref,
                                k * N_HEADS + h)
    cols = [_lane_spread(seq_refs[k][6][...], spread_ref) for k in range(SEQ_PER_STEP)]
    for h in range(N_HEADS):
        sl = _head_cols(h)
        for k in range(SEQ_PER_STEP):
            mq_ref, mk_ref, mv_ref, _, gr_ref = seq_refs[k][3:8]
            h_b = _mlstm_direction(mq_ref[:, sl], mk_ref[:, sl], mv_ref[:, sl],
                                   cols[k][:, _head_cols(h)], cols[k][:, _head_cols(h, SEC)],
                                   cols[k][:, _head_cols(h, 2 * SEC)],
                                   gr_ref[CH_LI_B + h:CH_LI_B + h + 1, :],
                                   gr_ref[CH_CUM_B + h:CH_CUM_B + h + 1, :], 0, mask,
                                   cn_ref, m_ref, k * N_HEADS + h)
            yb_ref[k, :, _head_cols(h, SEC)] = h_b


def _sweep_fwd_kernel(*refs):
    seq_refs = [refs[10 * k:10 * (k + 1)] for k in range(SEQ_PER_STEP)]
    (rng_ref, mng_ref, dsym_ref, qwf_ref, kwf_ref, gl_ref, spread_ref,
     mixed_ref, s_ref, cn_ref, m_ref) = refs[10 * SEQ_PER_STEP:]

    @pl.when(pl.program_id(1) == 0)
    def _():
        _init_state(s_ref, cn_ref, m_ref)

    row = lax.broadcasted_iota(jnp.int32, (CHUNK, CHUNK), 0)
    col = lax.broadcasted_iota(jnp.int32, (CHUNK, CHUNK), 1)
    mask = col <= row

    pairs = [(h, k) for h in range(N_HEADS) for k in range(SEQ_PER_STEP)]

    def head_norms(ys):
        centred = [y - m for y, m in zip(ys, [jnp.mean(y, axis=1, keepdims=True) for y in ys])]
        var = [jnp.mean(c * c, axis=1, keepdims=True) for c in centred]
        return [c * lax.rsqrt(v + LN_EPS) for c, v in zip(centred, var)]

    qkv = [[seq_refs[k][i][:, _head_cols(h)] for i in range(3)] for h, k in pairs]
    scores = [_dot_nt(q, kk) for q, kk, _ in qkv]
    inter = [_dot((q.astype(_F32) * qwf_ref[h]).astype(_MXU), s_ref[k * N_HEADS + h].astype(_MXU))
             for (h, k), (q, _, _) in zip(pairs, qkv)]
    ys = [_dot((s * dsym_ref[h]).astype(_MXU), v) + it + seq_refs[k][8][:, _head_cols(h)]
          for (h, k), (_, _, v), s, it in zip(pairs, qkv, scores, inter)]
    for (h, k), (_, kk, v) in zip(pairs, qkv):
        _retention_state_update(kk, v, kwf_ref[h], gl_ref[h], s_ref, k * N_HEADS + h)
    for (h, k), yn in zip(pairs, head_norms(ys)):
        sl = _head_cols(h)
        g = seq_refs[k][9][:, sl]
        mixed_ref[k, :, sl] = (yn * rng_ref[:, sl] * (g * _sigmoid(g))).astype(mixed_ref.dtype)

    cols = [_lane_spread(seq_refs[k][6][...], spread_ref) for k in range(SEQ_PER_STEP)]
    ys = []
    for h, k in pairs:
        sl = _head_cols(h)
        mq_ref, mk_ref, mv_ref, _, gr_ref, yb_ref = seq_refs[k][3:9]
        h_f = _mlstm_direction(mq_ref[:, sl], mk_ref[:, sl], mv_ref[:, sl],
                               cols[k][:, _head_cols(h)], cols[k][:, _head_cols(h, SEC)],
                               cols[k][:, _head_cols(h, 2 * SEC)],
                               gr_ref[CH_LI_F + h:CH_LI_F + h + 1, :],
                               gr_ref[CH_CUM_F + h:CH_CUM_F + h + 1, :], CHUNK - 1, mask,
                               cn_ref, m_ref, k * N_HEADS + h)
        ys.append(h_f + yb_ref[:, _head_cols(h, SEC)])
    for (h, k), yn in zip(pairs, head_norms(ys)):
        sl2 = _head_cols(h, SEC)
        out = yn * mng_ref[:, _head_cols(h)] * _sigmoid(seq_refs[k][9][:, sl2])
        mixed_ref[k, :, sl2] = out.astype(mixed_ref.dtype)


def _state_scratch():
    return [
        pltpu.VMEM((SEQ_PER_STEP * N_HEADS, HEAD_DIM, HEAD_DIM), _F32),
        pltpu.VMEM((SEQ_PER_STEP * N_HEADS, HEAD_DIM, 2 * HEAD_DIM), _F32),
        pltpu.VMEM((SEQ_PER_STEP * N_HEADS, SUBLANES, LANES), _F32),
    ]


_SPREAD_SPEC = pl.BlockSpec((LANES, 3 * SEC), lambda b, n: (0, 0))


def _sweep_specs(nchunk, reverse, n_wide):
    def chunk(n):
        return (nchunk - 1 - n) if reverse else n

    per_seq = []
    for k in range(SEQ_PER_STEP):
        def rb(b, n, k=k):
            return (b * SEQ_PER_STEP + k) * nchunk + chunk(n)
        per_seq += [pl.BlockSpec((CHUNK, SEC), functools.partial(lambda b, n, s, rb: (rb(b, n), s), s=s, rb=rb))
                    for s in range(6)]
        per_seq.append(pl.BlockSpec((CHUNK, LANES), functools.partial(lambda b, n, rb: (rb(b, n), 0), rb=rb)))
        per_seq.append(pl.BlockSpec((LANES, CHUNK), functools.partial(lambda b, n, rb: (0, rb(b, n)), rb=rb)))
        per_seq += [pl.BlockSpec((CHUNK, 2 * SEC), functools.partial(lambda b, n, rb: (rb(b, n), 0), rb=rb))
                    for _ in range(n_wide)]
    out = pl.BlockSpec((SEQ_PER_STEP, CHUNK, 2 * SEC), lambda b, n: (b, chunk(n), 0))
    tab = pl.BlockSpec((N_HEADS, CHUNK, HEAD_DIM), lambda b, n: (0, 0, 0))
    return per_seq, out, tab


def _sweep_bwd(P, GC, GR, qwb, kwb, gl, spread, batch, seq):
    nchunk = seq // CHUNK
    per_seq, out, tab = _sweep_specs(nchunk, True, 0)
    yb = pl.pallas_call(
        _sweep_bwd_kernel,
        grid=(batch // SEQ_PER_STEP, nchunk),
        in_specs=per_seq + [tab, tab, tab, _SPREAD_SPEC],
        out_specs=out,
        out_shape=jax.ShapeDtypeStruct((batch, seq, 2 * SEC), _F32),
        scratch_shapes=_state_scratch(),
        compiler_params=_params(("parallel", "arbitrary")),
        name="sweep_bwd",
    )(*([P] * 6 + [GC, GR]) * SEQ_PER_STEP, qwb, kwb, gl, spread)
    return yb.reshape(batch * seq, 2 * SEC)


def _sweep_fwd(P, G2, GC, GR, YB, rng, mng, dsym, qwf, kwf, gl, spread, batch, seq):
    nchunk = seq // CHUNK
    per_seq, out, tab = _sweep_specs(nchunk, False, 2)
    gain = pl.BlockSpec((1, SEC), lambda b, n: (0, 0))
    mixed = pl.pallas_call(
        _sweep_fwd_kernel,
        grid=(batch // SEQ_PER_STEP, nchunk),
        in_specs=per_seq + [gain, gain, tab, tab, tab, tab, _SPREAD_SPEC],
        out_specs=out,
        out_shape=jax.ShapeDtypeStruct((batch, seq, 2 * SEC), _MXU),
        scratch_shapes=_state_scratch(),
        compiler_params=_params(("parallel", "arbitrary")),
        name="sweep_fwd",
    )(*([P] * 6 + [GC, GR, YB, G2]) * SEQ_PER_STEP, rng, mng, dsym, qwf, kwf, gl, spread)
    return mixed.reshape(batch * seq, 2 * SEC)


def _outproj_kernel(mixed_ref, h0_ref, wo_ref, lg_ref, lb_ref, wr_ref, h1_hbm, aff_ref, hbuf, sem, *, nsteps):
    tm = mixed_ref.shape[0]
    parts = [slice(i * tm // 2, (i + 1) * tm // 2) for i in range(2)]
    zs = [ALPHA * h0_ref[p, :] + _dot(mixed_ref[p, :], wo_ref[...]) for p in parts]
    h1s = [_layer_norm(z, lg_ref[...], lb_ref[...]) for z in zs]
    _pipelined_writeback(hbuf, sem, h1_hbm, jnp.concatenate(h1s, axis=0), pl.program_id(0), nsteps, tm)
    logits = [_dot(h1.astype(_MXU), wr_ref[...]) for h1 in h1s]
    valid = lax.broadcasted_iota(jnp.int32, logits[0].shape, 1) < N_EXPERTS
    logits = [jnp.where(valid, lg, NEG_BIG) for lg in logits]
    es = [jnp.exp(lg - jnp.max(lg, axis=1, keepdims=True)) for lg in logits]
    for p, e in zip(parts, es):
        aff_ref[p, :] = jnp.where(valid, e / jnp.sum(e, axis=1, keepdims=True), 0.0)


def _outproj(mixed, h0, wo, ln_g, ln_b, wr, tm=PROJ_ROWS):
    T = mixed.shape[0]
    const = lambda i: (0, 0)
    return pl.pallas_call(
        functools.partial(_outproj_kernel, nsteps=T // tm),
        grid=(T // tm,),
        in_specs=[
            pl.BlockSpec((tm, D_MODEL), lambda i: (i, 0)),
            pl.BlockSpec((tm, D_MODEL), lambda i: (i, 0)),
            pl.BlockSpec((D_MODEL, D_MODEL), const),
            pl.BlockSpec((1, D_MODEL), const),
            pl.BlockSpec((1, D_MODEL), const),
            pl.BlockSpec((D_MODEL, LANES), const),
        ],
        out_specs=[
            pl.BlockSpec(memory_space=pl.ANY),
            pl.BlockSpec((tm, LANES), lambda i: (i, 0)),
        ],
        out_shape=[
            jax.ShapeDtypeStruct((T, ROW_TILES, LANES), _F32),
            jax.ShapeDtypeStruct((T, LANES), _F32),
        ],
        scratch_shapes=[pltpu.VMEM((2, tm, D_MODEL), _F32), pltpu.SemaphoreType.DMA((2,))],
        compiler_params=_params(("arbitrary",)),
        name="outproj",
    )(mixed, h0, wo, ln_g, ln_b, wr)


def _thresh_kernel(aff_ref, thr_ref, rem_ref, *, cap):
    rows = aff_ref.shape[0]
    aff = aff_ref[...]

    def count(pred):
        c = jnp.sum(pred.astype(jnp.int32).reshape(rows // SUBLANES, SUBLANES, LANES), axis=0)
        c = jnp.broadcast_to(jnp.sum(c, axis=0, keepdims=True), (SUBLANES, LANES))
        shift = LANES // 2
        while shift >= N_EXPERTS:
            c = c + pltpu.roll(c, shift, 1)
            shift //= 2
        return c

    def body(i, ans):
        cand = ans | jnp.left_shift(jnp.int32(1), 30 - i)
        c = count(aff >= lax.bitcast_convert_type(cand[0:1, :], _F32))
        return jnp.where(c >= cap, cand, ans)

    ans = lax.fori_loop(0, 31, body, jnp.zeros((SUBLANES, LANES), jnp.int32))
    thr = jnp.where(ans >= MIN_NORMAL_BITS, lax.bitcast_convert_type(ans, _F32), 0.0)
    thr_ref[...] = thr
    rem_ref[...] = cap - count(aff > thr[0:1, :])


def _thresh(affc, cap):
    return pl.pallas_call(
        functools.partial(_thresh_kernel, cap=cap),
        out_shape=[jax.ShapeDtypeStruct((SUBLANES, LANES), _F32),
                   jax.ShapeDtypeStruct((SUBLANES, LANES), jnp.int32)],
        compiler_params=pltpu.CompilerParams(vmem_limit_bytes=VMEM_LIMIT),
        name="thresh",
    )(affc)


SELECT_BLOCKS = 4
SELECT_ROWS = SELECT_BLOCKS * TOK_BLOCK


def _select_kernel(aff_ref, thr_ref, rem_ref, spread_ref, before_ref, gsel_ref, lidx_ref, cnt_ref, off_ref,
                   nsel_ref, neq_ref):
    @pl.when(pl.program_id(0) == 0)
    def _():
        nsel_ref[...] = jnp.zeros(nsel_ref.shape, _F32)
        neq_ref[...] = jnp.zeros(neq_ref.shape, _F32)

    aff = aff_ref[...]
    thr = thr_ref[0:1, :]
    rem = rem_ref[0:1, :].astype(_F32)
    valid = lax.broadcasted_iota(jnp.int32, aff.shape, 1) < N_EXPERTS
    before = before_ref[...]
    gt = (aff > thr) & valid
    eq = (aff == thr) & valid
    eq_before = _dot(before, eq.astype(_MXU)) + neq_ref[0:1, :]
    sel = gt | (eq & (eq_before < rem))
    pos = _dot(before, sel.astype(_MXU))
    neq_ref[0:1, :] = neq_ref[0:1, :] + jnp.sum(eq.astype(_F32), axis=0, keepdims=True)
    gsel_ref[...] = jnp.where(sel, aff, 0.0)

    slot = (lax.broadcasted_iota(jnp.int32, (TOK_BLOCK, N_EXPERTS * TOK_BLOCK), 1) % TOK_BLOCK).astype(_F32)
    tok = lax.broadcasted_iota(jnp.int32, (TOK_BLOCK, N_EXPERTS * TOK_BLOCK), 0)
    start = nsel_ref[0:1, :]
    taken = start
    for q in range(SELECT_BLOCKS):
        rows = slice(q * TOK_BLOCK, (q + 1) * TOK_BLOCK)
        sel_q = sel[rows, :]
        off_ref[q] = taken.astype(jnp.int32)
        cnt = jnp.sum(sel_q.astype(_F32), axis=0, keepdims=True)
        cnt_ref[q] = cnt.astype(jnp.int32)
        ranked = jnp.where(sel_q, pos[rows, :] - (taken - start), -1.0).astype(_MXU)
        spread = _dot(ranked, spread_ref[...])
        tok0 = (pl.program_id(0) * SELECT_BLOCKS + q) * TOK_BLOCK
        lidx_ref[q] = jnp.sum(jnp.where(spread == slot, tok + tok0, 0), axis=0, keepdims=True)
        taken = taken + cnt
    nsel_ref[0:1, :] = taken


def _select(aff, thr, rem, spread, before):
    T = aff.shape[0]
    nb = T // TOK_BLOCK
    const = lambda b: (0, 0)
    return pl.pallas_call(
        _select_kernel,
        grid=(nb // SELECT_BLOCKS,),
        in_specs=[
            pl.BlockSpec((SELECT_ROWS, LANES), lambda b: (b, 0)),
            pl.BlockSpec((SUBLANES, LANES), const),
            pl.BlockSpec((SUBLANES, LANES), const),
            pl.BlockSpec((LANES, N_EXPERTS * TOK_BLOCK), const),
            pl.BlockSpec((SELECT_ROWS, SELECT_ROWS), const),
        ],
        out_specs=[
            pl.BlockSpec((SELECT_ROWS, LANES), lambda b: (b, 0)),
            pl.BlockSpec((SELECT_BLOCKS, 1, N_EXPERTS * TOK_BLOCK), lambda b: (b, 0, 0)),
            pl.BlockSpec((SELECT_BLOCKS, 1, LANES), lambda b: (b, 0, 0)),
            pl.BlockSpec((SELECT_BLOCKS, 1, LANES), lambda b: (b, 0, 0)),
        ],
        out_shape=[
            jax.ShapeDtypeStruct((T, LANES), _F32),
            jax.ShapeDtypeStruct((nb, 1, N_EXPERTS * TOK_BLOCK), jnp.int32),
            jax.ShapeDtypeStruct((nb, 1, LANES), jnp.int32),
            jax.ShapeDtypeStruct((nb, 1, LANES), jnp.int32),
        ],
        scratch_shapes=[pltpu.VMEM((SUBLANES, LANES), _F32), pltpu.VMEM((SUBLANES, LANES), _F32)],
        compiler_params=_params(("arbitrary",)),
        name="select",
    )(aff, thr, rem, spread, before)


FFN_ROWS = 512


def _ffn_kernel(cnt_ref, lidx_hbm, h1_hbm, wg_ref, wu_ref, wd_ref, y_hbm,
                xbuf, ybuf, lidx_smem, walk_ref, sem_idx, sem_rows, sem_out, *, tm, per, nb):
    e = pl.program_id(0)
    j = pl.program_id(1)
    step = e * per + j
    slot = step % 2

    per_expert = nb * TOK_BLOCK

    def idx_copy(en):
        return pltpu.make_async_copy(lidx_hbm.at[en], lidx_smem.at[pl.ds((en % 2) * per_expert, per_expert)],
                                     sem_idx.at[en % 2])

    def issue_tile(en, dst_slot):
        list_base = (en % 2) * per_expert
        group0 = dst_slot * (tm // SUBLANES)
        sem = sem_rows.at[dst_slot]

        def cond(st):
            return st[0] < tm

        def body(st):
            n, b, r = st
            c = cnt_ref[en * nb + b]
            take = jnp.minimum(c - r, tm - n)
            src_minus_dst = list_base + b * TOK_BLOCK + r - n

            def one(m):
                t = lidx_smem[src_minus_dst + m]
                pltpu.make_async_copy(h1_hbm.at[t], _row_view(xbuf, m, group0), sem).start()

            _for_each(n, n + take, one)
            done = r + take >= c
            return n + take, jnp.where(done, b + 1, b), jnp.where(done, 0, r + take)

        _, b, r = lax.while_loop(cond, body, (jnp.int32(0), walk_ref[0], walk_ref[1]))
        walk_ref[0] = b
        walk_ref[1] = r

    def restart_walk():
        walk_ref[0] = jnp.int32(0)
        walk_ref[1] = jnp.int32(0)

    @pl.when(step == 0)
    def _():
        idx_copy(0).start()
        idx_copy(0).wait()
        restart_walk()
        issue_tile(0, 0)

    @pl.when((j == 0) & (e + 1 < N_EXPERTS))
    def _():
        idx_copy(e + 1).start()

    @pl.when(j + 1 < per)
    def _():
        issue_tile(e, 1 - slot)

    @pl.when((j + 1 == per) & (e + 1 < N_EXPERTS))
    def _():
        idx_copy(e + 1).wait()
        restart_walk()
        issue_tile(e + 1, 1 - slot)

    pltpu.make_async_copy(h1_hbm.at[pl.ds(0, tm)], h1_hbm.at[pl.ds(0, tm)], sem_rows.at[slot]).wait()
    x = _matrix_value(xbuf.at[pl.ds(slot * (tm // SUBLANES), tm // SUBLANES)]).astype(_MXU)
    g = _dot(x, wg_ref[0])
    u = _dot(x, wu_ref[0])
    hid = (g * _sigmoid(g) * u).astype(_MXU)
    _pipelined_writeback(ybuf, sem_out, y_hbm, _dot(hid, wd_ref[0]), step, N_EXPERTS * per, tm)


def _ffn(cnt_e, lidx_e, h1, wg, wu, wd, cap, tm):
    per = cap // tm
    nb = lidx_e.shape[1] // TOK_BLOCK
    return pl.pallas_call(
        functools.partial(_ffn_kernel, tm=tm, per=per, nb=nb),
        grid_spec=pltpu.PrefetchScalarGridSpec(
            num_scalar_prefetch=1,
            grid=(N_EXPERTS, per),
            in_specs=[
                pl.BlockSpec(memory_space=pl.ANY),
                pl.BlockSpec(memory_space=pl.ANY),
                pl.BlockSpec((1, D_MODEL, D_FF), lambda e, j, *_: (e, 0, 0)),
                pl.BlockSpec((1, D_MODEL, D_FF), lambda e, j, *_: (e, 0, 0)),
                pl.BlockSpec((1, D_FF, D_MODEL), lambda e, j, *_: (e, 0, 0)),
            ],
            out_specs=pl.BlockSpec(memory_space=pl.ANY),
            scratch_shapes=[
                pltpu.VMEM((2 * tm // SUBLANES, ROW_TILES, SUBLANES, LANES), _F32),
                pltpu.VMEM((2, tm, D_MODEL), _F32),
                pltpu.SMEM((2 * nb * TOK_BLOCK,), jnp.int32),
                pltpu.SMEM((2,), jnp.int32),
                pltpu.SemaphoreType.DMA((2,)),
                pltpu.SemaphoreType.DMA((2,)),
                pltpu.SemaphoreType.DMA((2,)),
            ],
        ),
        out_shape=jax.ShapeDtypeStruct((N_EXPERTS * cap, ROW_TILES, LANES), _F32),
        compiler_params=_params(("arbitrary", "arbitrary")),
        name="ffn",
    )(cnt_e, lidx_e, h1, wg, wu, wd)


SLOT_GROUPS = TOK_BLOCK // SUBLANES
COMBINE_ROWS = 32


def _combine_kernel(cnt_ref, off_ref, lidx_hbm, ye_hbm, h1_hbm, gsel_ref, lg_ref, lb_ref, y_ref,
                    slots_ref, hres_ref, lidx_smem, sem_idx, sem_rows, sem_res, *, cap, nb):
    b = pl.program_id(0)
    slot = b % 2

    per_block = N_EXPERTS * TOK_BLOCK

    def idx_copy(bn):
        return pltpu.make_async_copy(lidx_hbm.at[bn], lidx_smem.at[pl.ds((bn % 2) * per_block, per_block)],
                                     sem_idx.at[bn % 2])

    def res_copies(bn):
        return _matrix_copies(h1_hbm, bn * TOK_BLOCK, hres_ref.at[bn % 2], sem_res.at[bn % 2])

    def issue_block(bn):
        par = bn % 2
        sem = sem_rows.at[par]
        tok0 = bn * TOK_BLOCK
        for c in res_copies(bn):
            c.start()
        for e in range(N_EXPERTS):
            c = cnt_ref[bn * N_EXPERTS + e]
            base = e * cap + off_ref[bn * N_EXPERTS + e]
            list_minus_src = par * per_block + e * TOK_BLOCK - base
            group0 = (par * N_EXPERTS + e) * SLOT_GROUPS

            def one(src_row):
                t = lidx_smem[list_minus_src + src_row] - tok0
                pltpu.make_async_copy(ye_hbm.at[src_row], _row_view(slots_ref, t, group0), sem).start()

            _for_each(base, base + c, one)

    @pl.when(b == 0)
    def _():
        slots_ref[...] = jnp.zeros(slots_ref.shape, _F32)
        idx_copy(0).start()
        idx_copy(0).wait()
        issue_block(0)
        if nb > 1:
            idx_copy(1).start()

    @pl.when(b + 1 < nb)
    def _():
        idx_copy(b + 1).wait()
        issue_block(b + 1)

    @pl.when(b + 2 < nb)
    def _():
        idx_copy(b + 2).start()

    rows_in_flight = cnt_ref[b * N_EXPERTS]
    for e in range(1, N_EXPERTS):
        rows_in_flight = rows_in_flight + cnt_ref[b * N_EXPERTS + e]

    @pl.when(rows_in_flight > 0)
    def _():
        pltpu.make_async_copy(ye_hbm.at[pl.ds(0, rows_in_flight)], ye_hbm.at[pl.ds(0, rows_in_flight)],
                              sem_rows.at[slot]).wait()

    for c in res_copies(b):
        c.wait()

    groups = COMBINE_ROWS // SUBLANES
    for tg in range(TOK_BLOCK // COMBINE_ROWS):
        rows = slice(tg * COMBINE_ROWS, (tg + 1) * COMBINE_ROWS)
        acc = [ALPHA * hres_ref[slot, rows, j * LANES:(j + 1) * LANES] for j in range(ROW_TILES)]
        for e in range(N_EXPERTS):
            gate = jnp.broadcast_to(gsel_ref[rows, e:e + 1], (COMBINE_ROWS, LANES))
            blk = slots_ref[pl.ds((slot * N_EXPERTS + e) * SLOT_GROUPS + tg * groups, groups)]
            for j in range(ROW_TILES):
                acc[j] = acc[j] + gate * blk[:, j].reshape(COMBINE_ROWS, LANES)
        for j in range(ROW_TILES):
            y_ref[rows, j * LANES:(j + 1) * LANES] = acc[j]
    y_ref[...] = _layer_norm(y_ref[...], lg_ref[...], lb_ref[...])


def _combine(cnt_t, off_t, lidx, ye, gsel, h1, ln_g, ln_b, cap):
    T = h1.shape[0]
    nb = T // TOK_BLOCK
    return pl.pallas_call(
        functools.partial(_combine_kernel, cap=cap, nb=nb),
        grid_spec=pltpu.PrefetchScalarGridSpec(
            num_scalar_prefetch=2,
            grid=(nb,),
            in_specs=[
                pl.BlockSpec(memory_space=pl.ANY),
                pl.BlockSpec(memory_space=pl.ANY),
                pl.BlockSpec(memory_space=pl.ANY),
                pl.BlockSpec((TOK_BLOCK, LANES), lambda b, *_: (b, 0)),
                pl.BlockSpec((1, D_MODEL), lambda b, *_: (0, 0)),
                pl.BlockSpec((1, D_MODEL), lambda b, *_: (0, 0)),
            ],
            out_specs=pl.BlockSpec((TOK_BLOCK, D_MODEL), lambda b, *_: (b, 0)),
            scratch_shapes=[
                pltpu.VMEM((2 * N_EXPERTS * SLOT_GROUPS, ROW_TILES, SUBLANES, LANES), _F32),
                pltpu.VMEM((2, TOK_BLOCK, D_MODEL), _F32),
                pltpu.SMEM((2 * N_EXPERTS * TOK_BLOCK,), jnp.int32),
                pltpu.SemaphoreType.DMA((2,)),
                pltpu.SemaphoreType.DMA((2,)),
                pltpu.SemaphoreType.DMA((2,)),
            ],
        ),
        out_shape=jax.ShapeDtypeStruct((T, D_MODEL), _F32),
        compiler_params=_params(("arbitrary",)),
        name="combine",
    )(cnt_t, off_t, lidx, ye, h1, gsel, ln_g, ln_b)


def _tables(seq):
    half = HEAD_DIM // 2
    inv = 1.0 / (ROPE_BASE ** (jnp.arange(half, dtype=_F32) / half))
    ang = jnp.arange(seq, dtype=_F32)[:, None] * inv[None, :]
    cos = jnp.concatenate([jnp.cos(ang), jnp.cos(ang)], axis=1)
    sin = jnp.concatenate([-jnp.sin(ang), jnp.sin(ang)], axis=1)
    log_g = jnp.log1p(-jnp.exp2(-5.0 - jnp.arange(N_HEADS, dtype=_F32)))[:, None, None]
    pos = jnp.arange(CHUNK, dtype=_F32)
    rows = lambda f: jnp.broadcast_to(jnp.exp(log_g * f[None, :, None]), (N_HEADS, CHUNK, HEAD_DIM))
    dsym = jnp.exp(log_g * jnp.abs(pos[:, None] - pos[None, :])[None])
    tabs = dict(
        cos=cos, sin=sin, dsym=dsym,
        qwf=rows(pos + 1.0), kwf=rows(CHUNK - 1.0 - pos),
        qwb=rows(CHUNK - pos), kwb=rows(pos),
        gl=rows(jnp.full((CHUNK,), float(CHUNK), _F32)),
    )
    spread = lambda chans: (jnp.arange(LANES)[:, None] == jnp.repeat(jnp.asarray(chans), LANES)[None, :]).astype(_MXU)
    tabs["spread"] = spread(range(N_EXPERTS))
    r = jnp.arange(SELECT_ROWS)
    tabs["before"] = (r[None, :] < r[:, None]).astype(_MXU)
    heads = lambda ch: list(range(ch, ch + N_HEADS))
    tabs["spread_f"] = spread(heads(CH_LI_F) + heads(CH_CUM_F) + heads(CH_MAX_F))
    tabs["spread_b"] = spread(heads(CH_LI_B) + heads(CH_CUM_B) + heads(CH_MAX_B))
    return tabs


def _trunk(x, w):
    batch, seq, _ = x.shape
    T = batch * seq
    nb = T // TOK_BLOCK
    cap = CAP_FACTOR * T // N_EXPERTS
    t = _tables(seq)
    h0, P, G2, GC, GR = _inproj(x.reshape(T, D_MODEL), seq, w["ln_in_g"], w["ln_in_b"], w["w_main"],
                                w["b_main"], w["wg"], w["bg"], t["cos"], t["sin"])
    YB = _sweep_bwd(P, GC, GR, t["qwb"], t["kwb"], t["gl"], t["spread_b"], batch, seq)
    mixed = _sweep_fwd(P, G2, GC, GR, YB, w["ret_g"], w["mlstm_g"], t["dsym"], t["qwf"], t["kwf"], t["gl"],
                       t["spread_f"], batch, seq)
    h1, aff = _outproj(mixed, h0, w["w_o"], w["ln1_g"], w["ln1_b"], w["w_r"])
    affc = aff[:, :N_EXPERTS].reshape(T // SUBLANES, LANES)
    thr, rem = _thresh(affc, cap)
    gsel, lidx, cnt, off = _select(aff, thr, rem, t["spread"], t["before"])
    cnt2 = cnt.reshape(nb, LANES)[:, :N_EXPERTS]
    off2 = off.reshape(nb, LANES)[:, :N_EXPERTS]
    lidx_e = lidx.reshape(nb, N_EXPERTS, TOK_BLOCK).transpose(1, 0, 2).reshape(N_EXPERTS, nb * TOK_BLOCK)
    ye = _ffn(cnt2.T.reshape(-1), lidx_e, h1, w["w_gate"], w["w_up"], w["w_down"], cap, min(FFN_ROWS, cap))
    y = _combine(cnt2.reshape(-1), off2.reshape(-1), lidx.reshape(nb, N_EXPERTS * TOK_BLOCK), ye, gsel, h1,
                 w["ln2_g"], w["ln2_b"], cap)
    return y.reshape(batch, seq, D_MODEL)


def _prep_weights(ln_in_g, ln_in_b, w_in, b_in, ret_norm_g, mlstm_norm_g, w_o, ln1_g, ln1_b, w_router,
                  w_gate, w_up, w_down, ln2_g, ln2_b):
    main = 8 * SEC
    ngate = 4 * N_HEADS
    row = lambda v: v.reshape(1, -1).astype(_F32)
    wg = jnp.pad(w_in[0][:, main:main + ngate], ((0, 0), (0, LANES - ngate)))
    bg = jnp.pad(b_in[0][main:main + ngate], (0, LANES - ngate))
    return dict(
        ln_in_g=row(ln_in_g), ln_in_b=row(ln_in_b),
        w_main=w_in[0][:, :main].astype(_MXU), b_main=row(b_in[0][:main]),
        wg=wg.astype(_MXU), bg=row(bg),
        ret_g=row(ret_norm_g[0]), mlstm_g=row(mlstm_norm_g[0]),
        w_o=w_o[0].astype(_MXU), ln1_g=row(ln1_g[0]), ln1_b=row(ln1_b[0]),
        w_r=jnp.pad(w_router[0], ((0, 0), (0, LANES - N_EXPERTS))).astype(_MXU),
        w_gate=w_gate[0].astype(_MXU), w_up=w_up[0].astype(_MXU), w_down=w_down[0].astype(_MXU),
        ln2_g=row(ln2_g[0]), ln2_b=row(ln2_b[0]),
    )


def kernel(x_prompt, x_sample, ln_in_g, ln_in_b, w_in, b_in, ret_norm_g, mlstm_norm_g, w_o, ln1_g, ln1_b,
           w_router, w_gate, w_up, w_down, ln2_g, ln2_b):
    w = _prep_weights(ln_in_g, ln_in_b, w_in, b_in, ret_norm_g, mlstm_norm_g, w_o, ln1_g, ln1_b, w_router,
                      w_gate, w_up, w_down, ln2_g, ln2_b)
    return (_trunk(x_prompt, w), _trunk(x_sample, w))
```

```python
import functools

import jax
import jax.numpy as jnp
from jax import lax
from jax.experimental import pallas as pl
from jax.experimental.pallas import tpu as pltpu

D_MODEL = 1024
N_HEADS = 4
HEAD_DIM = 128
SEC = N_HEADS * HEAD_DIM
CHUNK = 128
N_EXPERTS = 16
D_FF = 2 * D_MODEL
CAP_FACTOR = 2
ROPE_BASE = 10000.0
LN_EPS = 1e-5
NEG_BIG = -1e30
DEPTH = 1
ALPHA = (2.0 * DEPTH) ** 0.25
K_SCALE = HEAD_DIM ** -0.5
LANES = 128
SUBLANES = 8
SUBLANE_BITS = 3
TOK_BLOCK = 128
PROJ_ROWS = 1024
MIN_NORMAL_BITS = 0x00800000
VMEM_LIMIT = 56 * 1024 * 1024

CH_LI_F, CH_CUM_F = 0, 4
CH_LI_B, CH_CUM_B = 8, 12
CH_MAX_F, CH_MAX_B = 16, 24

_MXU = jnp.bfloat16
_F32 = jnp.float32


def _dot(a, b):
    return jnp.dot(a, b, preferred_element_type=_F32)


def _dot_nt(a, b):
    return lax.dot_general(a, b, (((1,), (1,)), ((), ())), preferred_element_type=_F32)


def _split3(x):
    x1 = x.astype(_MXU)
    r1 = x - x1.astype(_F32)
    x2 = r1.astype(_MXU)
    r2 = r1 - x2.astype(_F32)
    return x1, x2, r2.astype(_MXU)


def _dot01_left(a01, x):
    x1, x2, x3 = _split3(x)
    return _dot(a01, x1) + _dot(a01, x2) + _dot(a01, x3)


def _dot01_right(x, a01):
    x1, x2, x3 = _split3(x)
    return _dot(x1, a01) + _dot(x2, a01) + _dot(x3, a01)


def _layer_norm(x, g, b):
    mu = jnp.mean(x, axis=-1, keepdims=True)
    xc = x - mu
    var = jnp.mean(xc * xc, axis=-1, keepdims=True)
    return xc * lax.rsqrt(var + LN_EPS) * g + b


def _log_sigmoid(x):
    return jnp.minimum(x, 0.0) - jnp.log1p(jnp.exp(-jnp.abs(x)))


def _sigmoid(x):
    return 1.0 / (1.0 + jnp.exp(-x))


def _params(sem):
    return pltpu.CompilerParams(dimension_semantics=sem, vmem_limit_bytes=VMEM_LIMIT)


ROW_TILES = D_MODEL // LANES


ISSUE_UNROLL = 4


def _for_each(lo, hi, body):
    nblk = lax.shift_right_logical(hi - lo, ISSUE_UNROLL.bit_length() - 1)

    def block(k, carry):
        for u in range(ISSUE_UNROLL):
            body(lo + k * ISSUE_UNROLL + u)
        return carry

    def single(i, carry):
        body(i)
        return carry

    lax.fori_loop(0, nblk, block, 0)
    lax.fori_loop(lo + nblk * ISSUE_UNROLL, hi, single, 0)


def _row_view(buf, r, group0=0):
    return buf.at[group0 + lax.shift_right_logical(r, SUBLANE_BITS), :, jnp.bitwise_and(r, SUBLANES - 1), :]


def _matrix_value(buf):
    rows = buf.shape[0] * SUBLANES
    return jnp.concatenate([buf[:, j].reshape(rows, LANES) for j in range(ROW_TILES)], axis=1)


def _row_tile_copies(mat, hbm, row0, sem):
    rows = mat.shape[0]
    return [pltpu.make_async_copy(mat.at[:, pl.ds(j * LANES, LANES)], hbm.at[pl.ds(row0, rows), j, :], sem)
            for j in range(ROW_TILES)]


def _matrix_copies(hbm, row0, mat, sem):
    rows = mat.shape[0]
    return [pltpu.make_async_copy(hbm.at[pl.ds(row0, rows), j, :], mat.at[:, pl.ds(j * LANES, LANES)], sem)
            for j in range(ROW_TILES)]


def _pipelined_writeback(buf, sem, hbm, value, step, nsteps, rows):
    slot = step % 2

    @pl.when(step >= 2)
    def _():
        for c in _row_tile_copies(buf.at[slot], hbm, (step - 2) * rows, sem.at[slot]):
            c.wait()

    buf[slot] = value
    for c in _row_tile_copies(buf.at[slot], hbm, step * rows, sem.at[slot]):
        c.start()

    @pl.when(step == nsteps - 1)
    def _():
        if nsteps > 1:
            for c in _row_tile_copies(buf.at[1 - slot], hbm, (step - 1) * rows, sem.at[1 - slot]):
                c.wait()
        for c in _row_tile_copies(buf.at[slot], hbm, step * rows, sem.at[slot]):
            c.wait()


def _running_max(x, reverse):
    n = x.shape[0]
    row = lax.broadcasted_iota(jnp.int32, x.shape, 0)
    step = 1
    while step < n:
        if reverse:
            shifted = jnp.where(row < n - step, pltpu.roll(x, n - step, 0), NEG_BIG)
        else:
            shifted = jnp.where(row >= step, pltpu.roll(x, step, 0), NEG_BIG)
        x = jnp.maximum(x, shifted)
        step *= 2
    return x


_P_COL = {0: 0, 1: 1, 2: 2, 4: 3, 5: 4, 6: 5}
_G2_COL = {3: 0, 7: 1}


def _inproj_kernel(x_ref, lg_ref, lb_ref, w_ref, b_ref, wg_ref, bg_ref,
                   cos_ref, sin_ref, h0_ref, p_ref, g2_ref, gc_ref, gr_ref):
    tm = x_ref.shape[0]
    h = _layer_norm(x_ref[...], lg_ref[...], lb_ref[...])
    h0_ref[...] = h
    hb = h.astype(_MXU)
    cos = cos_ref[...]
    sin = sin_ref[...]

    pre = _dot(hb, wg_ref[...]) + bg_ref[...]
    row = lax.broadcasted_iota(jnp.int32, (CHUNK, CHUNK), 0)
    col = lax.broadcasted_iota(jnp.int32, (CHUNK, CHUNK), 1)
    tri_le = (col <= row).astype(_MXU)
    tri_ge = (col >= row).astype(_MXU)
    for c in range(tm // CHUNK):
        sl = slice(c * CHUNK, (c + 1) * CHUNK)
        blk = pre[sl, :]
        ls = _log_sigmoid(blk)
        pref = _dot01_left(tri_le, ls)
        suf = _dot01_left(tri_ge, ls)
        is_cum_f = (col >= CH_CUM_F) & (col < CH_CUM_F + N_HEADS)
        is_cum_b = (col >= CH_CUM_B) & (col < CH_CUM_B + N_HEADS)
        cum = jnp.where(is_cum_f, pref, jnp.where(is_cum_b, suf, 0.0))
        excess = blk - pltpu.roll(cum, LANES - (CH_CUM_F - CH_LI_F), 1)
        run_max = jnp.where(col < CH_LI_F + N_HEADS, _running_max(excess, False), _running_max(excess, True))
        is_max = (((col >= CH_MAX_F) & (col < CH_MAX_F + N_HEADS))
                  | ((col >= CH_MAX_B) & (col < CH_MAX_B + N_HEADS)))
        gates = jnp.where(is_cum_f | is_cum_b, cum,
                          jnp.where(is_max, pltpu.roll(run_max, CH_MAX_F - CH_LI_F, 1), blk))
        gc_ref[sl, :] = gates
        gr_ref[:, sl] = gates.T

    for sec in range(8):
        acc = _dot(hb, w_ref[:, sec * SEC:(sec + 1) * SEC]) + b_ref[:, sec * SEC:(sec + 1) * SEC]
        if sec in (0, 1):
            c0 = _P_COL[sec] * SEC
            for hh in range(N_HEADS):
                s = acc[:, hh * HEAD_DIM:(hh + 1) * HEAD_DIM]
                r = s * cos + pltpu.roll(s, HEAD_DIM // 2, 1) * sin
                if sec == 1:
                    r = r * K_SCALE
                p_ref[:, c0 + hh * HEAD_DIM:c0 + (hh + 1) * HEAD_DIM] = r.astype(p_ref.dtype)
        elif sec in _P_COL:
            if sec == 5:
                acc = acc * K_SCALE
            c0 = _P_COL[sec] * SEC
            p_ref[:, c0:c0 + SEC] = acc.astype(p_ref.dtype)
        else:
            c0 = _G2_COL[sec] * SEC
            g2_ref[:, c0:c0 + SEC] = acc


def _inproj(x2, seq, ln_g, ln_b, w_main, b_main, wg, bg, cos, sin, tm=PROJ_ROWS):
    T = x2.shape[0]
    nseq = seq // tm
    const = lambda i: (0, 0)
    return pl.pallas_call(
        _inproj_kernel,
        grid=(T // tm,),
        in_specs=[
            pl.BlockSpec((tm, D_MODEL), lambda i: (i, 0)),
            pl.BlockSpec((1, D_MODEL), const),
            pl.BlockSpec((1, D_MODEL), const),
            pl.BlockSpec((D_MODEL, 8 * SEC), const, pipeline_mode=pl.Buffered(1)),
            pl.BlockSpec((1, 8 * SEC), const),
            pl.BlockSpec((D_MODEL, LANES), const),
            pl.BlockSpec((1, LANES), const),
            pl.BlockSpec((tm, HEAD_DIM), lambda i: (i % nseq, 0)),
            pl.BlockSpec((tm, HEAD_DIM), lambda i: (i % nseq, 0)),
        ],
        out_specs=[
            pl.BlockSpec((tm, D_MODEL), lambda i: (i, 0)),
            pl.BlockSpec((tm, 6 * SEC), lambda i: (i, 0)),
            pl.BlockSpec((tm, 2 * SEC), lambda i: (i, 0)),
            pl.BlockSpec((tm, LANES), lambda i: (i, 0)),
            pl.BlockSpec((LANES, tm), lambda i: (0, i)),
        ],
        out_shape=[
            jax.ShapeDtypeStruct((T, D_MODEL), _F32),
            jax.ShapeDtypeStruct((T, 6 * SEC), _MXU),
            jax.ShapeDtypeStruct((T, 2 * SEC), _F32),
            jax.ShapeDtypeStruct((T, LANES), _F32),
            jax.ShapeDtypeStruct((LANES, T), _F32),
        ],
        compiler_params=_params(("parallel",)),
        name="inproj",
    )(x2, ln_g, ln_b, w_main, b_main, wg, bg, cos, sin)


def _init_state(s_ref, cn_ref, m_ref):
    s_ref[...] = jnp.zeros(s_ref.shape, _F32)
    cn_ref[...] = jnp.zeros(cn_ref.shape, _F32)
    m_ref[...] = jnp.full(m_ref.shape, NEG_BIG, _F32)


def _dot_tn(a, b):
    return lax.dot_general(a, b, (((0,), (0,)), ((), ())), preferred_element_type=_F32)


def _lane_spread(gc, spread_ref):
    return _dot01_right(gc, spread_ref[...])


def _mlstm_direction(q, k, v, li_b, cum_b, max_b, li_row, cum_row, last_lane, mask, cn_ref, m_ref, si):
    cum_last = cum_row[:, last_lane:last_lane + 1]
    m_prev = m_ref[si, 0:1, :]
    cn_prev = cn_ref[si]
    ones = jnp.ones((CHUNK, HEAD_DIM), _MXU)
    v1 = jnp.concatenate([v, ones], axis=1)

    m_row = cum_b + jnp.maximum(max_b, m_prev)
    log_d = jnp.where(mask, cum_b - cum_row + li_row, NEG_BIG)
    d_w = jnp.exp(log_d - m_row)
    s_inter = jnp.exp(cum_b + m_prev - m_row)
    qk = _dot_nt(q, k) * d_w
    qk_hi = qk.astype(_MXU)
    qk_lo = (qk - qk_hi.astype(_F32)).astype(_MXU)
    intra = _dot(qk_hi, v1)
    inter = _dot(q, cn_prev.astype(_MXU))
    num = intra[:, :HEAD_DIM] + s_inter * inter[:, :HEAD_DIM]
    den = intra[:, HEAD_DIM:] + _dot(qk_lo, ones) + s_inter * inter[:, HEAD_DIM:]
    h_out = num / jnp.maximum(jnp.abs(den), jnp.exp(-m_row))

    a_max = jnp.max(cum_last - cum_row + li_row, axis=1, keepdims=True)
    kw = (k.astype(_F32) * jnp.exp(cum_last - cum_b + li_b - a_max)).astype(_MXU)
    m_new = jnp.maximum(cum_last + m_prev, a_max)
    s_old = jnp.exp(cum_last + m_prev - m_new)
    s_new = jnp.exp(a_max - m_new)
    cn_ref[si] = (jnp.concatenate([s_old, s_old], axis=1) * cn_prev
                  + jnp.concatenate([s_new, s_new], axis=1) * _dot_tn(kw, v1))
    m_ref[si] = jnp.broadcast_to(m_new, (SUBLANES, LANES))
    return h_out


def _retention_state_update(k, v, kw, gl, s_ref, si):
    kwv = (k.astype(_F32) * kw).astype(_MXU)
    s_ref[si] = gl * s_ref[si] + _dot_tn(kwv, v)


SEQ_PER_STEP = 4


def _head_cols(h, base=0):
    return slice(base + h * HEAD_DIM, base + (h + 1) * HEAD_DIM)


def _sweep_bwd_kernel(*refs):
    seq_refs = [refs[8 * k:8 * (k + 1)] for k in range(SEQ_PER_STEP)]
    qwb_ref, kwb_ref, gl_ref, spread_ref, yb_ref, s_ref, cn_ref, m_ref = refs[8 * SEQ_PER_STEP:]

    @pl.when(pl.program_id(1) == 0)
    def _():
        _init_state(s_ref, cn_ref, m_ref)

    row = lax.broadcasted_iota(jnp.int32, (CHUNK, CHUNK), 0)
    col = lax.broadcasted_iota(jnp.int32, (CHUNK, CHUNK), 1)
    mask = col >= row
    pairs = [(h, k) for h in range(N_HEADS) for k in range(SEQ_PER_STEP)]
    for h, k in pairs:
        sl = _head_cols(h)
        qs = (seq_refs[k][0][:, sl].astype(_F32) * qwb_ref[h]).astype(_MXU)
        yb_ref[k, :, sl] = _dot(qs, s_ref[k * N_HEADS + h].astype(_MXU))
    for h, k in pairs:
        sl = _head_cols(h)
        _retention_state_update(seq_refs[k][1][:, sl], seq_refs[k][2][:, sl], kwb_ref[h], gl_ref[h], s_ref,
                                k * N_HEADS + h)
    cols = [_lane_spread(seq_refs[k][6][...], spread_ref) for k in range(SEQ_PER_STEP)]
    for h in range(N_HEADS):
        sl = _head_cols(h)
        for k in range(SEQ_PER_STEP):
            mq_ref, mk_ref, mv_ref, _, gr_ref = seq_refs[k][3:8]
            h_b = _mlstm_direction(mq_ref[:, sl], mk_ref[:, sl], mv_ref[:, sl],
                                   cols[k][:, _head_cols(h)], cols[k][:, _head_cols(h, SEC)],
                                   cols[k][:, _head_cols(h, 2 * SEC)],
                                   gr_ref[CH_LI_B + h:CH_LI_B + h + 1, :],
                                   gr_ref[CH_CUM_B + h:CH_CUM_B + h + 1, :], 0, mask,
                                   cn_ref, m_ref, k * N_HEADS + h)
            yb_ref[k, :, _head_cols(h, SEC)] = h_b


def _sweep_fwd_kernel(*refs):
    seq_refs = [refs[10 * k:10 * (k + 1)] for k in range(SEQ_PER_STEP)]
    (rng_ref, mng_ref, dsym_ref, qwf_ref, kwf_ref, gl_ref, spread_ref,
     mixed_ref, s_ref, cn_ref, m_ref) = refs[10 * SEQ_PER_STEP:]

    @pl.when(pl.program_id(1) == 0)
    def _():
        _init_state(s_ref, cn_ref, m_ref)

    row = lax.broadcasted_iota(jnp.int32, (CHUNK, CHUNK), 0)
    col = lax.broadcasted_iota(jnp.int32, (CHUNK, CHUNK), 1)
    mask = col <= row

    pairs = [(h, k) for h in range(N_HEADS) for k in range(SEQ_PER_STEP)]

    def head_norms(ys):
        centred = [y - m for y, m in zip(ys, [jnp.mean(y, axis=1, keepdims=True) for y in ys])]
        var = [jnp.mean(c * c, axis=1, keepdims=True) for c in centred]
        return [c * lax.rsqrt(v + LN_EPS) for c, v in zip(centred, var)]

    qkv = [[seq_refs[k][i][:, _head_cols(h)] for i in range(3)] for h, k in pairs]
    scores = [_dot_nt(q, kk) for q, kk, _ in qkv]
    inter = [_dot((q.astype(_F32) * qwf_ref[h]).astype(_MXU), s_ref[k * N_HEADS + h].astype(_MXU))
             for (h, k), (q, _, _) in zip(pairs, qkv)]
    ys = [_dot((s * dsym_ref[h]).astype(_MXU), v) + it + seq_refs[k][8][:, _head_cols(h)]
          for (h, k), (_, _, v), s, it in zip(pairs, qkv, scores, inter)]
    for (h, k), (_, kk, v) in zip(pairs, qkv):
        _retention_state_update(kk, v, kwf_ref[h], gl_ref[h], s_ref, k * N_HEADS + h)
    for (h, k), yn in zip(pairs, head_norms(ys)):
        sl = _head_cols(h)
        g = seq_refs[k][9][:, sl]
        mixed_ref[k, :, sl] = (yn * rng_ref[:, sl] * (g * _sigmoid(g))).astype(mixed_ref.dtype)

    cols = [_lane_spread(seq_refs[k][6][...], spread_ref) for k in range(SEQ_PER_STEP)]
    ys = []
    for h, k in pairs:
        sl = _head_cols(h)
        mq_ref, mk_ref, mv_ref, _, gr_ref, yb_ref = seq_refs[k][3:9]
        h_f = _mlstm_direction(mq_ref[:, sl], mk_ref[:, sl], mv_ref[:, sl],
                               cols[k][:, _head_cols(h)], cols[k][:, _head_cols(h, SEC)],
                               cols[k][:, _head_cols(h, 2 * SEC)],
                               gr_ref[CH_LI_F + h:CH_LI_F + h + 1, :],
                               gr_ref[CH_CUM_F + h:CH_CUM_F + h + 1, :], CHUNK - 1, mask,
                               cn_ref, m_ref, k * N_HEADS + h)
        ys.append(h_f + yb_ref[:, _head_cols(h, SEC)])
    for (h, k), yn in zip(pairs, head_norms(ys)):
        sl2 = _head_cols(h, SEC)
        out = yn * mng_ref[:, _head_cols(h)] * _sigmoid(seq_refs[k][9][:, sl2])
        mixed_ref[k, :, sl2] = out.astype(mixed_ref.dtype)


def _state_scratch():
    return [
        pltpu.VMEM((SEQ_PER_STEP * N_HEADS, HEAD_DIM, HEAD_DIM), _F32),
        pltpu.VMEM((SEQ_PER_STEP * N_HEADS, HEAD_DIM, 2 * HEAD_DIM), _F32),
        pltpu.VMEM((SEQ_PER_STEP * N_HEADS, SUBLANES, LANES), _F32),
    ]


_SPREAD_SPEC = pl.BlockSpec((LANES, 3 * SEC), lambda b, n: (0, 0))


def _sweep_specs(nchunk, reverse, n_wide):
    def chunk(n):
        return (nchunk - 1 - n) if reverse else n

    per_seq = []
    for k in range(SEQ_PER_STEP):
        def rb(b, n, k=k):
            return (b * SEQ_PER_STEP + k) * nchunk + chunk(n)
        per_seq += [pl.BlockSpec((CHUNK, SEC), functools.partial(lambda b, n, s, rb: (rb(b, n), s), s=s, rb=rb))
                    for s in range(6)]
        per_seq.append(pl.BlockSpec((CHUNK, LANES), functools.partial(lambda b, n, rb: (rb(b, n), 0), rb=rb)))
        per_seq.append(pl.BlockSpec((LANES, CHUNK), functools.partial(lambda b, n, rb: (0, rb(b, n)), rb=rb)))
        per_seq += [pl.BlockSpec((CHUNK, 2 * SEC), functools.partial(lambda b, n, rb: (rb(b, n), 0), rb=rb))
                    for _ in range(n_wide)]
    out = pl.BlockSpec((SEQ_PER_STEP, CHUNK, 2 * SEC), lambda b, n: (b, chunk(n), 0))
    tab = pl.BlockSpec((N_HEADS, CHUNK, HEAD_DIM), lambda b, n: (0, 0, 0))
    return per_seq, out, tab


def _sweep_bwd(P, GC, GR, qwb, kwb, gl, spread, batch, seq):
    nchunk = seq // CHUNK
    per_seq, out, tab = _sweep_specs(nchunk, True, 0)
    yb = pl.pallas_call(
        _sweep_bwd_kernel,
        grid=(batch // SEQ_PER_STEP, nchunk),
        in_specs=per_seq + [tab, tab, tab, _SPREAD_SPEC],
        out_specs=out,
        out_shape=jax.ShapeDtypeStruct((batch, seq, 2 * SEC), _F32),
        scratch_shapes=_state_scratch(),
        compiler_params=_params(("parallel", "arbitrary")),
        name="sweep_bwd",
    )(*([P] * 6 + [GC, GR]) * SEQ_PER_STEP, qwb, kwb, gl, spread)
    return yb.reshape(batch * seq, 2 * SEC)


def _sweep_fwd(P, G2, GC, GR, YB, rng, mng, dsym, qwf, kwf, gl, spread, batch, seq):
    nchunk = seq // CHUNK
    per_seq, out, tab = _sweep_specs(nchunk, False, 2)
    gain = pl.BlockSpec((1, SEC), lambda b, n: (0, 0))
    mixed = pl.pallas_call(
        _sweep_fwd_kernel,
        grid=(batch // SEQ_PER_STEP, nchunk),
        in_specs=per_seq + [gain, gain, tab, tab, tab, tab, _SPREAD_SPEC],
        out_specs=out,
        out_shape=jax.ShapeDtypeStruct((batch, seq, 2 * SEC), _MXU),
        scratch_shapes=_state_scratch(),
        compiler_params=_params(("parallel", "arbitrary")),
        name="sweep_fwd",
    )(*([P] * 6 + [GC, GR, YB, G2]) * SEQ_PER_STEP, rng, mng, dsym, qwf, kwf, gl, spread)
    return mixed.reshape(batch * seq, 2 * SEC)


def _outproj_kernel(mixed_ref, h0_ref, wo_ref, lg_ref, lb_ref, wr_ref, h1_hbm, aff_ref, hbuf, sem, *, nsteps):
    tm = mixed_ref.shape[0]
    parts = [slice(i * tm // 2, (i + 1) * tm // 2) for i in range(2)]
    zs = [ALPHA * h0_ref[p, :] + _dot(mixed_ref[p, :], wo_ref[...]) for p in parts]
    h1s = [_layer_norm(z, lg_ref[...], lb_ref[...]) for z in zs]
    _pipelined_writeback(hbuf, sem, h1_hbm, jnp.concatenate(h1s, axis=0), pl.program_id(0), nsteps, tm)
    logits = [_dot(h1.astype(_MXU), wr_ref[...]) for h1 in h1s]
    valid = lax.broadcasted_iota(jnp.int32, logits[0].shape, 1) < N_EXPERTS
    logits = [jnp.where(valid, lg, NEG_BIG) for lg in logits]
    es = [jnp.exp(lg - jnp.max(lg, axis=1, keepdims=True)) for lg in logits]
    for p, e in zip(parts, es):
        aff_ref[p, :] = jnp.where(valid, e / jnp.sum(e, axis=1, keepdims=True), 0.0)


def _outproj(mixed, h0, wo, ln_g, ln_b, wr, tm=PROJ_ROWS):
    T = mixed.shape[0]
    const = lambda i: (0, 0)
    return pl.pallas_call(
        functools.partial(_outproj_kernel, nsteps=T // tm),
        grid=(T // tm,),
        in_specs=[
            pl.BlockSpec((tm, D_MODEL), lambda i: (i, 0)),
            pl.BlockSpec((tm, D_MODEL), lambda i: (i, 0)),
            pl.BlockSpec((D_MODEL, D_MODEL), const),
            pl.BlockSpec((1, D_MODEL), const),
            pl.BlockSpec((1, D_MODEL), const),
            pl.BlockSpec((D_MODEL, LANES), const),
        ],
        out_specs=[
            pl.BlockSpec(memory_space=pl.ANY),
            pl.BlockSpec((tm, LANES), lambda i: (i, 0)),
        ],
        out_shape=[
            jax.ShapeDtypeStruct((T, ROW_TILES, LANES), _F32),
            jax.ShapeDtypeStruct((T, LANES), _F32),
        ],
        scratch_shapes=[pltpu.VMEM((2, tm, D_MODEL), _F32), pltpu.SemaphoreType.DMA((2,))],
        compiler_params=_params(("arbitrary",)),
        name="outproj",
    )(mixed, h0, wo, ln_g, ln_b, wr)


def _thresh_kernel(aff_ref, thr_ref, rem_ref, *, cap):
    rows = aff_ref.shape[0]
    aff = aff_ref[...]

    def count(pred):
        c = jnp.sum(pred.astype(jnp.int32).reshape(rows // SUBLANES, SUBLANES, LANES), axis=0)
        c = jnp.broadcast_to(jnp.sum(c, axis=0, keepdims=True), (SUBLANES, LANES))
        shift = LANES // 2
        while shift >= N_EXPERTS:
            c = c + pltpu.roll(c, shift, 1)
            shift //= 2
        return c

    def body(i, ans):
        cand = ans | jnp.left_shift(jnp.int32(1), 30 - i)
        c = count(aff >= lax.bitcast_convert_type(cand[0:1, :], _F32))
        return jnp.where(c >= cap, cand, ans)

    ans = lax.fori_loop(0, 31, body, jnp.zeros((SUBLANES, LANES), jnp.int32))
    thr = jnp.where(ans >= MIN_NORMAL_BITS, lax.bitcast_convert_type(ans, _F32), 0.0)
    thr_ref[...] = thr
    rem_ref[...] = cap - count(aff > thr[0:1, :])


def _thresh(affc, cap):
    return pl.pallas_call(
        functools.partial(_thresh_kernel, cap=cap),
        out_shape=[jax.ShapeDtypeStruct((SUBLANES, LANES), _F32),
                   jax.ShapeDtypeStruct((SUBLANES, LANES), jnp.int32)],
        compiler_params=pltpu.CompilerParams(vmem_limit_bytes=VMEM_LIMIT),
        name="thresh",
    )(affc)


SELECT_BLOCKS = 4
SELECT_ROWS = SELECT_BLOCKS * TOK_BLOCK


def _select_kernel(aff_ref, thr_ref, rem_ref, spread_ref, before_ref, gsel_ref, lidx_ref, cnt_ref, off_ref,
                   nsel_ref, neq_ref):
    @pl.when(pl.program_id(0) == 0)
    def _():
        nsel_ref[...] = jnp.zeros(nsel_ref.shape, _F32)
        neq_ref[...] = jnp.zeros(neq_ref.shape, _F32)

    aff = aff_ref[...]
    thr = thr_ref[0:1, :]
    rem = rem_ref[0:1, :].astype(_F32)
    valid = lax.broadcasted_iota(jnp.int32, aff.shape, 1) < N_EXPERTS
    before = before_ref[...]
    gt = (aff > thr) & valid
    eq = (aff == thr) & valid
    eq_before = _dot(before, eq.astype(_MXU)) + neq_ref[0:1, :]
    sel = gt | (eq & (eq_before < rem))
    pos = _dot(before, sel.astype(_MXU))
    neq_ref[0:1, :] = neq_ref[0:1, :] + jnp.sum(eq.astype(_F32), axis=0, keepdims=True)
    gsel_ref[...] = jnp.where(sel, aff, 0.0)

    slot = (lax.broadcasted_iota(jnp.int32, (TOK_BLOCK, N_EXPERTS * TOK_BLOCK), 1) % TOK_BLOCK).astype(_F32)
    tok = lax.broadcasted_iota(jnp.int32, (TOK_BLOCK, N_EXPERTS * TOK_BLOCK), 0)
    start = nsel_ref[0:1, :]
    taken = start
    for q in range(SELECT_BLOCKS):
        rows = slice(q * TOK_BLOCK, (q + 1) * TOK_BLOCK)
        sel_q = sel[rows, :]
        off_ref[q] = taken.astype(jnp.int32)
        cnt = jnp.sum(sel_q.astype(_F32), axis=0, keepdims=True)
        cnt_ref[q] = cnt.astype(jnp.int32)
        ranked = jnp.where(sel_q, pos[rows, :] - (taken - start), -1.0).astype(_MXU)
        spread = _dot(ranked, spread_ref[...])
        tok0 = (pl.program_id(0) * SELECT_BLOCKS + q) * TOK_BLOCK
        lidx_ref[q] = jnp.sum(jnp.where(spread == slot, tok + tok0, 0), axis=0, keepdims=True)
        taken = taken + cnt
    nsel_ref[0:1, :] = taken


def _select(aff, thr, rem, spread, before):
    T = aff.shape[0]
    nb = T // TOK_BLOCK
    const = lambda b: (0, 0)
    return pl.pallas_call(
        _select_kernel,
        grid=(nb // SELECT_BLOCKS,),
        in_specs=[
            pl.BlockSpec((SELECT_ROWS, LANES), lambda b: (b, 0)),
            pl.BlockSpec((SUBLANES, LANES), const),
            pl.BlockSpec((SUBLANES, LANES), const),
            pl.BlockSpec((LANES, N_EXPERTS * TOK_BLOCK), const),
            pl.BlockSpec((SELECT_ROWS, SELECT_ROWS), const),
        ],
        out_specs=[
            pl.BlockSpec((SELECT_ROWS, LANES), lambda b: (b, 0)),
            pl.BlockSpec((SELECT_BLOCKS, 1, N_EXPERTS * TOK_BLOCK), lambda b: (b, 0, 0)),
            pl.BlockSpec((SELECT_BLOCKS, 1, LANES), lambda b: (b, 0, 0)),
            pl.BlockSpec((SELECT_BLOCKS, 1, LANES), lambda b: (b, 0, 0)),
        ],
        out_shape=[
            jax.ShapeDtypeStruct((T, LANES), _F32),
            jax.ShapeDtypeStruct((nb, 1, N_EXPERTS * TOK_BLOCK), jnp.int32),
            jax.ShapeDtypeStruct((nb, 1, LANES), jnp.int32),
            jax.ShapeDtypeStruct((nb, 1, LANES), jnp.int32),
        ],
        scratch_shapes=[pltpu.VMEM((SUBLANES, LANES), _F32), pltpu.VMEM((SUBLANES, LANES), _F32)],
        compiler_params=_params(("arbitrary",)),
        name="select",
    )(aff, thr, rem, spread, before)


FFN_ROWS = 512


def _ffn_kernel(cnt_ref, lidx_hbm, h1_hbm, wg_ref, wu_ref, wd_ref, y_hbm,
                xbuf, ybuf, lidx_smem, walk_ref, sem_idx, sem_rows, sem_out, *, tm, per, nb):
    e = pl.program_id(0)
    j = pl.program_id(1)
    step = e * per + j
    slot = step % 2

    per_expert = nb * TOK_BLOCK

    def idx_copy(en):
        return pltpu.make_async_copy(lidx_hbm.at[en], lidx_smem.at[pl.ds((en % 2) * per_expert, per_expert)],
                                     sem_idx.at[en % 2])

    def issue_tile(en, dst_slot):
        list_base = (en % 2) * per_expert
        group0 = dst_slot * (tm // SUBLANES)
        sem = sem_rows.at[dst_slot]

        def cond(st):
            return st[0] < tm

        def body(st):
            n, b, r = st
            c = cnt_ref[en * nb + b]
            take = jnp.minimum(c - r, tm - n)
            src_minus_dst = list_base + b * TOK_BLOCK + r - n

            def one(m):
                t = lidx_smem[src_minus_dst + m]
                pltpu.make_async_copy(h1_hbm.at[t], _row_view(xbuf, m, group0), sem).start()

            _for_each(n, n + take, one)
            done = r + take >= c
            return n + take, jnp.where(done, b + 1, b), jnp.where(done, 0, r + take)

        _, b, r = lax.while_loop(cond, body, (jnp.int32(0), walk_ref[0], walk_ref[1]))
        walk_ref[0] = b
        walk_ref[1] = r

    def restart_walk():
        walk_ref[0] = jnp.int32(0)
        walk_ref[1] = jnp.int32(0)

    @pl.when(step == 0)
    def _():
        idx_copy(0).start()
        idx_copy(0).wait()
        restart_walk()
        issue_tile(0, 0)

    @pl.when((j == 0) & (e + 1 < N_EXPERTS))
    def _():
        idx_copy(e + 1).start()

    @pl.when(j + 1 < per)
    def _():
        issue_tile(e, 1 - slot)

    @pl.when((j + 1 == per) & (e + 1 < N_EXPERTS))
    def _():
        idx_copy(e + 1).wait()
        restart_walk()
        issue_tile(e + 1, 1 - slot)

    pltpu.make_async_copy(h1_hbm.at[pl.ds(0, tm)], h1_hbm.at[pl.ds(0, tm)], sem_rows.at[slot]).wait()
    x = _matrix_value(xbuf.at[pl.ds(slot * (tm // SUBLANES), tm // SUBLANES)]).astype(_MXU)
    g = _dot(x, wg_ref[0])
    u = _dot(x, wu_ref[0])
    hid = (g * _sigmoid(g) * u).astype(_MXU)
    _pipelined_writeback(ybuf, sem_out, y_hbm, _dot(hid, wd_ref[0]), step, N_EXPERTS * per, tm)


def _ffn(cnt_e, lidx_e, h1, wg, wu, wd, cap, tm):
    per = cap // tm
    nb = lidx_e.shape[1] // TOK_BLOCK
    return pl.pallas_call(
        functools.partial(_ffn_kernel, tm=tm, per=per, nb=nb),
        grid_spec=pltpu.PrefetchScalarGridSpec(
            num_scalar_prefetch=1,
            grid=(N_EXPERTS, per),
            in_specs=[
                pl.BlockSpec(memory_space=pl.ANY),
                pl.BlockSpec(memory_space=pl.ANY),
                pl.BlockSpec((1, D_MODEL, D_FF), lambda e, j, *_: (e, 0, 0)),
                pl.BlockSpec((1, D_MODEL, D_FF), lambda e, j, *_: (e, 0, 0)),
                pl.BlockSpec((1, D_FF, D_MODEL), lambda e, j, *_: (e, 0, 0)),
            ],
            out_specs=pl.BlockSpec(memory_space=pl.ANY),
            scratch_shapes=[
                pltpu.VMEM((2 * tm // SUBLANES, ROW_TILES, SUBLANES, LANES), _F32),
                pltpu.VMEM((2, tm, D_MODEL), _F32),
                pltpu.SMEM((2 * nb * TOK_BLOCK,), jnp.int32),
                pltpu.SMEM((2,), jnp.int32),
                pltpu.SemaphoreType.DMA((2,)),
                pltpu.SemaphoreType.DMA((2,)),
                pltpu.SemaphoreType.DMA((2,)),
            ],
        ),
        out_shape=jax.ShapeDtypeStruct((N_EXPERTS * cap, ROW_TILES, LANES), _F32),
        compiler_params=_params(("arbitrary", "arbitrary")),
        name="ffn",
    )(cnt_e, lidx_e, h1, wg, wu, wd)


SLOT_GROUPS = TOK_BLOCK // SUBLANES
COMBINE_ROWS = 32


def _combine_kernel(cnt_ref, off_ref, lidx_hbm, ye_hbm, h1_hbm, gsel_ref, lg_ref, lb_ref, y_ref,
                    slots_ref, hres_ref, lidx_smem, sem_idx, sem_rows, sem_res, *, cap, nb):
    b = pl.program_id(0)
    slot = b % 2

    per_block = N_EXPERTS * TOK_BLOCK

    def idx_copy(bn):
        return pltpu.make_async_copy(lidx_hbm.at[bn], lidx_smem.at[pl.ds((bn % 2) * per_block, per_block)],
                                     sem_idx.at[bn % 2])

    def res_copies(bn):
        return _matrix_copies(h1_hbm, bn * TOK_BLOCK, hres_ref.at[bn % 2], sem_res.at[bn % 2])

    def issue_block(bn):
        par = bn % 2
        sem = sem_rows.at[par]
        tok0 = bn * TOK_BLOCK
        for c in res_copies(bn):
            c.start()
        for e in range(N_EXPERTS):
            c = cnt_ref[bn * N_EXPERTS + e]
            base = e * cap + off_ref[bn * N_EXPERTS + e]
            list_minus_src = par * per_block + e * TOK_BLOCK - base
            group0 = (par * N_EXPERTS + e) * SLOT_GROUPS

            def one(src_row):
                t = lidx_smem[list_minus_src + src_row] - tok0
                pltpu.make_async_copy(ye_hbm.at[src_row], _row_view(slots_ref, t, group0), sem).start()

            _for_each(base, base + c, one)

    @pl.when(b == 0)
    def _():
        slots_ref[...] = jnp.zeros(slots_ref.shape, _F32)
        idx_copy(0).start()
        idx_copy(0).wait()
        issue_block(0)
        if nb > 1:
            idx_copy(1).start()

    @pl.when(b + 1 < nb)
    def _():
        idx_copy(b + 1).wait()
        issue_block(b + 1)

    @pl.when(b + 2 < nb)
    def _():
        idx_copy(b + 2).start()

    rows_in_flight = cnt_ref[b * N_EXPERTS]
    for e in range(1, N_EXPERTS):
        rows_in_flight = rows_in_flight + cnt_ref[b * N_EXPERTS + e]

    @pl.when(rows_in_flight > 0)
    def _():
        pltpu.make_async_copy(ye_hbm.at[pl.ds(0, rows_in_flight)], ye_hbm.at[pl.ds(0, rows_in_flight)],
                              sem_rows.at[slot]).wait()

    for c in res_copies(b):
        c.wait()

    groups = COMBINE_ROWS // SUBLANES
    for tg in range(TOK_BLOCK // COMBINE_ROWS):
        rows = slice(tg * COMBINE_ROWS, (tg + 1) * COMBINE_ROWS)
        acc = [ALPHA * hres_ref[slot, rows, j * LANES:(j + 1) * LANES] for j in range(ROW_TILES)]
        for e in range(N_EXPERTS):
            gate = jnp.broadcast_to(gsel_ref[rows, e:e + 1], (COMBINE_ROWS, LANES))
            blk = slots_ref[pl.ds((slot * N_EXPERTS + e) * SLOT_GROUPS + tg * groups, groups)]
            for j in range(ROW_TILES):
                acc[j] = acc[j] + gate * blk[:, j].reshape(COMBINE_ROWS, LANES)
        for j in range(ROW_TILES):
            y_ref[rows, j * LANES:(j + 1) * LANES] = acc[j]
    y_ref[...] = _layer_norm(y_ref[...], lg_ref[...], lb_ref[...])


def _combine(cnt_t, off_t, lidx, ye, gsel, h1, ln_g, ln_b, cap):
    T = h1.shape[0]
    nb = T // TOK_BLOCK
    return pl.pallas_call(
        functools.partial(_combine_kernel, cap=cap, nb=nb),
        grid_spec=pltpu.PrefetchScalarGridSpec(
            num_scalar_prefetch=2,
            grid=(nb,),
            in_specs=[
                pl.BlockSpec(memory_space=pl.ANY),
                pl.BlockSpec(memory_space=pl.ANY),
                pl.BlockSpec(memory_space=pl.ANY),
                pl.BlockSpec((TOK_BLOCK, LANES), lambda b, *_: (b, 0)),
                pl.BlockSpec((1, D_MODEL), lambda b, *_: (0, 0)),
                pl.BlockSpec((1, D_MODEL), lambda b, *_: (0, 0)),
            ],
            out_specs=pl.BlockSpec((TOK_BLOCK, D_MODEL), lambda b, *_: (b, 0)),
            scratch_shapes=[
                pltpu.VMEM((2 * N_EXPERTS * SLOT_GROUPS, ROW_TILES, SUBLANES, LANES), _F32),
                pltpu.VMEM((2, TOK_BLOCK, D_MODEL), _F32),
                pltpu.SMEM((2 * N_EXPERTS * TOK_BLOCK,), jnp.int32),
                pltpu.SemaphoreType.DMA((2,)),
                pltpu.SemaphoreType.DMA((2,)),
                pltpu.SemaphoreType.DMA((2,)),
            ],
        ),
        out_shape=jax.ShapeDtypeStruct((T, D_MODEL), _F32),
        compiler_params=_params(("arbitrary",)),
        name="combine",
    )(cnt_t, off_t, lidx, ye, h1, gsel, ln_g, ln_b)


def _tables(seq):
    half = HEAD_DIM // 2
    inv = 1.0 / (ROPE_BASE ** (jnp.arange(half, dtype=_F32) / half))
    ang = jnp.arange(seq, dtype=_F32)[:, None] * inv[None, :]
    cos = jnp.concatenate([jnp.cos(ang), jnp.cos(ang)], axis=1)
    sin = jnp.concatenate([-jnp.sin(ang), jnp.sin(ang)], axis=1)
    log_g = jnp.log1p(-jnp.exp2(-5.0 - jnp.arange(N_HEADS, dtype=_F32)))[:, None, None]
    pos = jnp.arange(CHUNK, dtype=_F32)
    rows = lambda f: jnp.broadcast_to(jnp.exp(log_g * f[None, :, None]), (N_HEADS, CHUNK, HEAD_DIM))
    dsym = jnp.exp(log_g * jnp.abs(pos[:, None] - pos[None, :])[None])
    tabs = dict(
        cos=cos, sin=sin, dsym=dsym,
        qwf=rows(pos + 1.0), kwf=rows(CHUNK - 1.0 - pos),
        qwb=rows(CHUNK - pos), kwb=rows(pos),
        gl=rows(jnp.full((CHUNK,), float(CHUNK), _F32)),
    )
    spread = lambda chans: (jnp.arange(LANES)[:, None] == jnp.repeat(jnp.asarray(chans), LANES)[None, :]).astype(_MXU)
    tabs["spread"] = spread(range(N_EXPERTS))
    r = jnp.arange(SELECT_ROWS)
    tabs["before"] = (r[None, :] < r[:, None]).astype(_MXU)
    heads = lambda ch: list(range(ch, ch + N_HEADS))
    tabs["spread_f"] = spread(heads(CH_LI_F) + heads(CH_CUM_F) + heads(CH_MAX_F))
    tabs["spread_b"] = spread(heads(CH_LI_B) + heads(CH_CUM_B) + heads(CH_MAX_B))
    return tabs


def _trunk(x, w):
    batch, seq, _ = x.shape
    T = batch * seq
    nb = T // TOK_BLOCK
    cap = CAP_FACTOR * T // N_EXPERTS
    t = _tables(seq)
    h0, P, G2, GC, GR = _inproj(x.reshape(T, D_MODEL), seq, w["ln_in_g"], w["ln_in_b"], w["w_main"],
                                w["b_main"], w["wg"], w["bg"], t["cos"], t["sin"])
    YB = _sweep_bwd(P, GC, GR, t["qwb"], t["kwb"], t["gl"], t["spread_b"], batch, seq)
    mixed = _sweep_fwd(P, G2, GC, GR, YB, w["ret_g"], w["mlstm_g"], t["dsym"], t["qwf"], t["kwf"], t["gl"],
                       t["spread_f"], batch, seq)
    h1, aff = _outproj(mixed, h0, w["w_o"], w["ln1_g"], w["ln1_b"], w["w_r"])
    affc = aff[:, :N_EXPERTS].reshape(T // SUBLANES, LANES)
    thr, rem = _thresh(affc, cap)
    gsel, lidx, cnt, off = _select(aff, thr, rem, t["spread"], t["before"])
    cnt2 = cnt.reshape(nb, LANES)[:, :N_EXPERTS]
    off2 = off.reshape(nb, LANES)[:, :N_EXPERTS]
    lidx_e = lidx.reshape(nb, N_EXPERTS, TOK_BLOCK).transpose(1, 0, 2).reshape(N_EXPERTS, nb * TOK_BLOCK)
    ye = _ffn(cnt2.T.reshape(-1), lidx_e, h1, w["w_gate"], w["w_up"], w["w_down"], cap, min(FFN_ROWS, cap))
    y = _combine(cnt2.reshape(-1), off2.reshape(-1), lidx.reshape(nb, N_EXPERTS * TOK_BLOCK), ye, gsel, h1,
                 w["ln2_g"], w["ln2_b"], cap)
    return y.reshape(batch, seq, D_MODEL)


def _prep_weights(ln_in_g, ln_in_b, w_in, b_in, ret_norm_g, mlstm_norm_g, w_o, ln1_g, ln1_b, w_router,
                  w_gate, w_up, w_down, ln2_g, ln2_b):
    main = 8 * SEC
    ngate = 4 * N_HEADS
    row = lambda v: v.reshape(1, -1).astype(_F32)
    wg = jnp.pad(w_in[0][:, main:main + ngate], ((0, 0), (0, LANES - ngate)))
    bg = jnp.pad(b_in[0][main:main + ngate], (0, LANES - ngate))
    return dict(
        ln_in_g=row(ln_in_g), ln_in_b=row(ln_in_b),
        w_main=w_in[0][:, :main].astype(_MXU), b_main=row(b_in[0][:main]),
        wg=wg.astype(_MXU), bg=row(bg),
        ret_g=row(ret_norm_g[0]), mlstm_g=row(mlstm_norm_g[0]),
        w_o=w_o[0].astype(_MXU), ln1_g=row(ln1_g[0]), ln1_b=row(ln1_b[0]),
        w_r=jnp.pad(w_router[0], ((0, 0), (0, LANES - N_EXPERTS))).astype(_MXU),
        w_gate=w_gate[0].astype(_MXU), w_up=w_up[0].astype(_MXU), w_down=w_down[0].astype(_MXU),
        ln2_g=row(ln2_g[0]), ln2_b=row(ln2_b[0]),
    )


def kernel(x_prompt, x_sample, ln_in_g, ln_in_b, w_in, b_in, ret_norm_g, mlstm_norm_g, w_o, ln1_g, ln1_b,
           w_router, w_gate, w_up, w_down, ln2_g, ln2_b):
    w = _prep_weights(ln_in_g, ln_in_b, w_in, b_in, ret_norm_g, mlstm_norm_g, w_o, ln1_g, ln1_b, w_router,
                      w_gate, w_up, w_down, ln2_g, ln2_b)
    return (_trunk(x_prompt, w), _trunk(x_sample, w))
```
